```python
import math
import jax, jax.numpy as jnp
from jax import lax
import numpy as np

D_MODEL = 2048
BATCH = 4
SEQ = 2048
DEPTH = 1

HY_WIDTH = D_MODEL // 2
HY_ORDER = 2
HY_SHORT_CONV = 3
HY_EMB_DIM = 33
HY_FILTER_HIDDEN = 64
HY_FAST_DECAY_PCT = 0.3
HY_SLOW_DECAY_PCT = 1.5
HY_DECAY_TARGET = 1e-2
HY_MIN_DECAY = math.log(HY_DECAY_TARGET) / HY_SLOW_DECAY_PCT
HY_MAX_DECAY = math.log(HY_DECAY_TARGET) / HY_FAST_DECAY_PCT
CF_WIDTH = D_MODEL // 2
CF_KERNEL = 31
FFN_HIDDEN = -(-8 * D_MODEL // (3 * 256)) * 256
N_IN = 3 * HY_WIDTH + 2 * CF_WIDTH + 2 * D_MODEL
EPS = 1e-6

kernel_name = "hyena_conformer_gated_hybrid_block"


def rms_norm(x, g):
    xf = x.astype(jnp.float32)
    y = xf * lax.rsqrt(jnp.mean(xf * xf, axis=-1, keepdims=True) + EPS)
    return (y * g.astype(jnp.float32)).astype(x.dtype)


def layer_norm(x, g, b):
    xf = x.astype(jnp.float32)
    mu = jnp.mean(xf, axis=-1, keepdims=True)
    var = jnp.mean(jnp.square(xf - mu), axis=-1, keepdims=True)
    y = (xf - mu) * lax.rsqrt(var + EPS)
    return (y * g.astype(jnp.float32) + b.astype(jnp.float32)).astype(x.dtype)


def depthwise_conv(u, w, b):
    pad = w.shape[0] // 2
    y = lax.conv_general_dilated(
        u, w[:, None, :].astype(u.dtype), window_strides=(1,), padding=[(pad, pad)],
        dimension_numbers=("NWC", "WIO", "NWC"), feature_group_count=u.shape[-1])
    return y + b.astype(u.dtype)


def hyena_filters_freq(L, w1, b1, fr1, w2, b2, fr2, w3):
    f32 = jnp.float32
    t = jnp.linspace(0.0, 1.0, L, dtype=f32)[:, None]
    bands = (HY_EMB_DIM - 1) // 2
    w = 2.0 * math.pi * jnp.arange(L, dtype=f32)[:, None] / L
    f = jnp.linspace(1e-4, bands - 1, bands, dtype=f32)[None, :]
    z = jnp.concatenate([t, jnp.cos(f * w), -jnp.sin(f * w)], axis=-1)
    h = jnp.sin(fr1.astype(f32) * (z @ w1.astype(f32) + b1.astype(f32)))
    h = jnp.sin(fr2.astype(f32) * (h @ w2.astype(f32) + b2.astype(f32)))
    k = (h @ w3.astype(f32)).reshape(L, HY_ORDER, 2, HY_WIDTH)
    deltas = jnp.linspace(HY_MIN_DECAY, HY_MAX_DECAY, HY_WIDTH, dtype=f32)
    k = k * jnp.exp(-t * jnp.abs(deltas))[:, None, None, :]
    k_fwd = k[:, :, 0]
    k_bwd = k[1:, :, 1]
    l1 = jnp.sum(jnp.abs(k_fwd), axis=0) + jnp.sum(jnp.abs(k_bwd), axis=0)
    two_sided = jnp.concatenate(
        [k_fwd, jnp.zeros((1, HY_ORDER, HY_WIDTH), f32), k_bwd[::-1]], axis=0) / l1
    return jnp.fft.rfft(two_sided, axis=0)


def fft_long_conv(u, kf, bias):
    L = u.shape[1]
    uf32 = u.astype(jnp.float32)
    uf = jnp.fft.rfft(uf32, n=2 * L, axis=1)
    y = jnp.fft.irfft(uf * kf[None], n=2 * L, axis=1)[:, :L]
    return (y + uf32 * bias.astype(jnp.float32)).astype(u.dtype)


def setup_inputs(seed: int = 0) -> dict:
    key = jax.random.key(seed)
    ks = jax.random.split(key, 26)
    f32 = jnp.float32

    def nrm(k, shape, scale):
        return jax.random.normal(k, (DEPTH,) + shape, f32) * scale

    def gain(k, shape):
        return 1.0 + nrm(k, shape, 0.01)

    return {
        "x": jax.random.normal(ks[0], (BATCH, SEQ, D_MODEL), f32),
        "mix_pre_g": gain(ks[1], (D_MODEL,)),
        "w_in": nrm(ks[2], (D_MODEL, N_IN), D_MODEL ** -0.5),
        "hy_conv_w": nrm(ks[3], (HY_SHORT_CONV, 3 * HY_WIDTH), HY_SHORT_CONV ** -0.5),
        "hy_conv_b": nrm(ks[4], (3 * HY_WIDTH,), 0.01),
        "hy_filt_w1": nrm(ks[5], (HY_EMB_DIM, HY_FILTER_HIDDEN), HY_EMB_DIM ** -0.5),
        "hy_filt_b1": nrm(ks[6], (HY_FILTER_HIDDEN,), 0.01),
        "hy_filt_fr1": gain(ks[7], (HY_FILTER_HIDDEN,)),
        "hy_filt_w2": nrm(ks[8], (HY_FILTER_HIDDEN, HY_FILTER_HIDDEN), HY_FILTER_HIDDEN ** -0.5),
        "hy_filt_b2": nrm(ks[9], (HY_FILTER_HIDDEN,), 0.01),
        "hy_filt_fr2": gain(ks[10], (HY_FILTER_HIDDEN,)),
        "hy_filt_w3": nrm(ks[11], (HY_FILTER_HIDDEN, HY_ORDER * 2 * HY_WIDTH), HY_FILTER_HIDDEN ** -0.5),
        "hy_bias": nrm(ks[12], (HY_ORDER, HY_WIDTH), 1.0),
        "hy_proj": nrm(ks[13], (HY_WIDTH, D_MODEL), HY_WIDTH ** -0.5),
        "cf_dw_w": nrm(ks[14], (CF_KERNEL, CF_WIDTH), CF_KERNEL ** -0.5),
        "cf_dw_b": nrm(ks[15], (CF_WIDTH,), 0.01),
        "cf_ln_g": gain(ks[16], (CF_WIDTH,)),
        "cf_ln_b": nrm(ks[17], (CF_WIDTH,), 0.01),
        "cf_proj": nrm(ks[18], (CF_WIDTH, D_MODEL), CF_WIDTH ** -0.5),
        "w_out": nrm(ks[19], (D_MODEL, D_MODEL), D_MODEL ** -0.5),
        "mix_post_g": gain(ks[20], (D_MODEL,)),
        "ffn_pre_g": gain(ks[21], (D_MODEL,)),
        "ffn_w_gu": nrm(ks[22], (D_MODEL, 2 * FFN_HIDDEN), D_MODEL ** -0.5),
        "ffn_w_down": nrm(ks[23], (FFN_HIDDEN, D_MODEL), FFN_HIDDEN ** -0.5),
        "ffn_post_g": gain(ks[24], (D_MODEL,)),
    }


def reference(x, mix_pre_g, w_in, hy_conv_w, hy_conv_b, hy_filt_w1, hy_filt_b1, hy_filt_fr1,
              hy_filt_w2, hy_filt_b2, hy_filt_fr2, hy_filt_w3, hy_bias, hy_proj,
              cf_dw_w, cf_dw_b, cf_ln_g, cf_ln_b, cf_proj, w_out, mix_post_g,
              ffn_pre_g, ffn_w_gu, ffn_w_down, ffn_post_g):
    L = x.shape[1]
    s1 = 3 * HY_WIDTH
    s2 = s1 + CF_WIDTH
    s3 = s2 + CF_WIDTH
    s4 = s3 + D_MODEL
    for l in range(DEPTH):
        h = rms_norm(x, mix_pre_g[l])
        proj = h @ w_in[l]
        hy_in, cf_a, cf_b = proj[..., :s1], proj[..., s1:s2], proj[..., s2:s3]
        gate_a, gate_b = proj[..., s3:s4], proj[..., s4:]

        hy_in = depthwise_conv(hy_in, hy_conv_w[l], hy_conv_b[l])
        v, x1, x2 = jnp.split(hy_in, 3, axis=-1)
        kf = hyena_filters_freq(L, hy_filt_w1[l], hy_filt_b1[l], hy_filt_fr1[l],
                                hy_filt_w2[l], hy_filt_b2[l], hy_filt_fr2[l], hy_filt_w3[l])
        z = x1 * fft_long_conv(v, kf[:, 0], hy_bias[l, 0])
        y_a = x2 * fft_long_conv(z, kf[:, 1], hy_bias[l, 1])

        u = cf_a * jax.nn.sigmoid(cf_b)
        u = depthwise_conv(u, cf_dw_w[l], cf_dw_b[l])
        y_b = jax.nn.silu(layer_norm(u, cf_ln_g[l], cf_ln_b[l]))

        merged = (jax.nn.sigmoid(gate_a) * (y_a @ hy_proj[l])
                  + jax.nn.sigmoid(gate_b) * (y_b @ cf_proj[l]))
        x = x + rms_norm(merged @ w_out[l], mix_post_g[l])

        h = rms_norm(x, ffn_pre_g[l])
        gu = h @ ffn_w_gu[l]
        gate, up = gu[..., :FFN_HIDDEN], gu[..., FFN_HIDDEN:]
        x = x + rms_norm((jax.nn.silu(gate) * up) @ ffn_w_down[l], ffn_post_g[l])
    return x
```

```python
import functools
import math

import jax
import jax.numpy as jnp
from jax import lax
from jax.experimental import pallas as pl
from jax.experimental.pallas import tpu as pltpu

F32 = jnp.float32
BF16 = jnp.bfloat16
EPS = 1e-6

HY_ORDER = 2
HY_EMB_BANDS = 16
HY_DECAY_TARGET = 1e-2
HY_MIN_DECAY = math.log(HY_DECAY_TARGET) / 1.5
HY_MAX_DECAY = math.log(HY_DECAY_TARGET) / 0.3
HY_SHORT = 3
CF_KERNEL = 31
CF_PAD = 16

V7X_VMEM_LIMIT_BYTES = 56 * 1024 * 1024


def _params(*sem):
    return pltpu.CompilerParams(dimension_semantics=sem, vmem_limit_bytes=V7X_VMEM_LIMIT_BYTES)


def _dot(a, b):
    return jnp.dot(a, b, preferred_element_type=F32)


def _sigmoid(x):
    return 1.0 / (1.0 + jnp.exp(-x))


def _dft_kernel(cm_ref, smf_ref, smg_ref, rc_ref, rs_ref, *, tm, n, chunk):
    i = pl.program_id(0)
    scale = 2.0 * math.pi / (2 * n)

    @pl.when(i == 0)
    def _():
        def base(j, carry):
            r0 = pl.multiple_of(j * chunk, chunk)
            r = r0 + lax.broadcasted_iota(jnp.int32, (chunk, n), 0)
            c = lax.broadcasted_iota(jnp.int32, (chunk, n), 1)
            th = ((r * c) & (2 * n - 1)).astype(F32) * scale
            rc_ref[pl.ds(r0, chunk), :] = jnp.cos(th)
            rs_ref[pl.ds(r0, chunk), :] = jnp.sin(th)
            return carry
        lax.fori_loop(0, tm // chunk, base, 0)

    k0 = i * tm
    c1 = lax.broadcasted_iota(jnp.int32, (1, n), 1)
    th0 = ((k0 * c1) & (2 * n - 1)).astype(F32) * scale
    c0 = jnp.cos(th0)
    s0 = jnp.sin(th0)

    def tile(j, carry):
        r0 = pl.multiple_of(j * chunk, chunk)
        rc = rc_ref[pl.ds(r0, chunk), :]
        rs = rs_ref[pl.ds(r0, chunk), :]
        cm = rc * c0 - rs * s0
        sm = rs * c0 + rc * s0
        row = k0 + r0 + lax.broadcasted_iota(jnp.int32, (chunk, n), 0)
        col = lax.broadcasted_iota(jnp.int32, (chunk, n), 1)
        alt_col = (1 - 2 * (col & 1)).astype(F32)
        alt_row = (1 - 2 * (row & 1)).astype(F32)
        cm_ref[pl.ds(r0, chunk), :] = cm.astype(BF16)
        smf_ref[pl.ds(r0, chunk), :] = jnp.where(row == 0, alt_col, sm).astype(BF16)
        smg_ref[pl.ds(r0, chunk), :] = jnp.where(col == 0, alt_row, sm).astype(BF16)
        return carry
    lax.fori_loop(0, tm // chunk, tile, 0)


def _dft_tables(n, tm=256, chunk=32):
    out = jax.ShapeDtypeStruct((n, n), BF16)
    spec = pl.BlockSpec((tm, n), lambda i: (i, 0))
    return pl.pallas_call(
        functools.partial(_dft_kernel, tm=tm, n=n, chunk=chunk),
        grid=(n // tm,),
        out_specs=[spec, spec, spec],
        out_shape=[out, out, out],
        scratch_shapes=[pltpu.VMEM((tm, n), F32), pltpu.VMEM((tm, n), F32)],
        compiler_params=_params("arbitrary"),
        name="dft_tables",
    )()


def _in_proj_kernel(x_ref, g_ref, w_ref, o_ref, h_ref):
    @pl.when(pl.program_id(1) == 0)
    def _():
        x = x_ref[...]
        ms = jnp.mean(x * x, axis=-1, keepdims=True)
        h_ref[...] = (x * lax.rsqrt(ms + EPS) * g_ref[...]).astype(BF16)

    o_ref[...] = _dot(h_ref[...], w_ref[...]).astype(o_ref.dtype)


def _in_proj(x2d, g, w, tm=512, tn=1024):
    m, d = x2d.shape
    n = w.shape[1]
    return pl.pallas_call(
        _in_proj_kernel,
        grid=(m // tm, n // tn),
        in_specs=[pl.BlockSpec((tm, d), lambda i, j: (i, 0)),
                  pl.BlockSpec((1, d), lambda i, j: (0, 0)),
                  pl.BlockSpec((d, tn), lambda i, j: (0, j))],
        out_specs=pl.BlockSpec((tm, tn), lambda i, j: (i, j)),
        out_shape=jax.ShapeDtypeStruct((m, n), BF16),
        scratch_shapes=[pltpu.VMEM((tm, d), BF16)],
        compiler_params=_params("parallel", "arbitrary"),
        name="in_proj",
    )(x2d, g, w)


def _conv3_kernel(p_ref, w_ref, b_ref, o_ref):
    x = p_ref[0].astype(F32)
    rows = x.shape[0]
    row = lax.broadcasted_iota(jnp.int32, x.shape, 0)
    prev = jnp.where(row == 0, 0.0, pltpu.roll(x, 1, 0))
    nxt = jnp.where(row == rows - 1, 0.0, pltpu.roll(x, rows - 1, 0))
    w = w_ref[...]
    y = prev * w[0:1, :] + x * w[1:2, :] + nxt * w[2:3, :] + b_ref[...]
    o_ref[0] = y.astype(o_ref.dtype)


def _hy_conv3(proj3, w, b, width, ct=256):
    bsz, seq, _ = proj3.shape
    return pl.pallas_call(
        _conv3_kernel,
        grid=(bsz, width // ct),
        in_specs=[pl.BlockSpec((1, seq, ct), lambda i, j: (i, 0, j)),
                  pl.BlockSpec((HY_SHORT, ct), lambda i, j: (0, j)),
                  pl.BlockSpec((1, ct), lambda i, j: (0, j))],
        out_specs=pl.BlockSpec((1, seq, ct), lambda i, j: (i, 0, j)),
        out_shape=jax.ShapeDtypeStruct((bsz, seq, width), BF16),
        compiler_params=_params("parallel", "parallel"),
        name="hy_conv3",
    )(proj3, w, b)


def _filt_mlp_kernel(w1t_ref, w1c_ref, w1s_ref, b1_ref, fr1_ref, w2_ref, b2_ref, fr2_ref,
                     w3f_ref, w3b_ref, s_ref, d_ref, nyq_ref, *, seq, width, ct):
    jc = pl.program_id(1)
    n = lax.broadcasted_iota(jnp.int32, (seq, 1), 0).astype(F32)
    t = n / (seq - 1)
    wang = 2.0 * math.pi * n / seq
    band = lax.broadcasted_iota(jnp.int32, (1, HY_EMB_BANDS), 1).astype(F32)
    f = 1e-4 + band * ((HY_EMB_BANDS - 1 - 1e-4) / (HY_EMB_BANDS - 1))
    fw = f * wang
    pre1 = t * w1t_ref[...] + _dot(jnp.cos(fw), w1c_ref[...]) - _dot(jnp.sin(fw), w1s_ref[...]) + b1_ref[...]
    h1 = jnp.sin(fr1_ref[...] * pre1)
    h2 = jnp.sin(fr2_ref[...] * (_dot(h1, w2_ref[...]) + b2_ref[...]))
    ch = (jc * ct + lax.broadcasted_iota(jnp.int32, (1, ct), 1)).astype(F32)
    delta = HY_MIN_DECAY + ch * ((HY_MAX_DECAY - HY_MIN_DECAY) / (width - 1))
    decay = jnp.exp(-t * jnp.abs(delta))
    kf = _dot(h2, w3f_ref[...]) * decay
    kb = _dot(h2, w3b_ref[...]) * decay
    row = lax.broadcasted_iota(jnp.int32, (seq, ct), 0)
    kb = jnp.where(row == 0, 0.0, kb)
    l1 = jnp.sum(jnp.abs(kf), axis=0, keepdims=True) + jnp.sum(jnp.abs(kb), axis=0, keepdims=True)
    inv = 1.0 / l1
    s = (kf + kb) * inv
    s_ref[...] = s.astype(BF16)
    d_ref[...] = ((kf - kb) * inv).astype(BF16)
    alt = (1 - 2 * (row & 1)).astype(F32)
    nyq_ref[...] = jnp.sum(s * alt, axis=0, keepdims=True) * (1.0 / (2 * seq))


def _filt_mlp(seq, width, w1, b1, fr1, w2, b2, fr2, w3, ct=256):
    hid = w2.shape[0]
    nct = width // ct
    small = lambda shape: pl.BlockSpec(shape, lambda o, j: (0, 0))
    out_spec = pl.BlockSpec((seq, ct), lambda o, j: (0, o * nct + j))
    return pl.pallas_call(
        functools.partial(_filt_mlp_kernel, seq=seq, width=width, ct=ct),
        grid=(HY_ORDER, nct),
        in_specs=[small((1, hid)), small((HY_EMB_BANDS, hid)), small((HY_EMB_BANDS, hid)),
                  small((1, hid)), small((1, hid)), small((hid, hid)), small((1, hid)), small((1, hid)),
                  pl.BlockSpec((hid, ct), lambda o, j: (0, o * 2 * nct + j)),
                  pl.BlockSpec((hid, ct), lambda o, j: (0, o * 2 * nct + nct + j))],
        out_specs=[out_spec, out_spec, pl.BlockSpec((1, ct), lambda o, j: (0, o * nct + j))],
        out_shape=[jax.ShapeDtypeStruct((seq, HY_ORDER * width), BF16),
                   jax.ShapeDtypeStruct((seq, HY_ORDER * width), BF16),
                   jax.ShapeDtypeStruct((1, HY_ORDER * width), F32)],
        compiler_params=_params("parallel", "parallel"),
        name="filt_mlp",
    )(w1[0:1], w1[1:1 + HY_EMB_BANDS], w1[1 + HY_EMB_BANDS:], b1, fr1, w2, b2, fr2, w3, w3)


def _filt_dft_kernel(cm_ref, sm_ref, s_ref, d_ref, tr_ref, ti_ref, *, tm, seq):
    row = pl.program_id(1) * tm + lax.broadcasted_iota(jnp.int32, tr_ref.shape, 0)
    wk = jnp.where(row == 0, 1.0, 2.0) * (1.0 / (2 * seq))
    tr_ref[...] = _dot(cm_ref[...], s_ref[...]) * wk
    ti_ref[...] = jnp.where(row == 0, 0.0, -_dot(sm_ref[...], d_ref[...]) * wk)


def _filt_dft(cm, smf, s, d, tm=512, tn=512):
    seq = cm.shape[0]
    cols = s.shape[1]
    return pl.pallas_call(
        functools.partial(_filt_dft_kernel, tm=tm, seq=seq),
        grid=(cols // tn, seq // tm),
        in_specs=[pl.BlockSpec((tm, seq), lambda j, i: (i, 0)),
                  pl.BlockSpec((tm, seq), lambda j, i: (i, 0)),
                  pl.BlockSpec((seq, tn), lambda j, i: (0, j)),
                  pl.BlockSpec((seq, tn), lambda j, i: (0, j))],
        out_specs=[pl.BlockSpec((tm, tn), lambda j, i: (i, j)),
                   pl.BlockSpec((tm, tn), lambda j, i: (i, j))],
        out_shape=[jax.ShapeDtypeStruct((seq, cols), F32), jax.ShapeDtypeStruct((seq, cols), F32)],
        compiler_params=_params("parallel", "parallel"),
        name="filt_dft",
    )(cm, smf, s, d)


def _hy_fwd_kernel(cm_ref, sm_ref, u_ref, tr_ref, ti_ref, tn_ref, y_ref, *, tm):
    u = u_ref[0]
    a = _dot(cm_ref[...], u)
    b = _dot(sm_ref[...], u)
    tr = tr_ref[...]
    ti = ti_ref[...]
    row = pl.program_id(0) * tm + lax.broadcasted_iota(jnp.int32, a.shape, 0)
    y_ref[0, 0] = (a * tr + b * ti).astype(BF16)
    y_ref[0, 1] = jnp.where(row == 0, b * tn_ref[...], b * tr - a * ti).astype(BF16)


def _hy_fwd(cm, smf, u3, u_col, tr, ti, tnyq, t_col, width, tm=512):
    bsz, seq, _ = u3.shape
    return pl.pallas_call(
        functools.partial(_hy_fwd_kernel, tm=tm),
        grid=(seq // tm, bsz),
        in_specs=[pl.BlockSpec((tm, seq), lambda i, b: (i, 0)),
                  pl.BlockSpec((tm, seq), lambda i, b: (i, 0)),
                  pl.BlockSpec((1, seq, width), lambda i, b: (b, 0, u_col)),
                  pl.BlockSpec((tm, width), lambda i, b: (i, t_col)),
                  pl.BlockSpec((tm, width), lambda i, b: (i, t_col)),
                  pl.BlockSpec((1, width), lambda i, b: (0, t_col))],
        out_specs=pl.BlockSpec((1, 2, tm, width), lambda i, b: (b, 0, i, 0)),
        out_shape=jax.ShapeDtypeStruct((bsz, 2, seq, width), BF16),
        compiler_params=_params("parallel", "arbitrary"),
        name="hy_fwd",
    )(cm, smf, u3, tr, ti, tnyq)


def _hy_inv_kernel(cm_ref, sm_ref, y_ref, g_ref, u_ref, bias_ref, o_ref):
    y = _dot(cm_ref[...], y_ref[0, 0]) + _dot(sm_ref[...], y_ref[0, 1])
    u = u_ref[0].astype(F32)
    o_ref[0] = (g_ref[0].astype(F32) * (y + u * bias_ref[...])).astype(o_ref.dtype)


def _hy_inv(cm, smg, y4, g3, g_col, u3, u_col, bias, width, tm=512):
    bsz, _, seq, _ = y4.shape
    return pl.pallas_call(
        _hy_inv_kernel,
        grid=(bsz, seq // tm),
        in_specs=[pl.BlockSpec((tm, seq), lambda b, i: (i, 0)),
                  pl.BlockSpec((tm, seq), lambda b, i: (i, 0)),
                  pl.BlockSpec((1, 2, seq, width), lambda b, i: (b, 0, 0, 0)),
                  pl.BlockSpec((1, tm, width), lambda b, i: (b, i, g_col)),
                  pl.BlockSpec((1, tm, width), lambda b, i: (b, i, u_col)),
                  pl.BlockSpec((1, width), lambda b, i: (0, 0))],
        out_specs=pl.BlockSpec((1, tm, width), lambda b, i: (b, i, 0)),
        out_shape=jax.ShapeDtypeStruct((bsz, seq, width), BF16),
        compiler_params=_params("parallel", "arbitrary"),
        name="hy_inv",
    )(cm, smg, y4, g3, u3, bias)


def _cf_kernel(a_ref, b_ref, w_ref, cb_ref, lg_ref, lb_ref, o_ref, u_ref, *, seq, rows):
    width = u_ref.shape[1]
    zeros = jnp.zeros((CF_PAD, width), F32)
    u_ref[0:CF_PAD, :] = zeros
    u_ref[seq + CF_PAD:seq + 2 * CF_PAD, :] = zeros

    def fill(i, carry):
        r = pl.multiple_of(i * rows, rows)
        a = a_ref[0, pl.ds(r, rows), :].astype(F32)
        b = b_ref[0, pl.ds(r, rows), :].astype(F32)
        u_ref[pl.ds(CF_PAD + r, rows), :] = a * _sigmoid(b)
        return carry
    lax.fori_loop(0, seq // rows, fill, 0)

    half = CF_KERNEL // 2

    def conv(i, carry):
        r = pl.multiple_of(i * rows, rows)
        acc = jnp.zeros((rows, width), F32) + cb_ref[...]
        wrows = rows + 2 * CF_PAD
        win = u_ref[pl.ds(r, wrows), :]
        for s in range(8):
            sh = win if s == 0 else pltpu.roll(win, wrows - s, 0)
            for j in range(CF_KERNEL):
                off = CF_PAD - half + j
                if off % 8 == s:
                    acc = acc + w_ref[j:j + 1, :] * sh[off - s:off - s + rows, :]
        mu = jnp.mean(acc, axis=-1, keepdims=True)
        cen = acc - mu
        var = jnp.mean(cen * cen, axis=-1, keepdims=True)
        y = cen * lax.rsqrt(var + EPS) * lg_ref[...] + lb_ref[...]
        o_ref[0, pl.ds(r, rows), :] = (y * _sigmoid(y)).astype(o_ref.dtype)
        return carry
    lax.fori_loop(0, seq // rows, conv, 0)


def _cf_conv(proj3, a_col, b_col, w, cb, lg, lb, width, rows=32):
    bsz, seq, _ = proj3.shape
    vec = pl.BlockSpec((1, width), lambda b: (0, 0))
    return pl.pallas_call(
        functools.partial(_cf_kernel, seq=seq, rows=rows),
        grid=(bsz,),
        in_specs=[pl.BlockSpec((1, seq, width), lambda b: (b, 0, a_col)),
                  pl.BlockSpec((1, seq, width), lambda b: (b, 0, b_col)),
                  pl.BlockSpec((CF_KERNEL, width), lambda b: (0, 0)),
                  vec, vec, vec],
        out_specs=pl.BlockSpec((1, seq, width), lambda b: (b, 0, 0)),
        out_shape=jax.ShapeDtypeStruct((bsz, seq, width), BF16),
        scratch_shapes=[pltpu.VMEM((seq + 2 * CF_PAD, width), F32)],
        compiler_params=_params("parallel"),
        name="cf_conv",
    )(proj3, proj3, w, cb, lg, lb)


def _merge_kernel(ya_ref, yb_ref, ga0_ref, ga1_ref, gb0_ref, gb1_ref, x_ref, pa_ref, pb_ref, wo_ref,
                  g1_ref, g2_ref, x1_ref, h2_ref):
    a = _dot(ya_ref[...], pa_ref[...])
    b = _dot(yb_ref[...], pb_ref[...])
    w = ga0_ref.shape[1]
    m0 = _sigmoid(ga0_ref[...].astype(F32)) * a[:, :w] + _sigmoid(gb0_ref[...].astype(F32)) * b[:, :w]
    m1 = _sigmoid(ga1_ref[...].astype(F32)) * a[:, w:] + _sigmoid(gb1_ref[...].astype(F32)) * b[:, w:]
    o = _dot(m0.astype(BF16), wo_ref[0:w, :]) + _dot(m1.astype(BF16), wo_ref[w:2 * w, :])
    x1 = x_ref[...] + o * lax.rsqrt(jnp.mean(o * o, axis=-1, keepdims=True) + EPS) * g1_ref[...]
    x1_ref[...] = x1
    h2_ref[...] = (x1 * lax.rsqrt(jnp.mean(x1 * x1, axis=-1, keepdims=True) + EPS) * g2_ref[...]).astype(BF16)


def _merge(ya, yb, proj, ga_col, gb_col, x2d, pa, pb, wo, g1, g2, tm=256):
    m, d = x2d.shape
    wa = ya.shape[1]
    assert d == 2 * wa
    const = lambda shape: pl.BlockSpec(shape, lambda i: (0, 0))
    gate = lambda col: pl.BlockSpec((tm, wa), lambda i: (i, col))
    return pl.pallas_call(
        _merge_kernel,
        grid=(m // tm,),
        in_specs=[pl.BlockSpec((tm, wa), lambda i: (i, 0)),
                  pl.BlockSpec((tm, wa), lambda i: (i, 0)),
                  gate(ga_col), gate(ga_col + 1), gate(gb_col), gate(gb_col + 1),
                  pl.BlockSpec((tm, d), lambda i: (i, 0)),
                  const((wa, d)), const((wa, d)), const((d, d)), const((1, d)), const((1, d))],
        out_specs=[pl.BlockSpec((tm, d), lambda i: (i, 0)), pl.BlockSpec((tm, d), lambda i: (i, 0))],
        out_shape=[jax.ShapeDtypeStruct((m, d), F32), jax.ShapeDtypeStruct((m, d), BF16)],
        compiler_params=_params("parallel"),
        name="merge",
    )(ya, yb, proj, proj, proj, proj, x2d, pa, pb, wo, g1, g2)


def _ffn_kernel(h_ref, wg_ref, wu_ref, wd_ref, x1_ref, g_ref, o_ref, acc_ref):
    j = pl.program_id(1)

    @pl.when(j == 0)
    def _():
        acc_ref[...] = jnp.zeros_like(acc_ref)

    h = h_ref[...]
    gate = _dot(h, wg_ref[...])
    up = _dot(h, wu_ref[...])
    act = (gate * _sigmoid(gate) * up).astype(BF16)
    acc_ref[...] += _dot(act, wd_ref[...])

    @pl.when(j == pl.num_programs(1) - 1)
    def _():
        a = acc_ref[...]
        o_ref[...] = x1_ref[...] + a * lax.rsqrt(jnp.mean(a * a, axis=-1, keepdims=True) + EPS) * g_ref[...]


def _ffn(h2, wgu, wd, x1, g, tm=512, th=512):
    m, d = h2.shape
    hidden = wd.shape[0]
    nh = hidden // th
    return pl.pallas_call(
        _ffn_kernel,
        grid=(m // tm, nh),
        in_specs=[pl.BlockSpec((tm, d), lambda i, j: (i, 0)),
                  pl.BlockSpec((d, th), lambda i, j: (0, j)),
                  pl.BlockSpec((d, th), lambda i, j: (0, nh + j)),
                  pl.BlockSpec((th, d), lambda i, j: (j, 0)),
                  pl.BlockSpec((tm, d), lambda i, j: (i, 0)),
                  pl.BlockSpec((1, d), lambda i, j: (0, 0))],
        out_specs=pl.BlockSpec((tm, d), lambda i, j: (i, 0)),
        out_shape=jax.ShapeDtypeStruct((m, d), F32),
        scratch_shapes=[pltpu.VMEM((tm, d), F32)],
        compiler_params=_params("parallel", "arbitrary"),
        name="ffn",
    )(h2, wgu, wgu, wd, x1, g)


def kernel(x, mix_pre_g, w_in, hy_conv_w, hy_conv_b, hy_filt_w1, hy_filt_b1, hy_filt_fr1, hy_filt_w2,
           hy_filt_b2, hy_filt_fr2, hy_filt_w3, hy_bias, hy_proj, cf_dw_w, cf_dw_b, cf_ln_g, cf_ln_b,
           cf_proj, w_out, mix_post_g, ffn_pre_g, ffn_w_gu, ffn_w_down, ffn_post_g):
    bsz, seq, d = x.shape
    depth = w_in.shape[0]
    hw = hy_proj.shape[1]
    cw = cf_proj.shape[1]
    assert hw == cw and d % hw == 0
    row = lambda v: v.reshape(1, -1)

    cm, smf, smg = _dft_tables(seq)
    x2d = x.reshape(bsz * seq, d)
    for l in range(depth):
        proj = _in_proj(x2d, row(mix_pre_g[l]), w_in[l].astype(BF16))
        proj3 = proj.reshape(bsz, seq, -1)
        cf_a_col, cf_b_col = 3, 4
        ga_col, gb_col = 5, 5 + d // hw

        hy = _hy_conv3(proj3, hy_conv_w[l], row(hy_conv_b[l]), 3 * hw)
        s, dm, tnyq = _filt_mlp(seq, hw, hy_filt_w1[l], row(hy_filt_b1[l]), row(hy_filt_fr1[l]),
                                hy_filt_w2[l], row(hy_filt_b2[l]), row(hy_filt_fr2[l]), hy_filt_w3[l])
        tr, ti = _filt_dft(cm, smf, s, dm)
        y1 = _hy_fwd(cm, smf, hy, 0, tr, ti, tnyq, 0, hw)
        z = _hy_inv(cm, smg, y1, hy, 1, hy, 0, row(hy_bias[l, 0]), hw)
        y2 = _hy_fwd(cm, smf, z, 0, tr, ti, tnyq, 1, hw)
        y_a = _hy_inv(cm, smg, y2, hy, 2, z, 0, row(hy_bias[l, 1]), hw)

        y_b = _cf_conv(proj3, cf_a_col, cf_b_col, cf_dw_w[l], row(cf_dw_b[l]), row(cf_ln_g[l]),
                       row(cf_ln_b[l]), cw)

        x1, h2 = _merge(y_a.reshape(bsz * seq, hw), y_b.reshape(bsz * seq, cw), proj, ga_col, gb_col, x2d,
                        hy_proj[l].astype(BF16), cf_proj[l].astype(BF16), w_out[l].astype(BF16),
                        row(mix_post_g[l]), row(ffn_pre_g[l]))
        x2d = _ffn(h2, ffn_w_gu[l].astype(BF16), ffn_w_down[l].astype(BF16), x1, row(ffn_post_g[l]))
    return x2d.reshape(bsz, seq, d)
```

```python
import functools
import math

import jax
import jax.numpy as jnp
from jax import lax
from jax.experimental import pallas as pl
from jax.experimental.pallas import tpu as pltpu

F32 = jnp.float32
BF16 = jnp.bfloat16
EPS = 1e-6

HY_ORDER = 2
HY_EMB_BANDS = 16
HY_DECAY_TARGET = 1e-2
HY_MIN_DECAY = math.log(HY_DECAY_TARGET) / 1.5
HY_MAX_DECAY = math.log(HY_DECAY_TARGET) / 0.3
HY_SHORT = 3
CF_KERNEL = 31
CF_PAD = 16

V7X_VMEM_LIMIT_BYTES = 56 * 1024 * 1024


def _params(*sem):
    return pltpu.CompilerParams(dimension_semantics=sem, vmem_limit_bytes=V7X_VMEM_LIMIT_BYTES)


def _dot(a, b):
    return jnp.dot(a, b, preferred_element_type=F32)


def _sigmoid(x):
    return 1.0 / (1.0 + jnp.exp(-x))


def _dft_kernel(w_ref, cm_ref, smf_ref, smg_ref, wbf_ref, rc_ref, rs_ref, *, tm, n, chunk):
    i = pl.program_id(0)
    scale = 2.0 * math.pi / (2 * n)
    wbf_ref[...] = w_ref[...].astype(BF16)

    @pl.when(i == 0)
    def _():
        def base(j, carry):
            r0 = pl.multiple_of(j * chunk, chunk)
            r = r0 + lax.broadcasted_iota(jnp.int32, (chunk, n), 0)
            c = lax.broadcasted_iota(jnp.int32, (chunk, n), 1)
            th = ((r * c) & (2 * n - 1)).astype(F32) * scale
            rc_ref[pl.ds(r0, chunk), :] = jnp.cos(th)
            rs_ref[pl.ds(r0, chunk), :] = jnp.sin(th)
            return carry
        lax.fori_loop(0, tm // chunk, base, 0)

    k0 = i * tm
    c1 = lax.broadcasted_iota(jnp.int32, (1, n), 1)
    th0 = ((k0 * c1) & (2 * n - 1)).astype(F32) * scale
    c0 = jnp.cos(th0)
    s0 = jnp.sin(th0)

    def tile(j, carry):
        r0 = pl.multiple_of(j * chunk, chunk)
        rc = rc_ref[pl.ds(r0, chunk), :]
        rs = rs_ref[pl.ds(r0, chunk), :]
        cm = rc * c0 - rs * s0
        sm = rs * c0 + rc * s0
        row = k0 + r0 + lax.broadcasted_iota(jnp.int32, (chunk, n), 0)
        col = lax.broadcasted_iota(jnp.int32, (chunk, n), 1)
        alt_col = (1 - 2 * (col & 1)).astype(F32)
        alt_row = (1 - 2 * (row & 1)).astype(F32)
        cm_ref[pl.ds(r0, chunk), :] = cm.astype(BF16)
        smf_ref[pl.ds(r0, chunk), :] = jnp.where(row == 0, alt_col, sm).astype(BF16)
        smg_ref[pl.ds(r0, chunk), :] = jnp.where(col == 0, alt_row, sm).astype(BF16)
        return carry
    lax.fori_loop(0, tm // chunk, tile, 0)


def _dft_tables(n, w, w_rows, tm=256, chunk=32):
    steps = n // tm
    wr = w_rows // steps
    out = jax.ShapeDtypeStruct((n, n), BF16)
    spec = pl.BlockSpec((tm, n), lambda i: (i, 0))
    wspec = pl.BlockSpec((wr, w.shape[1]), lambda i: (i, 0))
    return pl.pallas_call(
        functools.partial(_dft_kernel, tm=tm, n=n, chunk=chunk),
        grid=(steps,),
        in_specs=[wspec],
        out_specs=[spec, spec, spec, wspec],
        out_shape=[out, out, out, jax.ShapeDtypeStruct((w_rows, w.shape[1]), BF16)],
        scratch_shapes=[pltpu.VMEM((tm, n), F32), pltpu.VMEM((tm, n), F32)],
        compiler_params=_params("arbitrary"),
        name="dft_tables",
    )(w)


def _in_proj_kernel(x_ref, g_ref, wa_ref, wb_ref, o_ref, h_ref):
    @pl.when(pl.program_id(1) == 0)
    def _():
        x = x_ref[...]
        ms = jnp.mean(x * x, axis=-1, keepdims=True)
        h_ref[...] = (x * lax.rsqrt(ms + EPS) * g_ref[...]).astype(BF16)

    ka = wa_ref.shape[0]
    acc = _dot(h_ref[:, 0:ka], wa_ref[...]) + _dot(h_ref[:, ka:], wb_ref[...])
    o_ref[...] = acc.astype(o_ref.dtype)


def _in_proj(x2d, g, wa, wb, tm=1024, tn=1024):
    m, d = x2d.shape
    n = wa.shape[1]
    assert wa.shape[0] + wb.shape[0] == d
    return pl.pallas_call(
        _in_proj_kernel,
        grid=(m // tm, n // tn),
        in_specs=[pl.BlockSpec((tm, d), lambda i, j: (i, 0)),
                  pl.BlockSpec((1, d), lambda i, j: (0, 0)),
                  pl.BlockSpec((wa.shape[0], tn), lambda i, j: (0, j)),
                  pl.BlockSpec((wb.shape[0], tn), lambda i, j: (0, j))],
        out_specs=pl.BlockSpec((tm, tn), lambda i, j: (i, j)),
        out_shape=jax.ShapeDtypeStruct((m, n), BF16),
        scratch_shapes=[pltpu.VMEM((tm, d), BF16)],
        compiler_params=_params("parallel", "arbitrary"),
        name="in_proj",
    )(x2d, g, wa, wb)


def _conv3_kernel(p_ref, w_ref, b_ref, o_ref):
    x = p_ref[0].astype(F32)
    rows = x.shape[0]
    row = lax.broadcasted_iota(jnp.int32, x.shape, 0)
    prev = jnp.where(row == 0, 0.0, pltpu.roll(x, 1, 0))
    nxt = jnp.where(row == rows - 1, 0.0, pltpu.roll(x, rows - 1, 0))
    w = w_ref[...]
    y = prev * w[0:1, :] + x * w[1:2, :] + nxt * w[2:3, :] + b_ref[...]
    o_ref[0] = y.astype(o_ref.dtype)


def _hy_conv3(proj3, w, b, width, ct=512):
    bsz, seq, _ = proj3.shape
    return pl.pallas_call(
        _conv3_kernel,
        grid=(bsz, width // ct),
        in_specs=[pl.BlockSpec((1, seq, ct), lambda i, j: (i, 0, j)),
                  pl.BlockSpec((HY_SHORT, ct), lambda i, j: (0, j)),
                  pl.BlockSpec((1, ct), lambda i, j: (0, j))],
        out_specs=pl.BlockSpec((1, seq, ct), lambda i, j: (i, 0, j)),
        out_shape=jax.ShapeDtypeStruct((bsz, seq, width), BF16),
        compiler_params=_params("parallel", "parallel"),
        name="hy_conv3",
    )(proj3, w, b)


def _filt_mlp_kernel(w1t_ref, w1c_ref, w1s_ref, b1_ref, fr1_ref, w2_ref, b2_ref, fr2_ref,
                     w3f_ref, w3b_ref, w_ref, s_ref, d_ref, nyq_ref, wbf_ref, h2_ref, *, seq, width, ct):
    jc = pl.program_id(1)
    wbf_ref[...] = w_ref[...].astype(BF16)
    n = lax.broadcasted_iota(jnp.int32, (seq, 1), 0).astype(F32)
    t = n / (seq - 1)

    @pl.when((pl.program_id(0) == 0) & (jc == 0))
    def _():
        wang = 2.0 * math.pi * n / seq
        band = lax.broadcasted_iota(jnp.int32, (1, HY_EMB_BANDS), 1).astype(F32)
        f = 1e-4 + band * ((HY_EMB_BANDS - 1 - 1e-4) / (HY_EMB_BANDS - 1))
        fw = f * wang
        pre1 = (t * w1t_ref[...] + _dot(jnp.cos(fw), w1c_ref[...]) - _dot(jnp.sin(fw), w1s_ref[...])
                + b1_ref[...])
        h1 = jnp.sin(fr1_ref[...] * pre1)
        h2_ref[...] = jnp.sin(fr2_ref[...] * (_dot(h1, w2_ref[...]) + b2_ref[...]))

    h2 = h2_ref[...]
    ch = (jc * ct + lax.broadcasted_iota(jnp.int32, (1, ct), 1)).astype(F32)
    delta = HY_MIN_DECAY + ch * ((HY_MAX_DECAY - HY_MIN_DECAY) / (width - 1))
    decay = jnp.exp(-t * jnp.abs(delta))
    kf = _dot(h2, w3f_ref[...]) * decay
    kb = _dot(h2, w3b_ref[...]) * decay
    row = lax.broadcasted_iota(jnp.int32, (seq, ct), 0)
    kb = jnp.where(row == 0, 0.0, kb)
    l1 = jnp.sum(jnp.abs(kf), axis=0, keepdims=True) + jnp.sum(jnp.abs(kb), axis=0, keepdims=True)
    inv = 1.0 / l1
    s = (kf + kb) * inv
    s_ref[...] = s.astype(BF16)
    d_ref[...] = ((kf - kb) * inv).astype(BF16)
    alt = (1 - 2 * (row & 1)).astype(F32)
    nyq_ref[...] = jnp.sum(s * alt, axis=0, keepdims=True) * (1.0 / (2 * seq))


def _filt_mlp(seq, width, w1, b1, fr1, w2, b2, fr2, w3, w, w_row0, ct=256):
    hid = w2.shape[0]
    nct = width // ct
    steps = HY_ORDER * nct
    w_rows = w.shape[0] - w_row0
    wr = w_rows // steps
    assert w_row0 % wr == 0
    small = lambda shape: pl.BlockSpec(shape, lambda o, j: (0, 0))
    out_spec = pl.BlockSpec((seq, ct), lambda o, j: (0, o * nct + j))
    return pl.pallas_call(
        functools.partial(_filt_mlp_kernel, seq=seq, width=width, ct=ct),
        grid=(HY_ORDER, nct),
        in_specs=[small((1, hid)), small((HY_EMB_BANDS, hid)), small((HY_EMB_BANDS, hid)),
                  small((1, hid)), small((1, hid)), small((hid, hid)), small((1, hid)), small((1, hid)),
                  pl.BlockSpec((hid, ct), lambda o, j: (0, o * 2 * nct + j)),
                  pl.BlockSpec((hid, ct), lambda o, j: (0, o * 2 * nct + nct + j)),
                  pl.BlockSpec((wr, w.shape[1]), lambda o, j: (w_row0 // wr + o * nct + j, 0))],
        out_specs=[out_spec, out_spec, pl.BlockSpec((1, ct), lambda o, j: (0, o * nct + j)),
                   pl.BlockSpec((wr, w.shape[1]), lambda o, j: (o * nct + j, 0))],
        out_shape=[jax.ShapeDtypeStruct((seq, HY_ORDER * width), BF16),
                   jax.ShapeDtypeStruct((seq, HY_ORDER * width), BF16),
                   jax.ShapeDtypeStruct((1, HY_ORDER * width), F32),
                   jax.ShapeDtypeStruct((w_rows, w.shape[1]), BF16)],
        scratch_shapes=[pltpu.VMEM((seq, hid), F32)],
        compiler_params=_params("arbitrary", "arbitrary"),
        name="filt_mlp",
    )(w1[0:1], w1[1:1 + HY_EMB_BANDS], w1[1 + HY_EMB_BANDS:], b1, fr1, w2, b2, fr2, w3, w3, w)


def _filt_dft_kernel(cm_ref, sm_ref, s_ref, d_ref, tr_ref, ti_ref, *, tm, seq):
    row = pl.program_id(1) * tm + lax.broadcasted_iota(jnp.int32, tr_ref.shape, 0)
    wk = jnp.where(row == 0, 1.0, 2.0) * (1.0 / (2 * seq))
    tr_ref[...] = _dot(cm_ref[...], s_ref[...]) * wk
    ti_ref[...] = jnp.where(row == 0, 0.0, -_dot(sm_ref[...], d_ref[...]) * wk)


def _filt_dft(cm, smf, s, d, tm=512, tn=512):
    seq = cm.shape[0]
    cols = s.shape[1]
    return pl.pallas_call(
        functools.partial(_filt_dft_kernel, tm=tm, seq=seq),
        grid=(cols // tn, seq // tm),
        in_specs=[pl.BlockSpec((tm, seq), lambda j, i: (i, 0)),
                  pl.BlockSpec((tm, seq), lambda j, i: (i, 0)),
                  pl.BlockSpec((seq, tn), lambda j, i: (0, j)),
                  pl.BlockSpec((seq, tn), lambda j, i: (0, j))],
        out_specs=[pl.BlockSpec((tm, tn), lambda j, i: (i, j)),
                   pl.BlockSpec((tm, tn), lambda j, i: (i, j))],
        out_shape=[jax.ShapeDtypeStruct((seq, cols), F32), jax.ShapeDtypeStruct((seq, cols), F32)],
        compiler_params=_params("parallel", "parallel"),
        name="filt_dft",
    )(cm, smf, s, d)


def _hy_fwd_kernel(cm_ref, sm_ref, u_ref, tr_ref, ti_ref, tn_ref, y_ref, *, tm):
    u = u_ref[0]
    a = _dot(cm_ref[...], u)
    b = _dot(sm_ref[...], u)
    tr = tr_ref[...]
    ti = ti_ref[...]
    row = pl.program_id(0) * tm + lax.broadcasted_iota(jnp.int32, a.shape, 0)
    y_ref[0, 0] = (a * tr + b * ti).astype(BF16)
    y_ref[0, 1] = jnp.where(row == 0, b * tn_ref[...], b * tr - a * ti).astype(BF16)


def _hy_fwd(cm, smf, u3, u_col, tr, ti, tnyq, t_col, width, tm=512):
    bsz, seq, _ = u3.shape
    return pl.pallas_call(
        functools.partial(_hy_fwd_kernel, tm=tm),
        grid=(seq // tm, bsz),
        in_specs=[pl.BlockSpec((tm, seq), lambda i, b: (i, 0)),
                  pl.BlockSpec((tm, seq), lambda i, b: (i, 0)),
                  pl.BlockSpec((1, seq, width), lambda i, b: (b, 0, u_col)),
                  pl.BlockSpec((tm, width), lambda i, b: (i, t_col)),
                  pl.BlockSpec((tm, width), lambda i, b: (i, t_col)),
                  pl.BlockSpec((1, width), lambda i, b: (0, t_col))],
        out_specs=pl.BlockSpec((1, 2, tm, width), lambda i, b: (b, 0, i, 0)),
        out_shape=jax.ShapeDtypeStruct((bsz, 2, seq, width), BF16),
        compiler_params=_params("parallel", "arbitrary"),
        name="hy_fwd",
    )(cm, smf, u3, tr, ti, tnyq)


def _hy_inv_kernel(cm_ref, sm_ref, y_ref, g_ref, u_ref, bias_ref, o_ref):
    y = _dot(cm_ref[...], y_ref[0, 0]) + _dot(sm_ref[...], y_ref[0, 1])
    u = u_ref[0].astype(F32)
    o_ref[0] = (g_ref[0].astype(F32) * (y + u * bias_ref[...])).astype(o_ref.dtype)


def _hy_inv(cm, smg, y4, g3, g_col, u3, u_col, bias, width, tm=512):
    bsz, _, seq, _ = y4.shape
    return pl.pallas_call(
        _hy_inv_kernel,
        grid=(bsz, seq // tm),
        in_specs=[pl.BlockSpec((tm, seq), lambda b, i: (i, 0)),
                  pl.BlockSpec((tm, seq), lambda b, i: (i, 0)),
                  pl.BlockSpec((1, 2, seq, width), lambda b, i: (b, 0, 0, 0)),
                  pl.BlockSpec((1, tm, width), lambda b, i: (b, i, g_col)),
                  pl.BlockSpec((1, tm, width), lambda b, i: (b, i, u_col)),
                  pl.BlockSpec((1, width), lambda b, i: (0, 0))],
        out_specs=pl.BlockSpec((1, tm, width), lambda b, i: (b, i, 0)),
        out_shape=jax.ShapeDtypeStruct((bsz, seq, width), BF16),
        compiler_params=_params("parallel", "arbitrary"),
        name="hy_inv",
    )(cm, smg, y4, g3, u3, bias)


def _cf_kernel(*refs, seq, rows, step_rows, n_cast):
    a_ref, b_ref, w_ref, cb_ref, lg_ref, lb_ref = refs[:6]
    cast_in = refs[6:6 + n_cast]
    o_ref = refs[6 + n_cast]
    cast_out = refs[7 + n_cast:7 + 2 * n_cast]
    u_ref = refs[7 + 2 * n_cast]
    width = u_ref.shape[1]

    for src, dst in zip(cast_in, cast_out):
        dst[...] = src[...].astype(dst.dtype)

    @pl.when(pl.program_id(1) == 0)
    def _():
        zeros = jnp.zeros((CF_PAD, width), F32)
        u_ref[0:CF_PAD, :] = zeros
        u_ref[seq + CF_PAD:seq + 2 * CF_PAD, :] = zeros

        def fill(i, carry):
            r = pl.multiple_of(i * rows, rows)
            a = a_ref[0, pl.ds(r, rows), :].astype(F32)
            b = b_ref[0, pl.ds(r, rows), :].astype(F32)
            u_ref[pl.ds(CF_PAD + r, rows), :] = a * _sigmoid(b)
            return carry
        lax.fori_loop(0, seq // rows, fill, 0)

    half = CF_KERNEL // 2
    base = pl.program_id(1) * step_rows

    def conv(i, carry):
        ro = pl.multiple_of(i * rows, rows)
        r = pl.multiple_of(base + ro, rows)
        acc = jnp.zeros((rows, width), F32) + cb_ref[...]
        wrows = rows + 2 * CF_PAD
        win = u_ref[pl.ds(r, wrows), :]
        for s in range(8):
            sh = win if s == 0 else pltpu.roll(win, wrows - s, 0)
            for j in range(CF_KERNEL):
                off = CF_PAD - half + j
                if off % 8 == s:
                    acc = acc + w_ref[j:j + 1, :] * sh[off - s:off - s + rows, :]
        mu = jnp.mean(acc, axis=-1, keepdims=True)
        cen = acc - mu
        var = jnp.mean(cen * cen, axis=-1, keepdims=True)
        y = cen * lax.rsqrt(var + EPS) * lg_ref[...] + lb_ref[...]
        o_ref[0, pl.ds(ro, rows), :] = (y * _sigmoid(y)).astype(o_ref.dtype)
        return carry
    lax.fori_loop(0, step_rows // rows, conv, 0)


def _cf_conv(proj3, a_col, b_col, w, cb, lg, lb, width, casts, rows=32, step_rows=128):
    bsz, seq, _ = proj3.shape
    nsteps = seq // step_rows
    total = bsz * nsteps
    vec = pl.BlockSpec((1, width), lambda b, j: (0, 0))

    def cast_spec(c):
        nblk = next(n for n in (total, total // 2, total // 4) if c.shape[0] % (16 * n) == 0)
        every = total // nblk
        return pl.BlockSpec((c.shape[0] // nblk, c.shape[1]), lambda b, j: ((b * nsteps + j) // every, 0))
    cast_specs = [cast_spec(c) for c in casts]
    outs = pl.pallas_call(
        functools.partial(_cf_kernel, seq=seq, rows=rows, step_rows=step_rows, n_cast=len(casts)),
        grid=(bsz, nsteps),
        in_specs=[pl.BlockSpec((1, seq, width), lambda b, j: (b, 0, a_col)),
                  pl.BlockSpec((1, seq, width), lambda b, j: (b, 0, b_col)),
                  pl.BlockSpec((CF_KERNEL, width), lambda b, j: (0, 0)),
                  vec, vec, vec] + cast_specs,
        out_specs=[pl.BlockSpec((1, step_rows, width), lambda b, j: (b, j, 0))] + cast_specs,
        out_shape=[jax.ShapeDtypeStruct((bsz, seq, width), BF16)]
                  + [jax.ShapeDtypeStruct(c.shape, BF16) for c in casts],
        scratch_shapes=[pltpu.VMEM((seq + 2 * CF_PAD, width), F32)],
        compiler_params=_params("arbitrary", "arbitrary"),
        name="cf_conv",
    )(proj3, proj3, w, cb, lg, lb, *casts)
    return outs[0], outs[1:]


def _merge_kernel(ya_ref, yb_ref, ga0_ref, ga1_ref, gb0_ref, gb1_ref, x_ref, pa_ref, pb_ref, wo_ref,
                  g1_ref, x1_ref):
    a = _dot(ya_ref[...], pa_ref[...])
    b = _dot(yb_ref[...], pb_ref[...])
    w = ga0_ref.shape[1]
    m0 = _sigmoid(ga0_ref[...].astype(F32)) * a[:, :w] + _sigmoid(gb0_ref[...].astype(F32)) * b[:, :w]
    m1 = _sigmoid(ga1_ref[...].astype(F32)) * a[:, w:] + _sigmoid(gb1_ref[...].astype(F32)) * b[:, w:]
    o = _dot(m0.astype(BF16), wo_ref[0:w, :]) + _dot(m1.astype(BF16), wo_ref[w:2 * w, :])
    x1_ref[...] = x_ref[...] + o * lax.rsqrt(jnp.mean(o * o, axis=-1, keepdims=True) + EPS) * g1_ref[...]


def _merge(ya, yb, proj, ga_col, gb_col, x2d, pa, pb, wo, g1, tm=256):
    m, d = x2d.shape
    wa = ya.shape[1]
    assert d == 2 * wa
    const = lambda shape: pl.BlockSpec(shape, lambda i: (0, 0))
    gate = lambda col: pl.BlockSpec((tm, wa), lambda i: (i, col))
    return pl.pallas_call(
        _merge_kernel,
        grid=(m // tm,),
        in_specs=[pl.BlockSpec((tm, wa), lambda i: (i, 0)),
                  pl.BlockSpec((tm, wa), lambda i: (i, 0)),
                  gate(ga_col), gate(ga_col + 1), gate(gb_col), gate(gb_col + 1),
                  pl.BlockSpec((tm, d), lambda i: (i, 0)),
                  const((wa, d)), const((wa, d)), const((d, d)), const((1, d))],
        out_specs=pl.BlockSpec((tm, d), lambda i: (i, 0)),
        out_shape=jax.ShapeDtypeStruct((m, d), F32),
        compiler_params=_params("parallel"),
        name="merge",
    )(ya, yb, proj, proj, proj, proj, x2d, pa, pb, wo, g1)


def _ffn_kernel(x1_ref, gpre_ref, wg_ref, wu_ref, wd_ref, gpost_ref, o_ref, h_ref):
    j = pl.program_id(1)

    @pl.when(j == 0)
    def _():
        x1 = x1_ref[...]
        ms = jnp.mean(x1 * x1, axis=-1, keepdims=True)
        h_ref[...] = (x1 * lax.rsqrt(ms + EPS) * gpre_ref[...]).astype(BF16)
        o_ref[...] = jnp.zeros_like(o_ref)

    h = h_ref[...]
    gate = _dot(h, wg_ref[...])
    up = _dot(h, wu_ref[...])
    act = (gate * _sigmoid(gate) * up).astype(BF16)
    o_ref[...] += _dot(act, wd_ref[...])

    @pl.when(j == pl.num_programs(1) - 1)
    def _():
        a = o_ref[...]
        o_ref[...] = x1_ref[...] + a * lax.rsqrt(jnp.mean(a * a, axis=-1, keepdims=True) + EPS) * gpost_ref[...]


def _ffn(x1, gpre, wgu, wd, gpost, tm=1024, th=256):
    m, d = x1.shape
    hidden = wd.shape[0]
    nh = hidden // th
    return pl.pallas_call(
        _ffn_kernel,
        grid=(m // tm, nh),
        in_specs=[pl.BlockSpec((tm, d), lambda i, j: (i, 0)),
                  pl.BlockSpec((1, d), lambda i, j: (0, 0)),
                  pl.BlockSpec((d, th), lambda i, j: (0, j)),
                  pl.BlockSpec((d, th), lambda i, j: (0, nh + j)),
                  pl.BlockSpec((th, d), lambda i, j: (j, 0)),
                  pl.BlockSpec((1, d), lambda i, j: (0, 0))],
        out_specs=pl.BlockSpec((tm, d), lambda i, j: (i, 0)),
        out_shape=jax.ShapeDtypeStruct((m, d), F32),
        scratch_shapes=[pltpu.VMEM((tm, d), BF16)],
        compiler_params=_params("parallel", "arbitrary"),
        name="ffn",
    )(x1, gpre, wgu, wgu, wd, gpost)


def kernel(x, mix_pre_g, w_in, hy_conv_w, hy_conv_b, hy_filt_w1, hy_filt_b1, hy_filt_fr1, hy_filt_w2,
           hy_filt_b2, hy_filt_fr2, hy_filt_w3, hy_bias, hy_proj, cf_dw_w, cf_dw_b, cf_ln_g, cf_ln_b,
           cf_proj, w_out, mix_post_g, ffn_pre_g, ffn_w_gu, ffn_w_down, ffn_post_g):
    bsz, seq, d = x.shape
    depth = w_in.shape[0]
    hw = hy_proj.shape[1]
    cw = cf_proj.shape[1]
    assert hw == cw and d % hw == 0
    row = lambda v: v.reshape(1, -1)

    assert depth == 1
    x2d = x.reshape(bsz * seq, d)
    for l in range(depth):
        cm, smf, smg, w_in_a = _dft_tables(seq, w_in[l], d // 2)
        s, dm, tnyq, w_in_b = _filt_mlp(seq, hw, hy_filt_w1[l], row(hy_filt_b1[l]), row(hy_filt_fr1[l]),
                                        hy_filt_w2[l], row(hy_filt_b2[l]), row(hy_filt_fr2[l]),
                                        hy_filt_w3[l], w_in[l], d // 2)
        proj = _in_proj(x2d, row(mix_pre_g[l]), w_in_a, w_in_b)
        proj3 = proj.reshape(bsz, seq, -1)
        cf_a_col, cf_b_col = 3, 4
        ga_col, gb_col = 5, 5 + d // hw

        hy = _hy_conv3(proj3, hy_conv_w[l], row(hy_conv_b[l]), 3 * hw)
        tr, ti = _filt_dft(cm, smf, s, dm)
        y1 = _hy_fwd(cm, smf, hy, 0, tr, ti, tnyq, 0, hw)
        z = _hy_inv(cm, smg, y1, hy, 1, hy, 0, row(hy_bias[l, 0]), hw)
        y2 = _hy_fwd(cm, smf, z, 0, tr, ti, tnyq, 1, hw)
        y_a = _hy_inv(cm, smg, y2, hy, 2, z, 0, row(hy_bias[l, 1]), hw)

        y_b, (hy_proj_bf, cf_proj_bf, w_out_bf, w_gu_bf, w_down_bf) = _cf_conv(
            proj3, cf_a_col, cf_b_col, cf_dw_w[l], row(cf_dw_b[l]), row(cf_ln_g[l]), row(cf_ln_b[l]), cw,
            [hy_proj[l], cf_proj[l], w_out[l], ffn_w_gu[l], ffn_w_down[l]])

        x1 = _merge(y_a.reshape(bsz * seq, hw), y_b.reshape(bsz * seq, cw), proj, ga_col, gb_col, x2d,
                    hy_proj_bf, cf_proj_bf, w_out_bf, row(mix_post_g[l]))
        x2d = _ffn(x1, row(ffn_pre_g[l]), w_gu_bf, w_down_bf, row(ffn_post_g[l]))
    return x2d.reshape(bsz, seq, d)
```

```python
import functools
import math

import jax
import jax.numpy as jnp
from jax import lax
from jax.experimental import pallas as pl
from jax.experimental.pallas import tpu as pltpu

F32 = jnp.float32
BF16 = jnp.bfloat16
EPS = 1e-6

HY_ORDER = 2
HY_EMB_BANDS = 16
HY_DECAY_TARGET = 1e-2
HY_MIN_DECAY = math.log(HY_DECAY_TARGET) / 1.5
HY_MAX_DECAY = math.log(HY_DECAY_TARGET) / 0.3
HY_SHORT = 3
CF_KERNEL = 31
CF_PAD = 16
LANES = 128

V7X_VMEM_LIMIT_BYTES = 56 * 1024 * 1024


def _params(*sem):
    return pltpu.CompilerParams(dimension_semantics=sem, vmem_limit_bytes=V7X_VMEM_LIMIT_BYTES)


def _dot(a, b):
    return jnp.dot(a, b, preferred_element_type=F32)


def _sigmoid(x):
    return 1.0 / (1.0 + jnp.exp(-x))


def _resident(shape):
    zeros = (0,) * len(shape)
    return pl.BlockSpec(shape, lambda *_: zeros, pipeline_mode=pl.Buffered(1))


def _parity_split_store(val, scr_ref, dst_ref):
    rows, cols = val.shape
    h = rows // 2
    for sl in range(cols // LANES):
        scr_ref[sl] = val[:, sl * LANES:(sl + 1) * LANES]
    for sl in range(cols // LANES):
        c = slice(sl * LANES, (sl + 1) * LANES)
        dst_ref[0:h, c] = scr_ref[sl, pl.ds(0, h, stride=2), :].astype(dst_ref.dtype)
        dst_ref[h:rows, c] = scr_ref[sl, pl.ds(1, h, stride=2), :].astype(dst_ref.dtype)


def _dft_kernel(w_ref, ce_ref, co_ref, sef_ref, sof_ref, seg_ref, cog_ref, sog_ref, wbf_ref,
                rc_ref, rs_ref, pc_ref, ps_ref, *, tm, h, chunk):
    i = pl.program_id(0)
    mask = 4 * h - 1
    scale = 2.0 * math.pi / (4 * h)
    wbf_ref[...] = w_ref[...].astype(BF16)

    @pl.when(i == 0)
    def _():
        def base(j, carry):
            r0 = pl.multiple_of(j * chunk, chunk)
            r = r0 + lax.broadcasted_iota(jnp.int32, (chunk, h), 0)
            c = lax.broadcasted_iota(jnp.int32, (chunk, h), 1)
            th_e = ((2 * r * c) & mask).astype(F32) * scale
            th_o = ((r * (2 * c + 1)) & mask).astype(F32) * scale
            rc_ref[pl.ds(r0, chunk), :] = jnp.cos(th_e)
            rs_ref[pl.ds(r0, chunk), :] = jnp.sin(th_e)
            pc_ref[pl.ds(r0, chunk), :] = jnp.cos(th_o)
            ps_ref[pl.ds(r0, chunk), :] = jnp.sin(th_o)
            return carry
        lax.fori_loop(0, tm // chunk, base, 0)

    k0 = i * tm
    c1 = lax.broadcasted_iota(jnp.int32, (1, h), 1)
    th = ((2 * k0 * c1) & mask).astype(F32) * scale
    ce0, se0 = jnp.cos(th), jnp.sin(th)
    th = ((k0 * (2 * c1 + 1)) & mask).astype(F32) * scale
    co0, so0 = jnp.cos(th), jnp.sin(th)
    th = (((2 * k0 + 1) * c1) & mask).astype(F32) * scale
    cg0, sg0 = jnp.cos(th), jnp.sin(th)

    def tile(j, carry):
        r0 = pl.multiple_of(j * chunk, chunk)
        rows = pl.ds(r0, chunk)
        rc, rs, pc, ps = rc_ref[rows, :], rs_ref[rows, :], pc_ref[rows, :], ps_ref[rows, :]
        row = k0 + r0 + lax.broadcasted_iota(jnp.int32, (chunk, h), 0)
        col = lax.broadcasted_iota(jnp.int32, (chunk, h), 1)
        alt_col = (1 - 2 * (col & 1)).astype(F32)
        alt_row = (1 - 2 * (row & 1)).astype(F32)
        se = rs * ce0 + rc * se0
        ce_ref[rows, :] = (rc * ce0 - rs * se0).astype(BF16)
        co_ref[rows, :] = (pc * co0 - ps * so0).astype(BF16)
        sef_ref[rows, :] = jnp.where(row == 0, alt_col, se).astype(BF16)
        sof_ref[rows, :] = jnp.where(row == 0, alt_col, ps * co0 + pc * so0).astype(BF16)
        seg_ref[rows, :] = jnp.where(col == 0, alt_row, se).astype(BF16)
        cog_ref[rows, :] = (rc * cg0 - rs * sg0).astype(BF16)
        sog_ref[rows, :] = jnp.where(col == 0, alt_row, rs * cg0 + rc * sg0).astype(BF16)
        return carry
    lax.fori_loop(0, tm // chunk, tile, 0)


def _dft_tables(seq, w, w_rows, tm=256, chunk=32):
    h = seq // 2
    steps = h // tm
    wr = w_rows // steps
    out = jax.ShapeDtypeStruct((h, h), BF16)
    spec = pl.BlockSpec((tm, h), lambda i: (i, 0))
    wspec = pl.BlockSpec((wr, w.shape[1]), lambda i: (i, 0))
    return pl.pallas_call(
        functools.partial(_dft_kernel, tm=tm, h=h, chunk=chunk),
        grid=(steps,),
        in_specs=[wspec],
        out_specs=[spec] * 7 + [wspec],
        out_shape=[out] * 7 + [jax.ShapeDtypeStruct((w_rows, w.shape[1]), BF16)],
        scratch_shapes=[pltpu.VMEM((tm, h), F32)] * 4,
        compiler_params=_params("arbitrary"),
        name="dft_tables",
    )(w)


def _in_proj_kernel(x_ref, g_ref, wa_ref, wb_ref, o_ref, h_ref):
    @pl.when(pl.program_id(1) == 0)
    def _():
        x = x_ref[...]
        ms = jnp.mean(x * x, axis=-1, keepdims=True)
        h_ref[...] = (x * lax.rsqrt(ms + EPS) * g_ref[...]).astype(BF16)

    ka = wa_ref.shape[0]
    acc = _dot(h_ref[:, 0:ka], wa_ref[...]) + _dot(h_ref[:, ka:], wb_ref[...])
    o_ref[...] = acc.astype(o_ref.dtype)


def _in_proj(x2d, g, wa, wb, tm=1024, tn=1024):
    m, d = x2d.shape
    n = wa.shape[1]
    assert wa.shape[0] + wb.shape[0] == d
    return pl.pallas_call(
        _in_proj_kernel,
        grid=(m // tm, n // tn),
        in_specs=[pl.BlockSpec((tm, d), lambda i, j: (i, 0)),
                  pl.BlockSpec((1, d), lambda i, j: (0, 0)),
                  pl.BlockSpec((wa.shape[0], tn), lambda i, j: (0, j)),
                  pl.BlockSpec((wb.shape[0], tn), lambda i, j: (0, j))],
        out_specs=pl.BlockSpec((tm, tn), lambda i, j: (i, j)),
        out_shape=jax.ShapeDtypeStruct((m, n), BF16),
        scratch_shapes=[pltpu.VMEM((tm, d), BF16)],
        compiler_params=_params("parallel", "arbitrary"),
        name="in_proj",
    )(x2d, g, wa, wb)


def _conv3_kernel(p_ref, w_ref, b_ref, o_ref, scr_ref):
    x = p_ref[0].astype(F32)
    rows = x.shape[0]
    row = lax.broadcasted_iota(jnp.int32, x.shape, 0)
    prev = jnp.where(row == 0, 0.0, pltpu.roll(x, 1, 0))
    nxt = jnp.where(row == rows - 1, 0.0, pltpu.roll(x, rows - 1, 0))
    w = w_ref[...]
    y = prev * w[0:1, :] + x * w[1:2, :] + nxt * w[2:3, :] + b_ref[...]
    _parity_split_store(y, scr_ref, o_ref.at[0])


def _hy_conv3(proj3, w, b, width, ct=512):
    bsz, seq, _ = proj3.shape
    return pl.pallas_call(
        _conv3_kernel,
        grid=(bsz, width // ct),
        in_specs=[pl.BlockSpec((1, seq, ct), lambda i, j: (i, 0, j)),
                  pl.BlockSpec((HY_SHORT, ct), lambda i, j: (0, j)),
                  pl.BlockSpec((1, ct), lambda i, j: (0, j))],
        out_specs=pl.BlockSpec((1, seq, ct), lambda i, j: (i, 0, j)),
        out_shape=jax.ShapeDtypeStruct((bsz, seq, width), BF16),
        scratch_shapes=[pltpu.VMEM((ct // LANES, seq, LANES), F32)],
        compiler_params=_params("parallel", "parallel"),
        name="hy_conv3",
    )(proj3, w, b)


def _filt_mlp_kernel(w1t_ref, w1c_ref, w1s_ref, b1_ref, fr1_ref, w2_ref, b2_ref, fr2_ref,
                     w3f_ref, w3b_ref, w_ref, s_ref, d_ref, tmid_ref, wbf_ref, h2_ref, scr_ref,
                     *, seq, width, ct):
    jc = pl.program_id(1)
    wbf_ref[...] = w_ref[...].astype(BF16)
    n = lax.broadcasted_iota(jnp.int32, (seq, 1), 0).astype(F32)
    t = n / (seq - 1)

    @pl.when((pl.program_id(0) == 0) & (jc == 0))
    def _():
        wang = 2.0 * math.pi * n / seq
        band = lax.broadcasted_iota(jnp.int32, (1, HY_EMB_BANDS), 1).astype(F32)
        f = 1e-4 + band * ((HY_EMB_BANDS - 1 - 1e-4) / (HY_EMB_BANDS - 1))
        fw = f * wang
        pre1 = (t * w1t_ref[...] + _dot(jnp.cos(fw), w1c_ref[...]) - _dot(jnp.sin(fw), w1s_ref[...])
                + b1_ref[...])
        h1 = jnp.sin(fr1_ref[...] * pre1)
        h2_ref[...] = jnp.sin(fr2_ref[...] * (_dot(h1, w2_ref[...]) + b2_ref[...]))

    h2 = h2_ref[...]
    ch = (jc * ct + lax.broadcasted_iota(jnp.int32, (1, ct), 1)).astype(F32)
    delta = HY_MIN_DECAY + ch * ((HY_MAX_DECAY - HY_MIN_DECAY) / (width - 1))
    decay = jnp.exp(-t * jnp.abs(delta))
    kf = _dot(h2, w3f_ref[...]) * decay
    kb = _dot(h2, w3b_ref[...]) * decay
    row = lax.broadcasted_iota(jnp.int32, (seq, ct), 0)
    kb = jnp.where(row == 0, 0.0, kb)
    l1 = jnp.sum(jnp.abs(kf), axis=0, keepdims=True) + jnp.sum(jnp.abs(kb), axis=0, keepdims=True)
    inv = 1.0 / l1
    s = (kf + kb) * inv
    dm = (kf - kb) * inv
    phase = row & 3
    cmid = jnp.where(phase == 0, 1.0, jnp.where(phase == 2, -1.0, 0.0))
    smid = jnp.where(phase == 1, 1.0, jnp.where(phase == 3, -1.0, 0.0))
    wmid = 2.0 / (2 * seq)
    tmid_ref[0:1, :] = jnp.sum(s * cmid, axis=0, keepdims=True) * wmid
    tmid_ref[1:2, :] = jnp.sum(dm * smid, axis=0, keepdims=True) * (-wmid)
    _parity_split_store(s, scr_ref, s_ref)
    _parity_split_store(dm, scr_ref, d_ref)


def _filt_mlp(seq, width, w1, b1, fr1, w2, b2, fr2, w3, w, w_row0, ct=256):
    hid = w2.shape[0]
    nct = width // ct
    steps = HY_ORDER * nct
    w_rows = w.shape[0] - w_row0
    wr = w_rows // steps
    assert w_row0 % wr == 0
    small = lambda shape: pl.BlockSpec(shape, lambda o, j: (0, 0))
    out_spec = pl.BlockSpec((seq, ct), lambda o, j: (0, o * nct + j))
    return pl.pallas_call(
        functools.partial(_filt_mlp_kernel, seq=seq, width=width, ct=ct),
        grid=(HY_ORDER, nct),
        in_specs=[small((1, hid)), small((HY_EMB_BANDS, hid)), small((HY_EMB_BANDS, hid)),
                  small((1, hid)), small((1, hid)), small((hid, hid)), small((1, hid)), small((1, hid)),
                  pl.BlockSpec((hid, ct), lambda o, j: (0, o * 2 * nct + j)),
                  pl.BlockSpec((hid, ct), lambda o, j: (0, o * 2 * nct + nct + j)),
                  pl.BlockSpec((wr, w.shape[1]), lambda o, j: (w_row0 // wr + o * nct + j, 0))],
        out_specs=[out_spec, out_spec, pl.BlockSpec((2, ct), lambda o, j: (0, o * nct + j)),
                   pl.BlockSpec((wr, w.shape[1]), lambda o, j: (o * nct + j, 0))],
        out_shape=[jax.ShapeDtypeStruct((seq, HY_ORDER * width), BF16),
                   jax.ShapeDtypeStruct((seq, HY_ORDER * width), BF16),
                   jax.ShapeDtypeStruct((2, HY_ORDER * width), F32),
                   jax.ShapeDtypeStruct((w_rows, w.shape[1]), BF16)],
        scratch_shapes=[pltpu.VMEM((seq, hid), F32), pltpu.VMEM((ct // LANES, seq, LANES), F32)],
        compiler_params=_params("arbitrary", "arbitrary"),
        name="filt_mlp",
    )(w1[0:1], w1[1:1 + HY_EMB_BANDS], w1[1 + HY_EMB_BANDS:], b1, fr1, w2, b2, fr2, w3, w3, w)


def _filt_dft_kernel(ce_ref, co_ref, sef_ref, sof_ref, s_ref, d_ref, t_ref, *, tm, seq):
    h = seq // 2
    pc = _dot(ce_ref[...], s_ref[0:h, :])
    qc = _dot(co_ref[...], s_ref[h:seq, :])
    ps = _dot(sef_ref[...], d_ref[0:h, :])
    qs = _dot(sof_ref[...], d_ref[h:seq, :])
    row = pl.program_id(1) * tm + lax.broadcasted_iota(jnp.int32, pc.shape, 0)
    wk = jnp.where(row == 0, 1.0, 2.0) * (1.0 / (2 * seq))
    t_ref[0] = ((pc + qc) * wk).astype(BF16)
    t_ref[1] = jnp.where(row == 0, 0.0, -(ps + qs) * wk).astype(BF16)
    t_ref[2] = ((pc - qc) * wk).astype(BF16)
    t_ref[3] = jnp.where(row == 0, 0.0, (ps - qs) * wk).astype(BF16)


def _filt_dft(ce, co, sef, sof, s, d, tm=512, tn=512):
    h = ce.shape[0]
    seq, cols = s.shape
    mat = pl.BlockSpec((tm, h), lambda j, i: (i, 0))
    tab = pl.BlockSpec((seq, tn), lambda j, i: (0, j))
    return pl.pallas_call(
        functools.partial(_filt_dft_kernel, tm=tm, seq=seq),
        grid=(cols // tn, h // tm),
        in_specs=[mat, mat, mat, mat, tab, tab],
        out_specs=pl.BlockSpec((4, tm, tn), lambda j, i: (0, i, j)),
        out_shape=jax.ShapeDtypeStruct((4, h, cols), BF16),
        compiler_params=_params("parallel", "parallel"),
        name="filt_dft",
    )(ce, co, sef, sof, s, d)


def _hy_fwd_kernel(ce_ref, co_ref, sef_ref, sof_ref, u_ref, t_ref, tmid_ref, y_ref, *, tm):
    h = ce_ref.shape[0]
    i = pl.program_id(1)
    rows = pl.ds(pl.multiple_of(i * tm, tm), tm)
    ue = u_ref[0, 0:h, :]
    uo = u_ref[0, h:2 * h, :]
    pc = _dot(ce_ref[rows, :], ue)
    qc = _dot(co_ref[rows, :], uo)
    ps = _dot(sef_ref[rows, :], ue)
    qs = _dot(sof_ref[rows, :], uo)
    r0 = (i * tm + lax.broadcasted_iota(jnp.int32, pc.shape, 0)) == 0
    tr1, ti1 = t_ref[0, rows, :].astype(F32), t_ref[1, rows, :].astype(F32)
    tr2, ti2 = t_ref[2, rows, :].astype(F32), t_ref[3, rows, :].astype(F32)
    a1, a2 = pc + qc, pc - qc
    b1 = jnp.where(r0, 0.0, ps + qs)
    b2 = jnp.where(r0, 0.0, qs - ps)
    yr1, yi1 = a1 * tr1 + b1 * ti1, b1 * tr1 - a1 * ti1
    yr2, yi2 = a2 * tr2 + b2 * ti2, b2 * tr2 - a2 * ti2
    trh, tih = tmid_ref[0:1, :], tmid_ref[1:2, :]
    y_ref[0, 0] = (yr1 + yr2).astype(BF16)
    y_ref[0, 1] = (yr1 - yr2).astype(BF16)
    y_ref[0, 2] = jnp.where(r0, ps * trh + qs * tih, yi1 - yi2).astype(BF16)
    y_ref[0, 3] = jnp.where(r0, qs * trh - ps * tih, yi1 + yi2).astype(BF16)


def _hy_fwd(mats, u3, u_col, tt, tmid, t_col, width, tm=256):
    bsz, seq, _ = u3.shape
    h = seq // 2
    return pl.pallas_call(
        functools.partial(_hy_fwd_kernel, tm=tm),
        grid=(bsz, h // tm),
        in_specs=[_resident((h, h))] * 4
                 + [pl.BlockSpec((1, seq, width), lambda b, i: (b, 0, u_col)),
                    pl.BlockSpec((4, h, width), lambda b, i: (0, 0, t_col), pipeline_mode=pl.Buffered(1)),
                    pl.BlockSpec((2, width), lambda b, i: (0, t_col))],
        out_specs=pl.BlockSpec((1, 4, tm, width), lambda b, i: (b, 0, i, 0)),
        out_shape=jax.ShapeDtypeStruct((bsz, 4, h, width), BF16),
        compiler_params=_params("arbitrary", "arbitrary"),
        name="hy_fwd",
    )(*mats, u3, tt, tmid)


def _hy_inv_kernel(ce_ref, seg_ref, cog_ref, sog_ref, y_ref, g_ref, u_ref, bias_ref, o_ref, *scr, tm, natural):
    i = pl.program_id(1)
    rows = pl.ds(pl.multiple_of(i * tm, tm), tm)
    ye = _dot(ce_ref[rows, :], y_ref[0, 0]) + _dot(seg_ref[rows, :], y_ref[0, 2])
    yo = _dot(cog_ref[rows, :], y_ref[0, 1]) + _dot(sog_ref[rows, :], y_ref[0, 3])
    bias = bias_ref[...]
    ze = g_ref[0, 0].astype(F32) * (ye + u_ref[0, 0].astype(F32) * bias)
    zo = g_ref[0, 1].astype(F32) * (yo + u_ref[0, 1].astype(F32) * bias)
    if natural:
        scr_ref, = scr
        for sl in range(ze.shape[1] // LANES):
            c = slice(sl * LANES, (sl + 1) * LANES)
            scr_ref[sl, pl.ds(0, tm, stride=2), :] = ze[:, c]
            scr_ref[sl, pl.ds(1, tm, stride=2), :] = zo[:, c]
        for sl in range(ze.shape[1] // LANES):
            o_ref[0, :, sl * LANES:(sl + 1) * LANES] = scr_ref[sl].astype(o_ref.dtype)
    else:
        o_ref[0, 0] = ze.astype(o_ref.dtype)
        o_ref[0, 1] = zo.astype(o_ref.dtype)


def _hy_inv(mats, y4, g4, g_col, u4, u_col, bias, width, natural, tm=256):
    bsz, _, h, _ = y4.shape
    par = lambda col: pl.BlockSpec((1, 2, tm, width), lambda b, i: (b, 0, i, col))
    if natural:
        out_spec = pl.BlockSpec((1, 2 * tm, width), lambda b, i: (b, i, 0))
        out_shape = jax.ShapeDtypeStruct((bsz, 2 * h, width), BF16)
        scratch = [pltpu.VMEM((width // LANES, 2 * tm, LANES), F32)]
    else:
        out_spec = par(0)
        out_shape = jax.ShapeDtypeStruct((bsz, 2, h, width), BF16)
        scratch = []
    return pl.pallas_call(
        functools.partial(_hy_inv_kernel, tm=tm, natural=natural),
        grid=(bsz, h // tm),
        in_specs=[_resident((h, h))] * 4
                 + [pl.BlockSpec((1, 4, h, width), lambda b, i: (b, 0, 0, 0)),
                    par(g_col), par(u_col),
                    pl.BlockSpec((1, width), lambda b, i: (0, 0))],
        out_specs=out_spec,
        out_shape=out_shape,
        scratch_shapes=scratch,
        compiler_params=_params("arbitrary", "arbitrary"),
        name="hy_inv",
    )(*mats, y4, g4, u4, bias)


def _cf_kernel(*refs, step_rows, grp, n_cast):
    (am_ref, ap_ref, an_ref, bm_ref, bp_ref, bn_ref, w_ref, cb_ref, lg_ref, lb_ref) = refs[:10]
    cast_in = refs[10:10 + n_cast]
    o_ref = refs[10 + n_cast]
    cast_out = refs[11 + n_cast:11 + 2 * n_cast]
    us_ref, sh_ref, wb_ref = refs[11 + 2 * n_cast:]
    width = us_ref.shape[-1]
    j = pl.program_id(1)

    for src, dst in zip(cast_in, cast_out):
        dst[...] = src[...].astype(dst.dtype)

    @pl.when((pl.program_id(0) == 0) & (j == 0))
    def _():
        for t in range(CF_KERNEL):
            wb_ref[t] = jnp.broadcast_to(w_ref[t:t + 1, :], (8, width))

    glu = lambda a, b: a[0].astype(F32) * _sigmoid(b[0].astype(F32))
    us_ref[0:CF_PAD, :] = jnp.where(j == 0, 0.0, glu(ap_ref, bp_ref))
    us_ref[CF_PAD:CF_PAD + step_rows, :] = glu(am_ref, bm_ref)
    us_ref[CF_PAD + step_rows:, :] = jnp.where(j == pl.num_programs(1) - 1, 0.0, glu(an_ref, bn_ref))

    ngrp = sh_ref.shape[1]
    blk = 8
    for g0 in range(0, ngrp, blk):
        ng = min(blk, ngrp - g0)
        win = us_ref[8 * g0:8 * (g0 + ng + 1), :]
        sh_ref[0, g0:g0 + ng] = win[0:8 * ng].reshape(ng, 8, width)
        for s in range(1, 8):
            sh = pltpu.roll(win, win.shape[0] - s, 0)
            sh_ref[s, g0:g0 + ng] = sh[0:8 * ng].reshape(ng, 8, width)

    half = CF_KERNEL // 2
    for c in range(step_rows // (8 * grp)):
        acc = jnp.zeros((grp, 8, width), F32) + cb_ref[...][None]
        for t in range(CF_KERNEL):
            off = CF_PAD - half + t
            acc = acc + wb_ref[t][None] * sh_ref[off % 8, c * grp + off // 8:c * grp + off // 8 + grp]
        mu = jnp.mean(acc, axis=-1, keepdims=True)
        cen = acc - mu
        var = jnp.mean(cen * cen, axis=-1, keepdims=True)
        y = cen * lax.rsqrt(var + EPS) * lg_ref[...][None] + lb_ref[...][None]
        o_ref[0, c * 8 * grp:(c + 1) * 8 * grp, :] = (y * _sigmoid(y)).reshape(8 * grp, width).astype(o_ref.dtype)


def _cf_conv(proj3, a_col, b_col, w, cb, lg, lb, width, casts, step_rows=128, grp=4):
    bsz, seq, _ = proj3.shape
    nsteps = seq // step_rows
    total = bsz * nsteps
    nblk = step_rows // CF_PAD
    last_blk = seq // CF_PAD - 1
    vec = pl.BlockSpec((1, width), lambda b, j: (0, 0))
    main = lambda col: pl.BlockSpec((1, step_rows, width), lambda b, j: (b, j, col))
    prev = lambda col: pl.BlockSpec((1, CF_PAD, width), lambda b, j: (b, jnp.maximum(j * nblk - 1, 0), col))
    nxt = lambda col: pl.BlockSpec((1, CF_PAD, width),
                                   lambda b, j: (b, jnp.minimum((j + 1) * nblk, last_blk), col))

    def cast_spec(c):
        n = next(n for n in (total, total // 2, total // 4) if c.shape[0] % (16 * n) == 0)
        every = total // n
        return pl.BlockSpec((c.shape[0] // n, c.shape[1]), lambda b, j: ((b * nsteps + j) // every, 0))
    cast_specs = [cast_spec(c) for c in casts]
    outs = pl.pallas_call(
        functools.partial(_cf_kernel, step_rows=step_rows, grp=grp, n_cast=len(casts)),
        grid=(bsz, nsteps),
        in_specs=[main(a_col), prev(a_col), nxt(a_col), main(b_col), prev(b_col), nxt(b_col),
                  pl.BlockSpec((CF_KERNEL, width), lambda b, j: (0, 0)),
                  vec, vec, vec] + cast_specs,
        out_specs=[pl.BlockSpec((1, step_rows, width), lambda b, j: (b, j, 0))] + cast_specs,
        out_shape=[jax.ShapeDtypeStruct((bsz, seq, width), BF16)]
                  + [jax.ShapeDtypeStruct(c.shape, BF16) for c in casts],
        scratch_shapes=[pltpu.VMEM((step_rows + 2 * CF_PAD, width), F32),
                        pltpu.VMEM((8, (step_rows + 2 * CF_PAD) // 8 - 1, 8, width), F32),
                        pltpu.VMEM((CF_KERNEL, 8, width), F32)],
        compiler_params=_params("arbitrary", "arbitrary"),
        name="cf_conv",
    )(proj3, proj3, proj3, proj3, proj3, proj3, w, cb, lg, lb, *casts)
    return outs[0], outs[1:]


def _merge_kernel(ya_ref, yb_ref, ga0_ref, ga1_ref, gb0_ref, gb1_ref, x_ref, pa_ref, pb_ref, wo_ref,
                  g1_ref, x1_ref):
    a = _dot(ya_ref[...], pa_ref[...])
    b = _dot(yb_ref[...], pb_ref[...])
    w = ga0_ref.shape[1]
    m0 = _sigmoid(ga0_ref[...].astype(F32)) * a[:, :w] + _sigmoid(gb0_ref[...].astype(F32)) * b[:, :w]
    m1 = _sigmoid(ga1_ref[...].astype(F32)) * a[:, w:] + _sigmoid(gb1_ref[...].astype(F32)) * b[:, w:]
    o = _dot(m0.astype(BF16), wo_ref[0:w, :]) + _dot(m1.astype(BF16), wo_ref[w:2 * w, :])
    x1_ref[...] = x_ref[...] + o * lax.rsqrt(jnp.mean(o * o, axis=-1, keepdims=True) + EPS) * g1_ref[...]


def _merge(ya, yb, proj, ga_col, gb_col, x2d, pa, pb, wo, g1, tm=256):
    m, d = x2d.shape
    wa = ya.shape[1]
    assert d == 2 * wa
    const = lambda shape: pl.BlockSpec(shape, lambda i: (0, 0))
    gate = lambda col: pl.BlockSpec((tm, wa), lambda i: (i, col))
    return pl.pallas_call(
        _merge_kernel,
        grid=(m // tm,),
        in_specs=[pl.BlockSpec((tm, wa), lambda i: (i, 0)),
                  pl.BlockSpec((tm, wa), lambda i: (i, 0)),
                  gate(ga_col), gate(ga_col + 1), gate(gb_col), gate(gb_col + 1),
                  pl.BlockSpec((tm, d), lambda i: (i, 0)),
                  const((wa, d)), const((wa, d)), const((d, d)), const((1, d))],
        out_specs=pl.BlockSpec((tm, d), lambda i: (i, 0)),
        out_shape=jax.ShapeDtypeStruct((m, d), F32),
        compiler_params=_params("parallel"),
        name="merge",
    )(ya, yb, proj, proj, proj, proj, x2d, pa, pb, wo, g1)


def _ffn_kernel(x1_ref, gpre_ref, wg_ref, wu_ref, wd_ref, gpost_ref, o_ref, h_ref):
    j = pl.program_id(1)

    @pl.when(j == 0)
    def _():
        x1 = x1_ref[...]
        ms = jnp.mean(x1 * x1, axis=-1, keepdims=True)
        h_ref[...] = (x1 * lax.rsqrt(ms + EPS) * gpre_ref[...]).astype(BF16)
        o_ref[...] = jnp.zeros_like(o_ref)

    h = h_ref[...]
    gate = _dot(h, wg_ref[...])
    up = _dot(h, wu_ref[...])
    act = (gate * _sigmoid(gate) * up).astype(BF16)
    o_ref[...] += _dot(act, wd_ref[...])

    @pl.when(j == pl.num_programs(1) - 1)
    def _():
        a = o_ref[...]
        o_ref[...] = x1_ref[...] + a * lax.rsqrt(jnp.mean(a * a, axis=-1, keepdims=True) + EPS) * gpost_ref[...]


def _ffn(x1, gpre, wgu, wd, gpost, tm=1024, th=256):
    m, d = x1.shape
    hidden = wd.shape[0]
    nh = hidden // th
    return pl.pallas_call(
        _ffn_kernel,
        grid=(m // tm, nh),
        in_specs=[pl.BlockSpec((tm, d), lambda i, j: (i, 0)),
                  pl.BlockSpec((1, d), lambda i, j: (0, 0)),
                  pl.BlockSpec((d, th), lambda i, j: (0, j)),
                  pl.BlockSpec((d, th), lambda i, j: (0, nh + j)),
                  pl.BlockSpec((th, d), lambda i, j: (j, 0)),
                  pl.BlockSpec((1, d), lambda i, j: (0, 0))],
        out_specs=pl.BlockSpec((tm, d), lambda i, j: (i, 0)),
        out_shape=jax.ShapeDtypeStruct((m, d), F32),
        scratch_shapes=[pltpu.VMEM((tm, d), BF16)],
        compiler_params=_params("parallel", "arbitrary"),
        name="ffn",
    )(x1, gpre, wgu, wgu, wd, gpost)


def kernel(x, mix_pre_g, w_in, hy_conv_w, hy_conv_b, hy_filt_w1, hy_filt_b1, hy_filt_fr1, hy_filt_w2,
           hy_filt_b2, hy_filt_fr2, hy_filt_w3, hy_bias, hy_proj, cf_dw_w, cf_dw_b, cf_ln_g, cf_ln_b,
           cf_proj, w_out, mix_post_g, ffn_pre_g, ffn_w_gu, ffn_w_down, ffn_post_g):
    bsz, seq, d = x.shape
    depth = w_in.shape[0]
    hw = hy_proj.shape[1]
    cw = cf_proj.shape[1]
    assert hw == cw and d % hw == 0
    row = lambda v: v.reshape(1, -1)

    assert depth == 1
    x2d = x.reshape(bsz * seq, d)
    for l in range(depth):
        ce, co, sef, sof, seg, cog, sog, w_in_a = _dft_tables(seq, w_in[l], d // 2)
        s, dm, tmid, w_in_b = _filt_mlp(seq, hw, hy_filt_w1[l], row(hy_filt_b1[l]), row(hy_filt_fr1[l]),
                                        hy_filt_w2[l], row(hy_filt_b2[l]), row(hy_filt_fr2[l]),
                                        hy_filt_w3[l], w_in[l], d // 2)
        proj = _in_proj(x2d, row(mix_pre_g[l]), w_in_a, w_in_b)
        proj3 = proj.reshape(bsz, seq, -1)
        cf_a_col, cf_b_col = 3, 4
        ga_col, gb_col = 5, 5 + d // hw

        hy = _hy_conv3(proj3, hy_conv_w[l], row(hy_conv_b[l]), 3 * hw)
        hy4 = hy.reshape(bsz, 2, seq // 2, 3 * hw)
        tt = _filt_dft(ce, co, sef, sof, s, dm)
        fwd, inv = (ce, co, sef, sof), (ce, seg, cog, sog)
        y1 = _hy_fwd(fwd, hy, 0, tt, tmid, 0, hw)
        z4 = _hy_inv(inv, y1, hy4, 1, hy4, 0, row(hy_bias[l, 0]), hw, natural=False)
        y2 = _hy_fwd(fwd, z4.reshape(bsz, seq, hw), 0, tt, tmid, 1, hw)
        y_a = _hy_inv(inv, y2, hy4, 2, z4, 0, row(hy_bias[l, 1]), hw, natural=True)

        y_b, (hy_proj_bf, cf_proj_bf, w_out_bf, w_gu_bf, w_down_bf) = _cf_conv(
            proj3, cf_a_col, cf_b_col, cf_dw_w[l], row(cf_dw_b[l]), row(cf_ln_g[l]), row(cf_ln_b[l]), cw,
            [hy_proj[l], cf_proj[l], w_out[l], ffn_w_gu[l], ffn_w_down[l]])

        x1 = _merge(y_a.reshape(bsz * seq, hw), y_b.reshape(bsz * seq, cw), proj, ga_col, gb_col, x2d,
                    hy_proj_bf, cf_proj_bf, w_out_bf, row(mix_post_g[l]))
        x2d = _ffn(x1, row(ffn_pre_g[l]), w_gu_bf, w_down_bf, row(ffn_post_g[l]))
    return x2d.reshape(bsz, seq, d)
```

```python
import functools
import math

import jax
import jax.numpy as jnp
from jax import lax
from jax.experimental import pallas as pl
from jax.experimental.pallas import tpu as pltpu

F32 = jnp.float32
BF16 = jnp.bfloat16
EPS = 1e-6

HY_ORDER = 2
HY_EMB_BANDS = 16
HY_DECAY_TARGET = 1e-2
HY_MIN_DECAY = math.log(HY_DECAY_TARGET) / 1.5
HY_MAX_DECAY = math.log(HY_DECAY_TARGET) / 0.3
HY_SHORT = 3
CF_KERNEL = 31
CF_PAD = 16
LANES = 128

V7X_VMEM_LIMIT_BYTES = 56 * 1024 * 1024


def _params(*sem):
    return pltpu.CompilerParams(dimension_semantics=sem, vmem_limit_bytes=V7X_VMEM_LIMIT_BYTES)


def _dot(a, b):
    return jnp.dot(a, b, preferred_element_type=F32)


def _sigmoid(x):
    return 1.0 / (1.0 + jnp.exp(-x))


def _resident(shape):
    zeros = (0,) * len(shape)
    return pl.BlockSpec(shape, lambda *_: zeros, pipeline_mode=pl.Buffered(1))


def _parity_split_store(val, scr_ref, dst_ref):
    rows, cols = val.shape
    h = rows // 2
    for sl in range(cols // LANES):
        scr_ref[sl] = val[:, sl * LANES:(sl + 1) * LANES]
    for sl in range(cols // LANES):
        c = slice(sl * LANES, (sl + 1) * LANES)
        dst_ref[0:h, c] = scr_ref[sl, pl.ds(0, h, stride=2), :].astype(dst_ref.dtype)
        dst_ref[h:rows, c] = scr_ref[sl, pl.ds(1, h, stride=2), :].astype(dst_ref.dtype)


def _dft_kernel(w_ref, ce_ref, co_ref, sef_ref, sof_ref, seg_ref, cog_ref, sog_ref, wbf_ref,
                rc_ref, rs_ref, pc_ref, ps_ref, *, tm, h, chunk):
    i = pl.program_id(0)
    mask = 4 * h - 1
    scale = 2.0 * math.pi / (4 * h)
    wbf_ref[...] = w_ref[...].astype(BF16)

    @pl.when(i == 0)
    def _():
        def base(j, carry):
            r0 = pl.multiple_of(j * chunk, chunk)
            r = r0 + lax.broadcasted_iota(jnp.int32, (chunk, h), 0)
            c = lax.broadcasted_iota(jnp.int32, (chunk, h), 1)
            th_e = ((2 * r * c) & mask).astype(F32) * scale
            th_o = ((r * (2 * c + 1)) & mask).astype(F32) * scale
            rc_ref[pl.ds(r0, chunk), :] = jnp.cos(th_e)
            rs_ref[pl.ds(r0, chunk), :] = jnp.sin(th_e)
            pc_ref[pl.ds(r0, chunk), :] = jnp.cos(th_o)
            ps_ref[pl.ds(r0, chunk), :] = jnp.sin(th_o)
            return carry
        lax.fori_loop(0, tm // chunk, base, 0)

    k0 = i * tm
    c1 = lax.broadcasted_iota(jnp.int32, (1, h), 1)
    th = ((2 * k0 * c1) & mask).astype(F32) * scale
    ce0, se0 = jnp.cos(th), jnp.sin(th)
    th = ((k0 * (2 * c1 + 1)) & mask).astype(F32) * scale
    co0, so0 = jnp.cos(th), jnp.sin(th)
    th = (((2 * k0 + 1) * c1) & mask).astype(F32) * scale
    cg0, sg0 = jnp.cos(th), jnp.sin(th)

    def tile(j, carry):
        r0 = pl.multiple_of(j * chunk, chunk)
        rows = pl.ds(r0, chunk)
        rc, rs, pc, ps = rc_ref[rows, :], rs_ref[rows, :], pc_ref[rows, :], ps_ref[rows, :]
        row = k0 + r0 + lax.broadcasted_iota(jnp.int32, (chunk, h), 0)
        col = lax.broadcasted_iota(jnp.int32, (chunk, h), 1)
        alt_col = (1 - 2 * (col & 1)).astype(F32)
        alt_row = (1 - 2 * (row & 1)).astype(F32)
        se = rs * ce0 + rc * se0
        ce_ref[rows, :] = (rc * ce0 - rs * se0).astype(BF16)
        co_ref[rows, :] = (pc * co0 - ps * so0).astype(BF16)
        sef_ref[rows, :] = jnp.where(row == 0, alt_col, se).astype(BF16)
        sof_ref[rows, :] = jnp.where(row == 0, alt_col, ps * co0 + pc * so0).astype(BF16)
        seg_ref[rows, :] = jnp.where(col == 0, alt_row, se).astype(BF16)
        cog_ref[rows, :] = (rc * cg0 - rs * sg0).astype(BF16)
        sog_ref[rows, :] = jnp.where(col == 0, alt_row, rs * cg0 + rc * sg0).astype(BF16)
        return carry
    lax.fori_loop(0, tm // chunk, tile, 0)


def _dft_tables(seq, w, w_rows, tm=256, chunk=32):
    h = seq // 2
    steps = h // tm
    wr = w_rows // steps
    out = jax.ShapeDtypeStruct((h, h), BF16)
    spec = pl.BlockSpec((tm, h), lambda i: (i, 0))
    wspec = pl.BlockSpec((wr, w.shape[1]), lambda i: (i, 0))
    return pl.pallas_call(
        functools.partial(_dft_kernel, tm=tm, h=h, chunk=chunk),
        grid=(steps,),
        in_specs=[wspec],
        out_specs=[spec] * 7 + [wspec],
        out_shape=[out] * 7 + [jax.ShapeDtypeStruct((w_rows, w.shape[1]), BF16)],
        scratch_shapes=[pltpu.VMEM((tm, h), F32)] * 4,
        compiler_params=_params("arbitrary"),
        name="dft_tables",
    )(w)


def _in_proj_kernel(x_ref, g_ref, wa_ref, wb_ref, o_ref, h_ref):
    @pl.when(pl.program_id(1) == 0)
    def _():
        x = x_ref[...]
        ms = jnp.mean(x * x, axis=-1, keepdims=True)
        h_ref[...] = (x * lax.rsqrt(ms + EPS) * g_ref[...]).astype(BF16)

    ka = wa_ref.shape[0]
    acc = _dot(h_ref[:, 0:ka], wa_ref[...]) + _dot(h_ref[:, ka:], wb_ref[...])
    o_ref[...] = acc.astype(o_ref.dtype)


def _in_proj(x2d, g, wa, wb, tm=1024, tn=1536):
    m, d = x2d.shape
    n = wa.shape[1]
    assert wa.shape[0] + wb.shape[0] == d
    return pl.pallas_call(
        _in_proj_kernel,
        grid=(m // tm, n // tn),
        in_specs=[pl.BlockSpec((tm, d), lambda i, j: (i, 0)),
                  pl.BlockSpec((1, d), lambda i, j: (0, 0)),
                  pl.BlockSpec((wa.shape[0], tn), lambda i, j: (0, j)),
                  pl.BlockSpec((wb.shape[0], tn), lambda i, j: (0, j))],
        out_specs=pl.BlockSpec((tm, tn), lambda i, j: (i, j)),
        out_shape=jax.ShapeDtypeStruct((m, n), BF16),
        scratch_shapes=[pltpu.VMEM((tm, d), BF16)],
        compiler_params=_params("parallel", "arbitrary"),
        name="in_proj",
    )(x2d, g, wa, wb)


def _conv3_kernel(p_ref, w_ref, b_ref, o_ref, scr_ref):
    x = p_ref[0].astype(F32)
    rows = x.shape[0]
    row = lax.broadcasted_iota(jnp.int32, x.shape, 0)
    prev = jnp.where(row == 0, 0.0, pltpu.roll(x, 1, 0))
    nxt = jnp.where(row == rows - 1, 0.0, pltpu.roll(x, rows - 1, 0))
    w = w_ref[...]
    y = prev * w[0:1, :] + x * w[1:2, :] + nxt * w[2:3, :] + b_ref[...]
    _parity_split_store(y, scr_ref, o_ref.at[0])


def _hy_conv3(proj3, w, b, width, ct=512):
    bsz, seq, _ = proj3.shape
    return pl.pallas_call(
        _conv3_kernel,
        grid=(bsz, width // ct),
        in_specs=[pl.BlockSpec((1, seq, ct), lambda i, j: (i, 0, j)),
                  pl.BlockSpec((HY_SHORT, ct), lambda i, j: (0, j)),
                  pl.BlockSpec((1, ct), lambda i, j: (0, j))],
        out_specs=pl.BlockSpec((1, seq, ct), lambda i, j: (i, 0, j)),
        out_shape=jax.ShapeDtypeStruct((bsz, seq, width), BF16),
        scratch_shapes=[pltpu.VMEM((ct // LANES, seq, LANES), F32)],
        compiler_params=_params("parallel", "parallel"),
        name="hy_conv3",
    )(proj3, w, b)


def _filt_mlp_kernel(w1t_ref, w1c_ref, w1s_ref, b1_ref, fr1_ref, w2_ref, b2_ref, fr2_ref,
                     w3f_ref, w3b_ref, w_ref, s_ref, d_ref, tmid_ref, wbf_ref, h2_ref, scr_ref,
                     *, seq, width, ct):
    jc = pl.program_id(1)
    wbf_ref[...] = w_ref[...].astype(BF16)
    n = lax.broadcasted_iota(jnp.int32, (seq, 1), 0).astype(F32)
    t = n / (seq - 1)

    @pl.when((pl.program_id(0) == 0) & (jc == 0))
    def _():
        wang = 2.0 * math.pi * n / seq
        band = lax.broadcasted_iota(jnp.int32, (1, HY_EMB_BANDS), 1).astype(F32)
        f = 1e-4 + band * ((HY_EMB_BANDS - 1 - 1e-4) / (HY_EMB_BANDS - 1))
        fw = f * wang
        pre1 = (t * w1t_ref[...] + _dot(jnp.cos(fw), w1c_ref[...]) - _dot(jnp.sin(fw), w1s_ref[...])
                + b1_ref[...])
        h1 = jnp.sin(fr1_ref[...] * pre1)
        h2_ref[...] = jnp.sin(fr2_ref[...] * (_dot(h1, w2_ref[...]) + b2_ref[...]))

    h2 = h2_ref[...]
    ch = (jc * ct + lax.broadcasted_iota(jnp.int32, (1, ct), 1)).astype(F32)
    delta = HY_MIN_DECAY + ch * ((HY_MAX_DECAY - HY_MIN_DECAY) / (width - 1))
    decay = jnp.exp(-t * jnp.abs(delta))
    kf = _dot(h2, w3f_ref[...]) * decay
    kb = _dot(h2, w3b_ref[...]) * decay
    row = lax.broadcasted_iota(jnp.int32, (seq, ct), 0)
    kb = jnp.where(row == 0, 0.0, kb)
    l1 = jnp.sum(jnp.abs(kf), axis=0, keepdims=True) + jnp.sum(jnp.abs(kb), axis=0, keepdims=True)
    inv = 1.0 / l1
    s = (kf + kb) * inv
    dm = (kf - kb) * inv
    phase = row & 3
    cmid = jnp.where(phase == 0, 1.0, jnp.where(phase == 2, -1.0, 0.0))
    smid = jnp.where(phase == 1, 1.0, jnp.where(phase == 3, -1.0, 0.0))
    wmid = 2.0 / (2 * seq)
    tmid_ref[0:1, :] = jnp.sum(s * cmid, axis=0, keepdims=True) * wmid
    tmid_ref[1:2, :] = jnp.sum(dm * smid, axis=0, keepdims=True) * (-wmid)
    _parity_split_store(s, scr_ref, s_ref)
    _parity_split_store(dm, scr_ref, d_ref)


def _filt_mlp(seq, width, w1, b1, fr1, w2, b2, fr2, w3, w, w_row0, ct=256):
    hid = w2.shape[0]
    nct = width // ct
    steps = HY_ORDER * nct
    w_rows = w.shape[0] - w_row0
    wr = w_rows // steps
    assert w_row0 % wr == 0
    small = lambda shape: pl.BlockSpec(shape, lambda o, j: (0, 0))
    out_spec = pl.BlockSpec((seq, ct), lambda o, j: (0, o * nct + j))
    return pl.pallas_call(
        functools.partial(_filt_mlp_kernel, seq=seq, width=width, ct=ct),
        grid=(HY_ORDER, nct),
        in_specs=[small((1, hid)), small((HY_EMB_BANDS, hid)), small((HY_EMB_BANDS, hid)),
                  small((1, hid)), small((1, hid)), small((hid, hid)), small((1, hid)), small((1, hid)),
                  pl.BlockSpec((hid, ct), lambda o, j: (0, o * 2 * nct + j)),
                  pl.BlockSpec((hid, ct), lambda o, j: (0, o * 2 * nct + nct + j)),
                  pl.BlockSpec((wr, w.shape[1]), lambda o, j: (w_row0 // wr + o * nct + j, 0))],
        out_specs=[out_spec, out_spec, pl.BlockSpec((2, ct), lambda o, j: (0, o * nct + j)),
                   pl.BlockSpec((wr, w.shape[1]), lambda o, j: (o * nct + j, 0))],
        out_shape=[jax.ShapeDtypeStruct((seq, HY_ORDER * width), BF16),
                   jax.ShapeDtypeStruct((seq, HY_ORDER * width), BF16),
                   jax.ShapeDtypeStruct((2, HY_ORDER * width), F32),
                   jax.ShapeDtypeStruct((w_rows, w.shape[1]), BF16)],
        scratch_shapes=[pltpu.VMEM((seq, hid), F32), pltpu.VMEM((ct // LANES, seq, LANES), F32)],
        compiler_params=_params("arbitrary", "arbitrary"),
        name="filt_mlp",
    )(w1[0:1], w1[1:1 + HY_EMB_BANDS], w1[1 + HY_EMB_BANDS:], b1, fr1, w2, b2, fr2, w3, w3, w)


def _filt_dft_kernel(ce_ref, co_ref, sef_ref, sof_ref, s_ref, d_ref, t_ref, *, tm, seq):
    h = seq // 2
    rows = pl.ds(pl.multiple_of(pl.program_id(1) * tm, tm), tm)
    pc = _dot(ce_ref[rows, :], s_ref[0:h, :])
    qc = _dot(co_ref[rows, :], s_ref[h:seq, :])
    ps = _dot(sef_ref[rows, :], d_ref[0:h, :])
    qs = _dot(sof_ref[rows, :], d_ref[h:seq, :])
    row = pl.program_id(1) * tm + lax.broadcasted_iota(jnp.int32, pc.shape, 0)
    wk = jnp.where(row == 0, 1.0, 2.0) * (1.0 / (2 * seq))
    t_ref[0] = ((pc + qc) * wk).astype(BF16)
    t_ref[1] = jnp.where(row == 0, 0.0, -(ps + qs) * wk).astype(BF16)
    t_ref[2] = ((pc - qc) * wk).astype(BF16)
    t_ref[3] = jnp.where(row == 0, 0.0, (ps - qs) * wk).astype(BF16)


def _filt_dft(ce, co, sef, sof, s, d, tm=512, tn=512):
    h = ce.shape[0]
    seq, cols = s.shape
    tab = pl.BlockSpec((seq, tn), lambda j, i: (0, j))
    return pl.pallas_call(
        functools.partial(_filt_dft_kernel, tm=tm, seq=seq),
        grid=(cols // tn, h // tm),
        in_specs=[_resident((h, h))] * 4 + [tab, tab],
        out_specs=pl.BlockSpec((4, tm, tn), lambda j, i: (0, i, j)),
        out_shape=jax.ShapeDtypeStruct((4, h, cols), BF16),
        compiler_params=_params("arbitrary", "arbitrary"),
        name="filt_dft",
    )(ce, co, sef, sof, s, d)


def _hy_fwd_kernel(ce_ref, co_ref, sef_ref, sof_ref, u_ref, t_ref, tmid_ref, y_ref, *, tm):
    h = ce_ref.shape[0]
    i = pl.program_id(1)
    rows = pl.ds(pl.multiple_of(i * tm, tm), tm)
    ue = u_ref[0, 0:h, :]
    uo = u_ref[0, h:2 * h, :]
    pc = _dot(ce_ref[rows, :], ue)
    qc = _dot(co_ref[rows, :], uo)
    ps = _dot(sef_ref[rows, :], ue)
    qs = _dot(sof_ref[rows, :], uo)
    r0 = (i * tm + lax.broadcasted_iota(jnp.int32, pc.shape, 0)) == 0
    tr1, ti1 = t_ref[0, rows, :].astype(F32), t_ref[1, rows, :].astype(F32)
    tr2, ti2 = t_ref[2, rows, :].astype(F32), t_ref[3, rows, :].astype(F32)
    a1, a2 = pc + qc, pc - qc
    b1 = jnp.where(r0, 0.0, ps + qs)
    b2 = jnp.where(r0, 0.0, qs - ps)
    yr1, yi1 = a1 * tr1 + b1 * ti1, b1 * tr1 - a1 * ti1
    yr2, yi2 = a2 * tr2 + b2 * ti2, b2 * tr2 - a2 * ti2
    trh, tih = tmid_ref[0:1, :], tmid_ref[1:2, :]
    y_ref[0, 0] = (yr1 + yr2).astype(BF16)
    y_ref[0, 1] = (yr1 - yr2).astype(BF16)
    y_ref[0, 2] = jnp.where(r0, ps * trh + qs * tih, yi1 - yi2).astype(BF16)
    y_ref[0, 3] = jnp.where(r0, qs * trh - ps * tih, yi1 + yi2).astype(BF16)


def _hy_fwd(mats, u3, u_col, tt, tmid, t_col, width, tm=512):
    bsz, seq, _ = u3.shape
    h = seq // 2
    return pl.pallas_call(
        functools.partial(_hy_fwd_kernel, tm=tm),
        grid=(bsz, h // tm),
        in_specs=[_resident((h, h))] * 4
                 + [pl.BlockSpec((1, seq, width), lambda b, i: (b, 0, u_col)),
                    pl.BlockSpec((4, h, width), lambda b, i: (0, 0, t_col), pipeline_mode=pl.Buffered(1)),
                    pl.BlockSpec((2, width), lambda b, i: (0, t_col))],
        out_specs=pl.BlockSpec((1, 4, tm, width), lambda b, i: (b, 0, i, 0)),
        out_shape=jax.ShapeDtypeStruct((bsz, 4, h, width), BF16),
        compiler_params=_params("arbitrary", "arbitrary"),
        name="hy_fwd",
    )(*mats, u3, tt, tmid)


def _hy_inv_kernel(ce_ref, seg_ref, cog_ref, sog_ref, y_ref, g_ref, u_ref, bias_ref, o_ref, *scr, tm, natural):
    i = pl.program_id(1)
    rows = pl.ds(pl.multiple_of(i * tm, tm), tm)
    ye = _dot(ce_ref[rows, :], y_ref[0, 0]) + _dot(seg_ref[rows, :], y_ref[0, 2])
    yo = _dot(cog_ref[rows, :], y_ref[0, 1]) + _dot(sog_ref[rows, :], y_ref[0, 3])
    bias = bias_ref[...]
    ze = g_ref[0, 0].astype(F32) * (ye + u_ref[0, 0].astype(F32) * bias)
    zo = g_ref[0, 1].astype(F32) * (yo + u_ref[0, 1].astype(F32) * bias)
    if natural:
        scr_ref, = scr
        for sl in range(ze.shape[1] // LANES):
            c = slice(sl * LANES, (sl + 1) * LANES)
            scr_ref[sl, pl.ds(0, tm, stride=2), :] = ze[:, c]
            scr_ref[sl, pl.ds(1, tm, stride=2), :] = zo[:, c]
        for sl in range(ze.shape[1] // LANES):
            o_ref[0, :, sl * LANES:(sl + 1) * LANES] = scr_ref[sl].astype(o_ref.dtype)
    else:
        o_ref[0, 0] = ze.astype(o_ref.dtype)
        o_ref[0, 1] = zo.astype(o_ref.dtype)


def _hy_inv(mats, y4, g4, g_col, u4, u_col, bias, width, natural, tm=512):
    bsz, _, h, _ = y4.shape
    par = lambda col: pl.BlockSpec((1, 2, tm, width), lambda b, i: (b, 0, i, col))
    if natural:
        out_spec = pl.BlockSpec((1, 2 * tm, width), lambda b, i: (b, i, 0))
        out_shape = jax.ShapeDtypeStruct((bsz, 2 * h, width), BF16)
        scratch = [pltpu.VMEM((width // LANES, 2 * tm, LANES), F32)]
    else:
        out_spec = par(0)
        out_shape = jax.ShapeDtypeStruct((bsz, 2, h, width), BF16)
        scratch = []
    return pl.pallas_call(
        functools.partial(_hy_inv_kernel, tm=tm, natural=natural),
        grid=(bsz, h // tm),
        in_specs=[_resident((h, h))] * 4
                 + [pl.BlockSpec((1, 4, h, width), lambda b, i: (b, 0, 0, 0)),
                    par(g_col), par(u_col),
                    pl.BlockSpec((1, width), lambda b, i: (0, 0))],
        out_specs=out_spec,
        out_shape=out_shape,
        scratch_shapes=scratch,
        compiler_params=_params("arbitrary", "arbitrary"),
        name="hy_inv",
    )(*mats, y4, g4, u4, bias)


def _cf_kernel(*refs, step_rows, grp, n_cast):
    (am_ref, ap_ref, an_ref, bm_ref, bp_ref, bn_ref, w_ref, cb_ref, lg_ref, lb_ref) = refs[:10]
    cast_in = refs[10:10 + n_cast]
    o_ref = refs[10 + n_cast]
    cast_out = refs[11 + n_cast:11 + 2 * n_cast]
    us_ref, sh_ref, wb_ref = refs[11 + 2 * n_cast:]
    width = us_ref.shape[-1]
    j = pl.program_id(1)

    for src, dst in zip(cast_in, cast_out):
        dst[...] = src[...].astype(dst.dtype)

    @pl.when((pl.program_id(0) == 0) & (j == 0))
    def _():
        for t in range(CF_KERNEL):
            wb_ref[t] = jnp.broadcast_to(w_ref[t:t + 1, :], (8, width))

    glu = lambda a, b: a[0].astype(F32) * _sigmoid(b[0].astype(F32))
    us_ref[0:CF_PAD, :] = jnp.where(j == 0, 0.0, glu(ap_ref, bp_ref))
    us_ref[CF_PAD:CF_PAD + step_rows, :] = glu(am_ref, bm_ref)
    us_ref[CF_PAD + step_rows:, :] = jnp.where(j == pl.num_programs(1) - 1, 0.0, glu(an_ref, bn_ref))

    ngrp = sh_ref.shape[1]
    blk = 8
    for g0 in range(0, ngrp, blk):
        ng = min(blk, ngrp - g0)
        win = us_ref[8 * g0:8 * (g0 + ng + 1), :]
        sh_ref[0, g0:g0 + ng] = win[0:8 * ng].reshape(ng, 8, width)
        for s in range(1, 8):
            sh = pltpu.roll(win, win.shape[0] - s, 0)
            sh_ref[s, g0:g0 + ng] = sh[0:8 * ng].reshape(ng, 8, width)

    half = CF_KERNEL // 2
    for c in range(step_rows // (8 * grp)):
        acc = jnp.zeros((grp, 8, width), F32) + cb_ref[...][None]
        for t in range(CF_KERNEL):
            off = CF_PAD - half + t
            acc = acc + wb_ref[t][None] * sh_ref[off % 8, c * grp + off // 8:c * grp + off // 8 + grp]
        mu = jnp.mean(acc, axis=-1, keepdims=True)
        cen = acc - mu
        var = jnp.mean(cen * cen, axis=-1, keepdims=True)
        y = cen * lax.rsqrt(var + EPS) * lg_ref[...][None] + lb_ref[...][None]
        o_ref[0, c * 8 * grp:(c + 1) * 8 * grp, :] = (y * _sigmoid(y)).reshape(8 * grp, width).astype(o_ref.dtype)


def _cf_conv(proj3, a_col, b_col, w, cb, lg, lb, width, casts, step_rows=128, grp=4):
    bsz, seq, _ = proj3.shape
    nsteps = seq // step_rows
    total = bsz * nsteps
    nblk = step_rows // CF_PAD
    last_blk = seq // CF_PAD - 1
    vec = pl.BlockSpec((1, width), lambda b, j: (0, 0))
    main = lambda col: pl.BlockSpec((1, step_rows, width), lambda b, j: (b, j, col))
    prev = lambda col: pl.BlockSpec((1, CF_PAD, width), lambda b, j: (b, jnp.maximum(j * nblk - 1, 0), col))
    nxt = lambda col: pl.BlockSpec((1, CF_PAD, width),
                                   lambda b, j: (b, jnp.minimum((j + 1) * nblk, last_blk), col))

    def cast_spec(c):
        n = next(n for n in (total, total // 2, total // 4) if c.shape[0] % (16 * n) == 0)
        every = total // n
        return pl.BlockSpec((c.shape[0] // n, c.shape[1]), lambda b, j: ((b * nsteps + j) // every, 0))
    cast_specs = [cast_spec(c) for c in casts]
    outs = pl.pallas_call(
        functools.partial(_cf_kernel, step_rows=step_rows, grp=grp, n_cast=len(casts)),
        grid=(bsz, nsteps),
        in_specs=[main(a_col), prev(a_col), nxt(a_col), main(b_col), prev(b_col), nxt(b_col),
                  pl.BlockSpec((CF_KERNEL, width), lambda b, j: (0, 0)),
                  vec, vec, vec] + cast_specs,
        out_specs=[pl.BlockSpec((1, step_rows, width), lambda b, j: (b, j, 0))] + cast_specs,
        out_shape=[jax.ShapeDtypeStruct((bsz, seq, width), BF16)]
                  + [jax.ShapeDtypeStruct(c.shape, BF16) for c in casts],
        scratch_shapes=[pltpu.VMEM((step_rows + 2 * CF_PAD, width), F32),
                        pltpu.VMEM((8, (step_rows + 2 * CF_PAD) // 8 - 1, 8, width), F32),
                        pltpu.VMEM((CF_KERNEL, 8, width), F32)],
        compiler_params=_params("arbitrary", "arbitrary"),
        name="cf_conv",
    )(proj3, proj3, proj3, proj3, proj3, proj3, w, cb, lg, lb, *casts)
    return outs[0], outs[1:]


def _merge_kernel(ya_ref, yb_ref, ga0_ref, ga1_ref, gb0_ref, gb1_ref, x_ref, pa_ref, pb_ref, wo_ref,
                  g1_ref, x1_ref):
    a = _dot(ya_ref[...], pa_ref[...])
    b = _dot(yb_ref[...], pb_ref[...])
    w = ga0_ref.shape[1]
    m0 = _sigmoid(ga0_ref[...].astype(F32)) * a[:, :w] + _sigmoid(gb0_ref[...].astype(F32)) * b[:, :w]
    m1 = _sigmoid(ga1_ref[...].astype(F32)) * a[:, w:] + _sigmoid(gb1_ref[...].astype(F32)) * b[:, w:]
    o = _dot(m0.astype(BF16), wo_ref[0:w, :]) + _dot(m1.astype(BF16), wo_ref[w:2 * w, :])
    x1_ref[...] = x_ref[...] + o * lax.rsqrt(jnp.mean(o * o, axis=-1, keepdims=True) + EPS) * g1_ref[...]


def _merge(ya, yb, proj, ga_col, gb_col, x2d, pa, pb, wo, g1, tm=256):
    m, d = x2d.shape
    wa = ya.shape[1]
    assert d == 2 * wa
    const = lambda shape: pl.BlockSpec(shape, lambda i: (0, 0))
    gate = lambda col: pl.BlockSpec((tm, wa), lambda i: (i, col))
    return pl.pallas_call(
        _merge_kernel,
        grid=(m // tm,),
        in_specs=[pl.BlockSpec((tm, wa), lambda i: (i, 0)),
                  pl.BlockSpec((tm, wa), lambda i: (i, 0)),
                  gate(ga_col), gate(ga_col + 1), gate(gb_col), gate(gb_col + 1),
                  pl.BlockSpec((tm, d), lambda i: (i, 0)),
                  const((wa, d)), const((wa, d)), const((d, d)), const((1, d))],
        out_specs=pl.BlockSpec((tm, d), lambda i: (i, 0)),
        out_shape=jax.ShapeDtypeStruct((m, d), F32),
        compiler_params=_params("parallel"),
        name="merge",
    )(ya, yb, proj, proj, proj, proj, x2d, pa, pb, wo, g1)


def _ffn_kernel(x1_ref, gpre_ref, wg_ref, wu_ref, wd_ref, gpost_ref, o_ref, h_ref):
    j = pl.program_id(1)

    @pl.when(j == 0)
    def _():
        x1 = x1_ref[...]
        ms = jnp.mean(x1 * x1, axis=-1, keepdims=True)
        h_ref[...] = (x1 * lax.rsqrt(ms + EPS) * gpre_ref[...]).astype(BF16)
        o_ref[...] = jnp.zeros_like(o_ref)

    h = h_ref[...]
    gate = _dot(h, wg_ref[...])
    up = _dot(h, wu_ref[...])
    act = (gate * _sigmoid(gate) * up).astype(BF16)
    o_ref[...] += _dot(act, wd_ref[...])

    @pl.when(j == pl.num_programs(1) - 1)
    def _():
        a = o_ref[...]
        o_ref[...] = x1_ref[...] + a * lax.rsqrt(jnp.mean(a * a, axis=-1, keepdims=True) + EPS) * gpost_ref[...]


def _ffn(x1, gpre, wgu, wd, gpost, tm=1024, th=256):
    m, d = x1.shape
    hidden = wd.shape[0]
    nh = hidden // th
    return pl.pallas_call(
        _ffn_kernel,
        grid=(m // tm, nh),
        in_specs=[pl.BlockSpec((tm, d), lambda i, j: (i, 0)),
                  pl.BlockSpec((1, d), lambda i, j: (0, 0)),
                  pl.BlockSpec((d, th), lambda i, j: (0, j)),
                  pl.BlockSpec((d, th), lambda i, j: (0, nh + j)),
                  pl.BlockSpec((th, d), lambda i, j: (j, 0)),
                  pl.BlockSpec((1, d), lambda i, j: (0, 0))],
        out_specs=pl.BlockSpec((tm, d), lambda i, j: (i, 0)),
        out_shape=jax.ShapeDtypeStruct((m, d), F32),
        scratch_shapes=[pltpu.VMEM((tm, d), BF16)],
        compiler_params=_params("parallel", "arbitrary"),
        name="ffn",
    )(x1, gpre, wgu, wgu, wd, gpost)


def kernel(x, mix_pre_g, w_in, hy_conv_w, hy_conv_b, hy_filt_w1, hy_filt_b1, hy_filt_fr1, hy_filt_w2,
           hy_filt_b2, hy_filt_fr2, hy_filt_w3, hy_bias, hy_proj, cf_dw_w, cf_dw_b, cf_ln_g, cf_ln_b,
           cf_proj, w_out, mix_post_g, ffn_pre_g, ffn_w_gu, ffn_w_down, ffn_post_g):
    bsz, seq, d = x.shape
    depth = w_in.shape[0]
    hw = hy_proj.shape[1]
    cw = cf_proj.shape[1]
    assert hw == cw and d % hw == 0
    row = lambda v: v.reshape(1, -1)

    assert depth == 1
    x2d = x.reshape(bsz * seq, d)
    for l in range(depth):
        ce, co, sef, sof, seg, cog, sog, w_in_a = _dft_tables(seq, w_in[l], d // 2)
        s, dm, tmid, w_in_b = _filt_mlp(seq, hw, hy_filt_w1[l], row(hy_filt_b1[l]), row(hy_filt_fr1[l]),
                                        hy_filt_w2[l], row(hy_filt_b2[l]), row(hy_filt_fr2[l]),
                                        hy_filt_w3[l], w_in[l], d // 2)
        proj = _in_proj(x2d, row(mix_pre_g[l]), w_in_a, w_in_b)
        proj3 = proj.reshape(bsz, seq, -1)
        cf_a_col, cf_b_col = 3, 4
        ga_col, gb_col = 5, 5 + d // hw

        hy = _hy_conv3(proj3, hy_conv_w[l], row(hy_conv_b[l]), 3 * hw)
        hy4 = hy.reshape(bsz, 2, seq // 2, 3 * hw)
        tt = _filt_dft(ce, co, sef, sof, s, dm)
        fwd, inv = (ce, co, sef, sof), (ce, seg, cog, sog)
        y1 = _hy_fwd(fwd, hy, 0, tt, tmid, 0, hw)
        z4 = _hy_inv(inv, y1, hy4, 1, hy4, 0, row(hy_bias[l, 0]), hw, natural=False)
        y2 = _hy_fwd(fwd, z4.reshape(bsz, seq, hw), 0, tt, tmid, 1, hw)
        y_a = _hy_inv(inv, y2, hy4, 2, z4, 0, row(hy_bias[l, 1]), hw, natural=True)

        y_b, (hy_proj_bf, cf_proj_bf, w_out_bf, w_gu_bf, w_down_bf) = _cf_conv(
            proj3, cf_a_col, cf_b_col, cf_dw_w[l], row(cf_dw_b[l]), row(cf_ln_g[l]), row(cf_ln_b[l]), cw,
            [hy_proj[l], cf_proj[l], w_out[l], ffn_w_gu[l], ffn_w_down[l]])

        x1 = _merge(y_a.reshape(bsz * seq, hw), y_b.reshape(bsz * seq, cw), proj, ga_col, gb_col, x2d,
                    hy_proj_bf, cf_proj_bf, w_out_bf, row(mix_post_g[l]))
        x2d = _ffn(x1, row(ffn_pre_g[l]), w_gu_bf, w_down_bf, row(ffn_post_g[l]))
    return x2d.reshape(bsz, seq, d)
```

```python
import functools
import math

import jax
import jax.numpy as jnp
from jax import lax
from jax.experimental import pallas as pl
from jax.experimental.pallas import tpu as pltpu

F32 = jnp.float32
BF16 = jnp.bfloat16
EPS = 1e-6

HY_ORDER = 2
HY_EMB_BANDS = 16
HY_DECAY_TARGET = 1e-2
HY_MIN_DECAY = math.log(HY_DECAY_TARGET) / 1.5
HY_MAX_DECAY = math.log(HY_DECAY_TARGET) / 0.3
HY_SHORT = 3
CF_KERNEL = 31
CF_PAD = 16
LANES = 128

V7X_VMEM_LIMIT_BYTES = 56 * 1024 * 1024


def _params(*sem):
    return pltpu.CompilerParams(dimension_semantics=sem, vmem_limit_bytes=V7X_VMEM_LIMIT_BYTES)


def _dot(a, b):
    return jnp.dot(a, b, preferred_element_type=F32)


def _sigmoid(x):
    return 1.0 / (1.0 + jnp.exp(-x))


def _resident(shape):
    zeros = (0,) * len(shape)
    return pl.BlockSpec(shape, lambda *_: zeros, pipeline_mode=pl.Buffered(1))


def _parity_split_store(val, scr_ref, dst_ref):
    rows, cols = val.shape
    h = rows // 2
    for sl in range(cols // LANES):
        scr_ref[sl] = val[:, sl * LANES:(sl + 1) * LANES]
    for sl in range(cols // LANES):
        c = slice(sl * LANES, (sl + 1) * LANES)
        dst_ref[0:h, c] = scr_ref[sl, pl.ds(0, h, stride=2), :].astype(dst_ref.dtype)
        dst_ref[h:rows, c] = scr_ref[sl, pl.ds(1, h, stride=2), :].astype(dst_ref.dtype)


def _dft_kernel(w_ref, ce_ref, co_ref, sef_ref, sof_ref, seg_ref, cog_ref, sog_ref, wbf_ref,
                rc_ref, rs_ref, pc_ref, ps_ref, *, tm, h, chunk):
    i = pl.program_id(0)
    mask = 4 * h - 1
    scale = 2.0 * math.pi / (4 * h)
    wbf_ref[...] = w_ref[...].astype(BF16)

    @pl.when(i == 0)
    def _():
        def base(j, carry):
            r0 = pl.multiple_of(j * chunk, chunk)
            r = r0 + lax.broadcasted_iota(jnp.int32, (chunk, h), 0)
            c = lax.broadcasted_iota(jnp.int32, (chunk, h), 1)
            th_e = ((2 * r * c) & mask).astype(F32) * scale
            th_o = ((r * (2 * c + 1)) & mask).astype(F32) * scale
            rc_ref[pl.ds(r0, chunk), :] = jnp.cos(th_e)
            rs_ref[pl.ds(r0, chunk), :] = jnp.sin(th_e)
            pc_ref[pl.ds(r0, chunk), :] = jnp.cos(th_o)
            ps_ref[pl.ds(r0, chunk), :] = jnp.sin(th_o)
            return carry
        lax.fori_loop(0, tm // chunk, base, 0)

    k0 = i * tm
    c1 = lax.broadcasted_iota(jnp.int32, (1, h), 1)
    th = ((2 * k0 * c1) & mask).astype(F32) * scale
    ce0, se0 = jnp.cos(th), jnp.sin(th)
    th = ((k0 * (2 * c1 + 1)) & mask).astype(F32) * scale
    co0, so0 = jnp.cos(th), jnp.sin(th)
    th = (((2 * k0 + 1) * c1) & mask).astype(F32) * scale
    cg0, sg0 = jnp.cos(th), jnp.sin(th)

    def tile(j, carry):
        r0 = pl.multiple_of(j * chunk, chunk)
        rows = pl.ds(r0, chunk)
        rc, rs, pc, ps = rc_ref[rows, :], rs_ref[rows, :], pc_ref[rows, :], ps_ref[rows, :]
        row = k0 + r0 + lax.broadcasted_iota(jnp.int32, (chunk, h), 0)
        col = lax.broadcasted_iota(jnp.int32, (chunk, h), 1)
        alt_col = (1 - 2 * (col & 1)).astype(F32)
        alt_row = (1 - 2 * (row & 1)).astype(F32)
        se = rs * ce0 + rc * se0
        ce_ref[rows, :] = (rc * ce0 - rs * se0).astype(BF16)
        co_ref[rows, :] = (pc * co0 - ps * so0).astype(BF16)
        sef_ref[rows, :] = jnp.where(row == 0, alt_col, se).astype(BF16)
        sof_ref[rows, :] = jnp.where(row == 0, alt_col, ps * co0 + pc * so0).astype(BF16)
        seg_ref[rows, :] = jnp.where(col == 0, alt_row, se).astype(BF16)
        cog_ref[rows, :] = (rc * cg0 - rs * sg0).astype(BF16)
        sog_ref[rows, :] = jnp.where(col == 0, alt_row, rs * cg0 + rc * sg0).astype(BF16)
        return carry
    lax.fori_loop(0, tm // chunk, tile, 0)


def _dft_tables(seq, w, w_rows, tm=256, chunk=32):
    h = seq // 2
    steps = h // tm
    wr = w_rows // steps
    out = jax.ShapeDtypeStruct((h, h), BF16)
    spec = pl.BlockSpec((tm, h), lambda i: (i, 0))
    wspec = pl.BlockSpec((wr, w.shape[1]), lambda i: (i, 0))
    return pl.pallas_call(
        functools.partial(_dft_kernel, tm=tm, h=h, chunk=chunk),
        grid=(steps,),
        in_specs=[wspec],
        out_specs=[spec] * 7 + [wspec],
        out_shape=[out] * 7 + [jax.ShapeDtypeStruct((w_rows, w.shape[1]), BF16)],
        scratch_shapes=[pltpu.VMEM((tm, h), F32)] * 4,
        compiler_params=_params("arbitrary"),
        name="dft_tables",
    )(w)


def _in_proj_kernel(x_ref, g_ref, wa_ref, wb_ref, o_ref, h_ref):
    @pl.when(pl.program_id(1) == 0)
    def _():
        x = x_ref[...]
        ms = jnp.mean(x * x, axis=-1, keepdims=True)
        h_ref[...] = (x * lax.rsqrt(ms + EPS) * g_ref[...]).astype(BF16)

    ka = wa_ref.shape[0]
    acc = _dot(h_ref[:, 0:ka], wa_ref[...]) + _dot(h_ref[:, ka:], wb_ref[...])
    o_ref[...] = acc.astype(o_ref.dtype)


def _in_proj(x2d, g, wa, wb, tm=1024, tn=1536):
    m, d = x2d.shape
    n = wa.shape[1]
    assert wa.shape[0] + wb.shape[0] == d
    return pl.pallas_call(
        _in_proj_kernel,
        grid=(m // tm, n // tn),
        in_specs=[pl.BlockSpec((tm, d), lambda i, j: (i, 0)),
                  pl.BlockSpec((1, d), lambda i, j: (0, 0)),
                  pl.BlockSpec((wa.shape[0], tn), lambda i, j: (0, j)),
                  pl.BlockSpec((wb.shape[0], tn), lambda i, j: (0, j))],
        out_specs=pl.BlockSpec((tm, tn), lambda i, j: (i, j)),
        out_shape=jax.ShapeDtypeStruct((m, n), BF16),
        scratch_shapes=[pltpu.VMEM((tm, d), BF16)],
        compiler_params=_params("parallel", "arbitrary"),
        name="in_proj",
    )(x2d, g, wa, wb)


def _conv3_kernel(p_ref, w_ref, b_ref, o_ref, scr_ref):
    rows, cols = p_ref.shape[1], p_ref.shape[2]
    h = rows // 2
    pad = 8
    zeros = jnp.zeros((pad, LANES), F32)
    for sl in range(cols // LANES):
        c = slice(sl * LANES, (sl + 1) * LANES)
        scr_ref[sl, 0:pad, :] = zeros
        scr_ref[sl, pad:pad + rows, :] = p_ref[0, :, c].astype(F32)
        scr_ref[sl, pad + rows:, :] = zeros
    for sl in range(cols // LANES):
        c = slice(sl * LANES, (sl + 1) * LANES)
        w0, w1, w2, b = w_ref[0:1, c], w_ref[1:2, c], w_ref[2:3, c], b_ref[:, c]
        xom = scr_ref[sl, pl.ds(pad - 1, h, stride=2), :]
        xe = scr_ref[sl, pl.ds(pad, h, stride=2), :]
        xo = scr_ref[sl, pl.ds(pad + 1, h, stride=2), :]
        xep = scr_ref[sl, pl.ds(pad + 2, h, stride=2), :]
        o_ref[0, 0:h, c] = (w0 * xom + w1 * xe + w2 * xo + b).astype(o_ref.dtype)
        o_ref[0, h:rows, c] = (w0 * xe + w1 * xo + w2 * xep + b).astype(o_ref.dtype)


def _hy_conv3(proj3, w, b, width, ct=512):
    bsz, seq, _ = proj3.shape
    return pl.pallas_call(
        _conv3_kernel,
        grid=(bsz, width // ct),
        in_specs=[pl.BlockSpec((1, seq, ct), lambda i, j: (i, 0, j)),
                  pl.BlockSpec((HY_SHORT, ct), lambda i, j: (0, j)),
                  pl.BlockSpec((1, ct), lambda i, j: (0, j))],
        out_specs=pl.BlockSpec((1, seq, ct), lambda i, j: (i, 0, j)),
        out_shape=jax.ShapeDtypeStruct((bsz, seq, width), BF16),
        scratch_shapes=[pltpu.VMEM((ct // LANES, seq + 16, LANES), F32)],
        compiler_params=_params("parallel", "parallel"),
        name="hy_conv3",
    )(proj3, w, b)


def _filt_mlp_kernel(w1t_ref, w1c_ref, w1s_ref, b1_ref, fr1_ref, w2_ref, b2_ref, fr2_ref,
                     w3f_ref, w3b_ref, w_ref, s_ref, d_ref, tmid_ref, wbf_ref, h2_ref, scr_ref,
                     *, seq, width, ct):
    jc = pl.program_id(1)
    wbf_ref[...] = w_ref[...].astype(BF16)
    n = lax.broadcasted_iota(jnp.int32, (seq, 1), 0).astype(F32)
    t = n / (seq - 1)

    @pl.when((pl.program_id(0) == 0) & (jc == 0))
    def _():
        wang = 2.0 * math.pi * n / seq
        band = lax.broadcasted_iota(jnp.int32, (1, HY_EMB_BANDS), 1).astype(F32)
        f = 1e-4 + band * ((HY_EMB_BANDS - 1 - 1e-4) / (HY_EMB_BANDS - 1))
        fw = f * wang
        pre1 = (t * w1t_ref[...] + _dot(jnp.cos(fw), w1c_ref[...]) - _dot(jnp.sin(fw), w1s_ref[...])
                + b1_ref[...])
        h1 = jnp.sin(fr1_ref[...] * pre1)
        h2_ref[...] = jnp.sin(fr2_ref[...] * (_dot(h1, w2_ref[...]) + b2_ref[...]))

    h2 = h2_ref[...]
    ch = (jc * ct + lax.broadcasted_iota(jnp.int32, (1, ct), 1)).astype(F32)
    delta = HY_MIN_DECAY + ch * ((HY_MAX_DECAY - HY_MIN_DECAY) / (width - 1))
    decay = jnp.exp(-t * jnp.abs(delta))
    kf = _dot(h2, w3f_ref[...]) * decay
    kb = _dot(h2, w3b_ref[...]) * decay
    row = lax.broadcasted_iota(jnp.int32, (seq, ct), 0)
    kb = jnp.where(row == 0, 0.0, kb)
    l1 = jnp.sum(jnp.abs(kf), axis=0, keepdims=True) + jnp.sum(jnp.abs(kb), axis=0, keepdims=True)
    inv = 1.0 / l1
    s = (kf + kb) * inv
    dm = (kf - kb) * inv
    phase = row & 3
    cmid = jnp.where(phase == 0, 1.0, jnp.where(phase == 2, -1.0, 0.0))
    smid = jnp.where(phase == 1, 1.0, jnp.where(phase == 3, -1.0, 0.0))
    wmid = 2.0 / (2 * seq)
    tmid_ref[0:1, :] = jnp.sum(s * cmid, axis=0, keepdims=True) * wmid
    tmid_ref[1:2, :] = jnp.sum(dm * smid, axis=0, keepdims=True) * (-wmid)
    _parity_split_store(s, scr_ref, s_ref)
    _parity_split_store(dm, scr_ref, d_ref)


def _filt_mlp(seq, width, w1, b1, fr1, w2, b2, fr2, w3, w, w_row0, ct=256):
    hid = w2.shape[0]
    nct = width // ct
    steps = HY_ORDER * nct
    w_rows = w.shape[0] - w_row0
    wr = w_rows // steps
    assert w_row0 % wr == 0
    small = lambda shape: pl.BlockSpec(shape, lambda o, j: (0, 0))
    out_spec = pl.BlockSpec((seq, ct), lambda o, j: (0, o * nct + j))
    return pl.pallas_call(
        functools.partial(_filt_mlp_kernel, seq=seq, width=width, ct=ct),
        grid=(HY_ORDER, nct),
        in_specs=[small((1, hid)), small((HY_EMB_BANDS, hid)), small((HY_EMB_BANDS, hid)),
                  small((1, hid)), small((1, hid)), small((hid, hid)), small((1, hid)), small((1, hid)),
                  pl.BlockSpec((hid, ct), lambda o, j: (0, o * 2 * nct + j)),
                  pl.BlockSpec((hid, ct), lambda o, j: (0, o * 2 * nct + nct + j)),
                  pl.BlockSpec((wr, w.shape[1]), lambda o, j: (w_row0 // wr + o * nct + j, 0))],
        out_specs=[out_spec, out_spec, pl.BlockSpec((2, ct), lambda o, j: (0, o * nct + j)),
                   pl.BlockSpec((wr, w.shape[1]), lambda o, j: (o * nct + j, 0))],
        out_shape=[jax.ShapeDtypeStruct((seq, HY_ORDER * width), BF16),
                   jax.ShapeDtypeStruct((seq, HY_ORDER * width), BF16),
                   jax.ShapeDtypeStruct((2, HY_ORDER * width), F32),
                   jax.ShapeDtypeStruct((w_rows, w.shape[1]), BF16)],
        scratch_shapes=[pltpu.VMEM((seq, hid), F32), pltpu.VMEM((ct // LANES, seq, LANES), F32)],
        compiler_params=_params("arbitrary", "arbitrary"),
        name="filt_mlp",
    )(w1[0:1], w1[1:1 + HY_EMB_BANDS], w1[1 + HY_EMB_BANDS:], b1, fr1, w2, b2, fr2, w3, w3, w)


def _filt_dft_kernel(ce_ref, co_ref, sef_ref, sof_ref, s_ref, d_ref, t_ref, *, tm, seq):
    h = seq // 2
    rows = pl.ds(pl.multiple_of(pl.program_id(1) * tm, tm), tm)
    pc = _dot(ce_ref[rows, :], s_ref[0:h, :])
    qc = _dot(co_ref[rows, :], s_ref[h:seq, :])
    ps = _dot(sef_ref[rows, :], d_ref[0:h, :])
    qs = _dot(sof_ref[rows, :], d_ref[h:seq, :])
    row = pl.program_id(1) * tm + lax.broadcasted_iota(jnp.int32, pc.shape, 0)
    wk = jnp.where(row == 0, 1.0, 2.0) * (1.0 / (2 * seq))
    t_ref[0] = ((pc + qc) * wk).astype(BF16)
    t_ref[1] = jnp.where(row == 0, 0.0, -(ps + qs) * wk).astype(BF16)
    t_ref[2] = ((pc - qc) * wk).astype(BF16)
    t_ref[3] = jnp.where(row == 0, 0.0, (ps - qs) * wk).astype(BF16)


def _filt_dft(ce, co, sef, sof, s, d, tm=512, tn=512):
    h = ce.shape[0]
    seq, cols = s.shape
    tab = pl.BlockSpec((seq, tn), lambda j, i: (0, j))
    return pl.pallas_call(
        functools.partial(_filt_dft_kernel, tm=tm, seq=seq),
        grid=(cols // tn, h // tm),
        in_specs=[_resident((h, h))] * 4 + [tab, tab],
        out_specs=pl.BlockSpec((4, tm, tn), lambda j, i: (0, i, j)),
        out_shape=jax.ShapeDtypeStruct((4, h, cols), BF16),
        compiler_params=_params("arbitrary", "arbitrary"),
        name="filt_dft",
    )(ce, co, sef, sof, s, d)


def _hy_fwd_kernel(ce_ref, co_ref, sef_ref, sof_ref, u_ref, t_ref, tmid_ref, y_ref, *, tm):
    h = ce_ref.shape[0]
    i = pl.program_id(1)
    rows = pl.ds(pl.multiple_of(i * tm, tm), tm)
    ue = u_ref[0, 0:h, :]
    uo = u_ref[0, h:2 * h, :]
    pc = _dot(ce_ref[rows, :], ue)
    qc = _dot(co_ref[rows, :], uo)
    ps = _dot(sef_ref[rows, :], ue)
    qs = _dot(sof_ref[rows, :], uo)
    r0 = (i * tm + lax.broadcasted_iota(jnp.int32, pc.shape, 0)) == 0
    tr1, ti1 = t_ref[0, rows, :].astype(F32), t_ref[1, rows, :].astype(F32)
    tr2, ti2 = t_ref[2, rows, :].astype(F32), t_ref[3, rows, :].astype(F32)
    a1, a2 = pc + qc, pc - qc
    b1 = jnp.where(r0, 0.0, ps + qs)
    b2 = jnp.where(r0, 0.0, qs - ps)
    yr1, yi1 = a1 * tr1 + b1 * ti1, b1 * tr1 - a1 * ti1
    yr2, yi2 = a2 * tr2 + b2 * ti2, b2 * tr2 - a2 * ti2
    trh, tih = tmid_ref[0:1, :], tmid_ref[1:2, :]
    y_ref[0, 0] = (yr1 + yr2).astype(BF16)
    y_ref[0, 1] = (yr1 - yr2).astype(BF16)
    y_ref[0, 2] = jnp.where(r0, ps * trh + qs * tih, yi1 - yi2).astype(BF16)
    y_ref[0, 3] = jnp.where(r0, qs * trh - ps * tih, yi1 + yi2).astype(BF16)


def _hy_fwd(mats, u3, u_col, tt, tmid, t_col, width, tm=512):
    bsz, seq, _ = u3.shape
    h = seq // 2
    return pl.pallas_call(
        functools.partial(_hy_fwd_kernel, tm=tm),
        grid=(bsz, h // tm),
        in_specs=[_resident((h, h))] * 4
                 + [pl.BlockSpec((1, seq, width), lambda b, i: (b, 0, u_col)),
                    pl.BlockSpec((4, h, width), lambda b, i: (0, 0, t_col), pipeline_mode=pl.Buffered(1)),
                    pl.BlockSpec((2, width), lambda b, i: (0, t_col))],
        out_specs=pl.BlockSpec((1, 4, tm, width), lambda b, i: (b, 0, i, 0)),
        out_shape=jax.ShapeDtypeStruct((bsz, 4, h, width), BF16),
        compiler_params=_params("arbitrary", "arbitrary"),
        name="hy_fwd",
    )(*mats, u3, tt, tmid)


def _hy_inv_kernel(ce_ref, seg_ref, cog_ref, sog_ref, y_ref, g_ref, u_ref, bias_ref, o_ref, *scr, tm, natural):
    i = pl.program_id(1)
    rows = pl.ds(pl.multiple_of(i * tm, tm), tm)
    ye = _dot(ce_ref[rows, :], y_ref[0, 0]) + _dot(seg_ref[rows, :], y_ref[0, 2])
    yo = _dot(cog_ref[rows, :], y_ref[0, 1]) + _dot(sog_ref[rows, :], y_ref[0, 3])
    bias = bias_ref[...]
    ze = g_ref[0, 0].astype(F32) * (ye + u_ref[0, 0].astype(F32) * bias)
    zo = g_ref[0, 1].astype(F32) * (yo + u_ref[0, 1].astype(F32) * bias)
    if natural:
        scr_ref, = scr
        for sl in range(ze.shape[1] // LANES):
            c = slice(sl * LANES, (sl + 1) * LANES)
            scr_ref[sl, pl.ds(0, tm, stride=2), :] = ze[:, c]
            scr_ref[sl, pl.ds(1, tm, stride=2), :] = zo[:, c]
        for sl in range(ze.shape[1] // LANES):
            o_ref[0, :, sl * LANES:(sl + 1) * LANES] = scr_ref[sl].astype(o_ref.dtype)
    else:
        o_ref[0, 0] = ze.astype(o_ref.dtype)
        o_ref[0, 1] = zo.astype(o_ref.dtype)


def _hy_inv(mats, y4, g4, g_col, u4, u_col, bias, width, natural, tm=512):
    bsz, _, h, _ = y4.shape
    par = lambda col: pl.BlockSpec((1, 2, tm, width), lambda b, i: (b, 0, i, col))
    if natural:
        out_spec = pl.BlockSpec((1, 2 * tm, width), lambda b, i: (b, i, 0))
        out_shape = jax.ShapeDtypeStruct((bsz, 2 * h, width), BF16)
        scratch = [pltpu.VMEM((width // LANES, 2 * tm, LANES), F32)]
    else:
        out_spec = par(0)
        out_shape = jax.ShapeDtypeStruct((bsz, 2, h, width), BF16)
        scratch = []
    return pl.pallas_call(
        functools.partial(_hy_inv_kernel, tm=tm, natural=natural),
        grid=(bsz, h // tm),
        in_specs=[_resident((h, h))] * 4
                 + [pl.BlockSpec((1, 4, h, width), lambda b, i: (b, 0, 0, 0)),
                    par(g_col), par(u_col),
                    pl.BlockSpec((1, width), lambda b, i: (0, 0))],
        out_specs=out_spec,
        out_shape=out_shape,
        scratch_shapes=scratch,
        compiler_params=_params("arbitrary", "arbitrary"),
        name="hy_inv",
    )(*mats, y4, g4, u4, bias)


def _cf_kernel(*refs, step_rows, n_cast):
    (am_ref, ap_ref, an_ref, bm_ref, bp_ref, bn_ref, w_ref, cb_ref, lg_ref, lb_ref) = refs[:10]
    cast_in = refs[10:10 + n_cast]
    o_ref = refs[10 + n_cast]
    cast_out = refs[11 + n_cast:11 + 2 * n_cast]
    us_ref, cs_ref, wb_ref = refs[11 + 2 * n_cast:]
    width = wb_ref.shape[-1]
    nsl = width // LANES
    j = pl.program_id(1)

    for src, dst in zip(cast_in, cast_out):
        dst[...] = src[...].astype(dst.dtype)

    @pl.when((pl.program_id(0) == 0) & (j == 0))
    def _():
        for t in range(CF_KERNEL):
            wb_ref[t] = jnp.broadcast_to(w_ref[t:t + 1, :], (8, width))

    glu = lambda a, b: a[0].astype(F32) * _sigmoid(b[0].astype(F32))
    u_prev = jnp.where(j == 0, 0.0, glu(ap_ref, bp_ref))
    u_main = glu(am_ref, bm_ref)
    u_next = jnp.where(j == pl.num_programs(1) - 1, 0.0, glu(an_ref, bn_ref))
    for sl in range(nsl):
        c = slice(sl * LANES, (sl + 1) * LANES)
        us_ref[sl, 0:CF_PAD, :] = u_prev[:, c]
        us_ref[sl, CF_PAD:CF_PAD + step_rows, :] = u_main[:, c]
        us_ref[sl, CF_PAD + step_rows:, :] = u_next[:, c]

    half = CF_KERNEL // 2
    nph = 4
    prow = step_rows // nph
    for sl in range(nsl):
        c = slice(sl * LANES, (sl + 1) * LANES)
        accs = [jnp.zeros((prow // 8, 8, LANES), F32) + cb_ref[:, c][None]] * nph
        for r in range(CF_KERNEL + nph - 1):
            tap = us_ref[sl, pl.ds(CF_PAD - half + r, prow, stride=nph), :].reshape(prow // 8, 8, LANES)
            for p in range(nph):
                if 0 <= r - p < CF_KERNEL:
                    accs[p] = accs[p] + wb_ref[r - p, :, c][None] * tap
        for p in range(nph):
            cs_ref[sl, pl.ds(p, prow, stride=nph), :] = accs[p].reshape(prow, LANES)

    tot = cs_ref[0]
    for sl in range(1, nsl):
        tot = tot + cs_ref[sl]
    mu = jnp.sum(tot, axis=-1, keepdims=True) * (1.0 / width)
    sq = None
    for sl in range(nsl):
        cen = cs_ref[sl] - mu
        sq = cen * cen if sq is None else sq + cen * cen
    rstd = lax.rsqrt(jnp.sum(sq, axis=-1, keepdims=True) * (1.0 / width) + EPS)
    for sl in range(nsl):
        c = slice(sl * LANES, (sl + 1) * LANES)
        y = (cs_ref[sl] - mu) * rstd * lg_ref[:, c] + lb_ref[:, c]
        o_ref[0, :, c] = (y * _sigmoid(y)).astype(o_ref.dtype)


def _cf_conv(proj3, a_col, b_col, w, cb, lg, lb, width, casts, step_rows=128):
    bsz, seq, _ = proj3.shape
    nsteps = seq // step_rows
    total = bsz * nsteps
    nblk = step_rows // CF_PAD
    last_blk = seq // CF_PAD - 1
    vec = pl.BlockSpec((1, width), lambda b, j: (0, 0))
    main = lambda col: pl.BlockSpec((1, step_rows, width), lambda b, j: (b, j, col))
    prev = lambda col: pl.BlockSpec((1, CF_PAD, width), lambda b, j: (b, jnp.maximum(j * nblk - 1, 0), col))
    nxt = lambda col: pl.BlockSpec((1, CF_PAD, width),
                                   lambda b, j: (b, jnp.minimum((j + 1) * nblk, last_blk), col))

    def cast_spec(c):
        n = next(n for n in (total, total // 2, total // 4) if c.shape[0] % (16 * n) == 0)
        every = total // n
        return pl.BlockSpec((c.shape[0] // n, c.shape[1]), lambda b, j: ((b * nsteps + j) // every, 0))
    cast_specs = [cast_spec(c) for c in casts]
    outs = pl.pallas_call(
        functools.partial(_cf_kernel, step_rows=step_rows, n_cast=len(casts)),
        grid=(bsz, nsteps),
        in_specs=[main(a_col), prev(a_col), nxt(a_col), main(b_col), prev(b_col), nxt(b_col),
                  pl.BlockSpec((CF_KERNEL, width), lambda b, j: (0, 0)),
                  vec, vec, vec] + cast_specs,
        out_specs=[pl.BlockSpec((1, step_rows, width), lambda b, j: (b, j, 0))] + cast_specs,
        out_shape=[jax.ShapeDtypeStruct((bsz, seq, width), BF16)]
                  + [jax.ShapeDtypeStruct(c.shape, BF16) for c in casts],
        scratch_shapes=[pltpu.VMEM((width // LANES, step_rows + 2 * CF_PAD, LANES), F32),
                        pltpu.VMEM((width // LANES, step_rows, LANES), F32),
                        pltpu.VMEM((CF_KERNEL, 8, width), F32)],
        compiler_params=_params("arbitrary", "arbitrary"),
        name="cf_conv",
    )(proj3, proj3, proj3, proj3, proj3, proj3, w, cb, lg, lb, *casts)
    return outs[0], outs[1:]


def _merge_kernel(ya_ref, yb_ref, ga0_ref, ga1_ref, gb0_ref, gb1_ref, x_ref, pa_ref, pb_ref, wo_ref,
                  g1_ref, x1_ref):
    a = _dot(ya_ref[...], pa_ref[...])
    b = _dot(yb_ref[...], pb_ref[...])
    w = ga0_ref.shape[1]
    m0 = _sigmoid(ga0_ref[...].astype(F32)) * a[:, :w] + _sigmoid(gb0_ref[...].astype(F32)) * b[:, :w]
    m1 = _sigmoid(ga1_ref[...].astype(F32)) * a[:, w:] + _sigmoid(gb1_ref[...].astype(F32)) * b[:, w:]
    o = _dot(m0.astype(BF16), wo_ref[0:w, :]) + _dot(m1.astype(BF16), wo_ref[w:2 * w, :])
    x1_ref[...] = x_ref[...] + o * lax.rsqrt(jnp.mean(o * o, axis=-1, keepdims=True) + EPS) * g1_ref[...]


def _merge(ya, yb, proj, ga_col, gb_col, x2d, pa, pb, wo, g1, tm=256):
    m, d = x2d.shape
    wa = ya.shape[1]
    assert d == 2 * wa
    const = lambda shape: pl.BlockSpec(shape, lambda i: (0, 0))
    gate = lambda col: pl.BlockSpec((tm, wa), lambda i: (i, col))
    return pl.pallas_call(
        _merge_kernel,
        grid=(m // tm,),
        in_specs=[pl.BlockSpec((tm, wa), lambda i: (i, 0)),
                  pl.BlockSpec((tm, wa), lambda i: (i, 0)),
                  gate(ga_col), gate(ga_col + 1), gate(gb_col), gate(gb_col + 1),
                  pl.BlockSpec((tm, d), lambda i: (i, 0)),
                  const((wa, d)), const((wa, d)), const((d, d)), const((1, d))],
        out_specs=pl.BlockSpec((tm, d), lambda i: (i, 0)),
        out_shape=jax.ShapeDtypeStruct((m, d), F32),
        compiler_params=_params("parallel"),
        name="merge",
    )(ya, yb, proj, proj, proj, proj, x2d, pa, pb, wo, g1)


def _ffn_kernel(x1_ref, gpre_ref, wg_ref, wu_ref, wd_ref, gpost_ref, o_ref, h_ref):
    j = pl.program_id(1)

    @pl.when(j == 0)
    def _():
        x1 = x1_ref[...]
        ms = jnp.mean(x1 * x1, axis=-1, keepdims=True)
        h_ref[...] = (x1 * lax.rsqrt(ms + EPS) * gpre_ref[...]).astype(BF16)
        o_ref[...] = jnp.zeros_like(o_ref)

    h = h_ref[...]
    gate = _dot(h, wg_ref[...])
    up = _dot(h, wu_ref[...])
    act = (gate * _sigmoid(gate) * up).astype(BF16)
    o_ref[...] += _dot(act, wd_ref[...])

    @pl.when(j == pl.num_programs(1) - 1)
    def _():
        a = o_ref[...]
        o_ref[...] = x1_ref[...] + a * lax.rsqrt(jnp.mean(a * a, axis=-1, keepdims=True) + EPS) * gpost_ref[...]


def _ffn(x1, gpre, wgu, wd, gpost, tm=1024, th=256):
    m, d = x1.shape
    hidden = wd.shape[0]
    nh = hidden // th
    return pl.pallas_call(
        _ffn_kernel,
        grid=(m // tm, nh),
        in_specs=[pl.BlockSpec((tm, d), lambda i, j: (i, 0)),
                  pl.BlockSpec((1, d), lambda i, j: (0, 0)),
                  pl.BlockSpec((d, th), lambda i, j: (0, j)),
                  pl.BlockSpec((d, th), lambda i, j: (0, nh + j)),
                  pl.BlockSpec((th, d), lambda i, j: (j, 0)),
                  pl.BlockSpec((1, d), lambda i, j: (0, 0))],
        out_specs=pl.BlockSpec((tm, d), lambda i, j: (i, 0)),
        out_shape=jax.ShapeDtypeStruct((m, d), F32),
        scratch_shapes=[pltpu.VMEM((tm, d), BF16)],
        compiler_params=_params("parallel", "arbitrary"),
        name="ffn",
    )(x1, gpre, wgu, wgu, wd, gpost)


def kernel(x, mix_pre_g, w_in, hy_conv_w, hy_conv_b, hy_filt_w1, hy_filt_b1, hy_filt_fr1, hy_filt_w2,
           hy_filt_b2, hy_filt_fr2, hy_filt_w3, hy_bias, hy_proj, cf_dw_w, cf_dw_b, cf_ln_g, cf_ln_b,
           cf_proj, w_out, mix_post_g, ffn_pre_g, ffn_w_gu, ffn_w_down, ffn_post_g):
    bsz, seq, d = x.shape
    depth = w_in.shape[0]
    hw = hy_proj.shape[1]
    cw = cf_proj.shape[1]
    assert hw == cw and d % hw == 0
    row = lambda v: v.reshape(1, -1)

    assert depth == 1
    x2d = x.reshape(bsz * seq, d)
    for l in range(depth):
        ce, co, sef, sof, seg, cog, sog, w_in_a = _dft_tables(seq, w_in[l], d // 2)
        s, dm, tmid, w_in_b = _filt_mlp(seq, hw, hy_filt_w1[l], row(hy_filt_b1[l]), row(hy_filt_fr1[l]),
                                        hy_filt_w2[l], row(hy_filt_b2[l]), row(hy_filt_fr2[l]),
                                        hy_filt_w3[l], w_in[l], d // 2)
        proj = _in_proj(x2d, row(mix_pre_g[l]), w_in_a, w_in_b)
        proj3 = proj.reshape(bsz, seq, -1)
        cf_a_col, cf_b_col = 3, 4
        ga_col, gb_col = 5, 5 + d // hw

        hy = _hy_conv3(proj3, hy_conv_w[l], row(hy_conv_b[l]), 3 * hw)
        hy4 = hy.reshape(bsz, 2, seq // 2, 3 * hw)
        tt = _filt_dft(ce, co, sef, sof, s, dm)
        fwd, inv = (ce, co, sef, sof), (ce, seg, cog, sog)
        y1 = _hy_fwd(fwd, hy, 0, tt, tmid, 0, hw)
        z4 = _hy_inv(inv, y1, hy4, 1, hy4, 0, row(hy_bias[l, 0]), hw, natural=False)
        y2 = _hy_fwd(fwd, z4.reshape(bsz, seq, hw), 0, tt, tmid, 1, hw)
        y_a = _hy_inv(inv, y2, hy4, 2, z4, 0, row(hy_bias[l, 1]), hw, natural=True)

        y_b, (hy_proj_bf, cf_proj_bf, w_out_bf, w_gu_bf, w_down_bf) = _cf_conv(
            proj3, cf_a_col, cf_b_col, cf_dw_w[l], row(cf_dw_b[l]), row(cf_ln_g[l]), row(cf_ln_b[l]), cw,
            [hy_proj[l], cf_proj[l], w_out[l], ffn_w_gu[l], ffn_w_down[l]])

        x1 = _merge(y_a.reshape(bsz * seq, hw), y_b.reshape(bsz * seq, cw), proj, ga_col, gb_col, x2d,
                    hy_proj_bf, cf_proj_bf, w_out_bf, row(mix_post_g[l]))
        x2d = _ffn(x1, row(ffn_pre_g[l]), w_gu_bf, w_down_bf, row(ffn_post_g[l]))
    return x2d.reshape(bsz, seq, d)
```

```python
import functools
import math

import jax
import jax.numpy as jnp
from jax import lax
from jax.experimental import pallas as pl
from jax.experimental.pallas import tpu as pltpu

F32 = jnp.float32
BF16 = jnp.bfloat16
EPS = 1e-6

HY_ORDER = 2
HY_EMB_BANDS = 16
HY_DECAY_TARGET = 1e-2
HY_MIN_DECAY = math.log(HY_DECAY_TARGET) / 1.5
HY_MAX_DECAY = math.log(HY_DECAY_TARGET) / 0.3
HY_SHORT = 3
CF_KERNEL = 31
CF_PAD = 16
LANES = 128

V7X_VMEM_LIMIT_BYTES = 56 * 1024 * 1024


def _params(*sem):
    return pltpu.CompilerParams(dimension_semantics=sem, vmem_limit_bytes=V7X_VMEM_LIMIT_BYTES)


def _dot(a, b):
    return jnp.dot(a, b, preferred_element_type=F32)


def _sigmoid(x):
    return 1.0 / (1.0 + jnp.exp(-x))


def _resident(shape):
    zeros = (0,) * len(shape)
    return pl.BlockSpec(shape, lambda *_: zeros, pipeline_mode=pl.Buffered(1))


def _parity_split_store(val, scr_ref, dst_ref):
    rows, cols = val.shape
    h = rows // 2
    for sl in range(cols // LANES):
        scr_ref[sl] = val[:, sl * LANES:(sl + 1) * LANES]
    for sl in range(cols // LANES):
        c = slice(sl * LANES, (sl + 1) * LANES)
        dst_ref[0:h, c] = scr_ref[sl, pl.ds(0, h, stride=2), :].astype(dst_ref.dtype)
        dst_ref[h:rows, c] = scr_ref[sl, pl.ds(1, h, stride=2), :].astype(dst_ref.dtype)


def _dft_kernel(ce_ref, co_ref, sef_ref, sof_ref, seg_ref, cog_ref, sog_ref,
                rc_ref, rs_ref, pc_ref, ps_ref, *, tm, h, chunk):
    i = pl.program_id(0)
    mask = 4 * h - 1
    scale = 2.0 * math.pi / (4 * h)

    @pl.when(i == 0)
    def _():
        def base(j, carry):
            r0 = pl.multiple_of(j * chunk, chunk)
            r = r0 + lax.broadcasted_iota(jnp.int32, (chunk, h), 0)
            c = lax.broadcasted_iota(jnp.int32, (chunk, h), 1)
            th_e = ((2 * r * c) & mask).astype(F32) * scale
            th_o = ((r * (2 * c + 1)) & mask).astype(F32) * scale
            rc_ref[pl.ds(r0, chunk), :] = jnp.cos(th_e)
            rs_ref[pl.ds(r0, chunk), :] = jnp.sin(th_e)
            pc_ref[pl.ds(r0, chunk), :] = jnp.cos(th_o)
            ps_ref[pl.ds(r0, chunk), :] = jnp.sin(th_o)
            return carry
        lax.fori_loop(0, tm // chunk, base, 0)

    k0 = i * tm
    c1 = lax.broadcasted_iota(jnp.int32, (1, h), 1)
    th = ((2 * k0 * c1) & mask).astype(F32) * scale
    ce0, se0 = jnp.cos(th), jnp.sin(th)
    th = ((k0 * (2 * c1 + 1)) & mask).astype(F32) * scale
    co0, so0 = jnp.cos(th), jnp.sin(th)
    th = (((2 * k0 + 1) * c1) & mask).astype(F32) * scale
    cg0, sg0 = jnp.cos(th), jnp.sin(th)

    def tile(j, carry):
        r0 = pl.multiple_of(j * chunk, chunk)
        rows = pl.ds(r0, chunk)
        rc, rs, pc, ps = rc_ref[rows, :], rs_ref[rows, :], pc_ref[rows, :], ps_ref[rows, :]
        row = k0 + r0 + lax.broadcasted_iota(jnp.int32, (chunk, h), 0)
        col = lax.broadcasted_iota(jnp.int32, (chunk, h), 1)
        alt_col = (1 - 2 * (col & 1)).astype(F32)
        alt_row = (1 - 2 * (row & 1)).astype(F32)
        se = rs * ce0 + rc * se0
        ce_ref[rows, :] = (rc * ce0 - rs * se0).astype(BF16)
        co_ref[rows, :] = (pc * co0 - ps * so0).astype(BF16)
        sef_ref[rows, :] = jnp.where(row == 0, alt_col, se).astype(BF16)
        sof_ref[rows, :] = jnp.where(row == 0, alt_col, ps * co0 + pc * so0).astype(BF16)
        seg_ref[rows, :] = jnp.where(col == 0, alt_row, se).astype(BF16)
        cog_ref[rows, :] = (rc * cg0 - rs * sg0).astype(BF16)
        sog_ref[rows, :] = jnp.where(col == 0, alt_row, rs * cg0 + rc * sg0).astype(BF16)
        return carry
    lax.fori_loop(0, tm // chunk, tile, 0)


def _dft_tables(seq, tm=256, chunk=32):
    h = seq // 2
    out = jax.ShapeDtypeStruct((h, h), BF16)
    spec = pl.BlockSpec((tm, h), lambda i: (i, 0))
    return pl.pallas_call(
        functools.partial(_dft_kernel, tm=tm, h=h, chunk=chunk),
        grid=(h // tm,),
        out_specs=[spec] * 7,
        out_shape=[out] * 7,
        scratch_shapes=[pltpu.VMEM((tm, h), F32)] * 4,
        compiler_params=_params("arbitrary"),
        name="dft_tables",
    )()


def _in_proj_kernel(x_ref, g_ref, wa_ref, wb_ref, o_ref, h_ref):
    @pl.when(pl.program_id(1) == 0)
    def _():
        x = x_ref[...]
        ms = jnp.mean(x * x, axis=-1, keepdims=True)
        h_ref[...] = (x * lax.rsqrt(ms + EPS) * g_ref[...]).astype(BF16)

    ka = wa_ref.shape[0]
    acc = _dot(h_ref[:, 0:ka], wa_ref[...]) + _dot(h_ref[:, ka:], wb_ref[...])
    o_ref[...] = acc.astype(o_ref.dtype)


def _in_proj(x2d, g, wa, wb, tm=1024, tn=1536):
    m, d = x2d.shape
    n = wa.shape[1]
    assert wa.shape[0] + wb.shape[0] == d
    return pl.pallas_call(
        _in_proj_kernel,
        grid=(m // tm, n // tn),
        in_specs=[pl.BlockSpec((tm, d), lambda i, j: (i, 0)),
                  pl.BlockSpec((1, d), lambda i, j: (0, 0)),
                  pl.BlockSpec((wa.shape[0], tn), lambda i, j: (0, j)),
                  pl.BlockSpec((wb.shape[0], tn), lambda i, j: (0, j))],
        out_specs=pl.BlockSpec((tm, tn), lambda i, j: (i, j)),
        out_shape=jax.ShapeDtypeStruct((m, n), BF16),
        scratch_shapes=[pltpu.VMEM((tm, d), BF16)],
        compiler_params=_params("parallel", "arbitrary"),
        name="in_proj",
    )(x2d, g, wa, wb)


def _conv3_kernel(p_ref, w_ref, b_ref, o_ref, scr_ref):
    rows, cols = p_ref.shape[1], p_ref.shape[2]
    h = rows // 2
    pad = 8
    zeros = jnp.zeros((pad, LANES), F32)
    for sl in range(cols // LANES):
        c = slice(sl * LANES, (sl + 1) * LANES)
        scr_ref[sl, 0:pad, :] = zeros
        scr_ref[sl, pad:pad + rows, :] = p_ref[0, :, c].astype(F32)
        scr_ref[sl, pad + rows:, :] = zeros
    for sl in range(cols // LANES):
        c = slice(sl * LANES, (sl + 1) * LANES)
        w0, w1, w2, b = w_ref[0:1, c], w_ref[1:2, c], w_ref[2:3, c], b_ref[:, c]
        xom = scr_ref[sl, pl.ds(pad - 1, h, stride=2), :]
        xe = scr_ref[sl, pl.ds(pad, h, stride=2), :]
        xo = scr_ref[sl, pl.ds(pad + 1, h, stride=2), :]
        xep = scr_ref[sl, pl.ds(pad + 2, h, stride=2), :]
        o_ref[0, 0:h, c] = (w0 * xom + w1 * xe + w2 * xo + b).astype(o_ref.dtype)
        o_ref[0, h:rows, c] = (w0 * xe + w1 * xo + w2 * xep + b).astype(o_ref.dtype)


def _hy_conv3(proj3, w, b, width, ct=1024):
    bsz, seq, _ = proj3.shape
    return pl.pallas_call(
        _conv3_kernel,
        grid=(bsz, width // ct),
        in_specs=[pl.BlockSpec((1, seq, ct), lambda i, j: (i, 0, j)),
                  pl.BlockSpec((HY_SHORT, ct), lambda i, j: (0, j)),
                  pl.BlockSpec((1, ct), lambda i, j: (0, j))],
        out_specs=pl.BlockSpec((1, seq, ct), lambda i, j: (i, 0, j)),
        out_shape=jax.ShapeDtypeStruct((bsz, seq, width), BF16),
        scratch_shapes=[pltpu.VMEM((ct // LANES, seq + 16, LANES), F32)],
        compiler_params=_params("parallel", "parallel"),
        name="hy_conv3",
    )(proj3, w, b)


def _filt_mlp_kernel(w1t_ref, w1c_ref, w1s_ref, b1_ref, fr1_ref, w2_ref, b2_ref, fr2_ref,
                     w3f_ref, w3b_ref, w_ref, s_ref, d_ref, tmid_ref, wbf_ref, h2_ref, scr_ref,
                     *, seq, width, ct):
    jc = pl.program_id(1)
    wbf_ref[...] = w_ref[...].astype(BF16)
    n = lax.broadcasted_iota(jnp.int32, (seq, 1), 0).astype(F32)
    t = n / (seq - 1)

    @pl.when((pl.program_id(0) == 0) & (jc == 0))
    def _():
        nl = lax.broadcasted_iota(jnp.int32, (1, seq), 1).astype(F32)
        tl = nl / (seq - 1)
        wang = 2.0 * math.pi * nl / seq
        band = lax.broadcasted_iota(jnp.int32, (HY_EMB_BANDS, 1), 0).astype(F32)
        f = 1e-4 + band * ((HY_EMB_BANDS - 1 - 1e-4) / (HY_EMB_BANDS - 1))
        fw = f * wang
        pre1 = (w1t_ref[...] * tl + _dot(w1c_ref[...], jnp.cos(fw)) - _dot(w1s_ref[...], jnp.sin(fw))
                + b1_ref[...])
        h1 = jnp.sin(fr1_ref[...] * pre1)
        h2t = jnp.sin(fr2_ref[...] * (_dot(w2_ref[...], h1) + b2_ref[...]))
        h2_ref[...] = h2t.T

    h2 = h2_ref[...]
    ch = (jc * ct + lax.broadcasted_iota(jnp.int32, (1, ct), 1)).astype(F32)
    delta = HY_MIN_DECAY + ch * ((HY_MAX_DECAY - HY_MIN_DECAY) / (width - 1))
    decay = jnp.exp(-t * jnp.abs(delta))
    kf = _dot(h2, w3f_ref[...]) * decay
    kb = _dot(h2, w3b_ref[...]) * decay
    row = lax.broadcasted_iota(jnp.int32, (seq, ct), 0)
    kb = jnp.where(row == 0, 0.0, kb)
    l1 = jnp.sum(jnp.abs(kf), axis=0, keepdims=True) + jnp.sum(jnp.abs(kb), axis=0, keepdims=True)
    inv = 1.0 / l1
    s = (kf + kb) * inv
    dm = (kf - kb) * inv
    phase = row & 3
    cmid = jnp.where(phase == 0, 1.0, jnp.where(phase == 2, -1.0, 0.0))
    smid = jnp.where(phase == 1, 1.0, jnp.where(phase == 3, -1.0, 0.0))
    wmid = 2.0 / (2 * seq)
    tmid_ref[0:1, :] = jnp.sum(s * cmid, axis=0, keepdims=True) * wmid
    tmid_ref[1:2, :] = jnp.sum(dm * smid, axis=0, keepdims=True) * (-wmid)
    _parity_split_store(s, scr_ref, s_ref)
    _parity_split_store(dm, scr_ref, d_ref)


def _filt_mlp(seq, width, w1, b1, fr1, w2, b2, fr2, w3, w, w_row0, ct=256):
    hid = w2.shape[0]
    nct = width // ct
    steps = HY_ORDER * nct
    w_rows = w.shape[0] - w_row0
    wr = w_rows // steps
    assert w_row0 % wr == 0
    small = lambda shape: pl.BlockSpec(shape, lambda o, j: (0, 0))
    out_spec = pl.BlockSpec((seq, ct), lambda o, j: (0, o * nct + j))
    return pl.pallas_call(
        functools.partial(_filt_mlp_kernel, seq=seq, width=width, ct=ct),
        grid=(HY_ORDER, nct),
        in_specs=[small((hid, 1)), small((hid, HY_EMB_BANDS)), small((hid, HY_EMB_BANDS)),
                  small((hid, 1)), small((hid, 1)), small((hid, hid)), small((hid, 1)), small((hid, 1)),
                  pl.BlockSpec((hid, ct), lambda o, j: (0, o * 2 * nct + j)),
                  pl.BlockSpec((hid, ct), lambda o, j: (0, o * 2 * nct + nct + j)),
                  pl.BlockSpec((wr, w.shape[1]), lambda o, j: (w_row0 // wr + o * nct + j, 0))],
        out_specs=[out_spec, out_spec, pl.BlockSpec((2, ct), lambda o, j: (0, o * nct + j)),
                   pl.BlockSpec((wr, w.shape[1]), lambda o, j: (o * nct + j, 0))],
        out_shape=[jax.ShapeDtypeStruct((seq, HY_ORDER * width), BF16),
                   jax.ShapeDtypeStruct((seq, HY_ORDER * width), BF16),
                   jax.ShapeDtypeStruct((2, HY_ORDER * width), F32),
                   jax.ShapeDtypeStruct((w_rows, w.shape[1]), BF16)],
        scratch_shapes=[pltpu.VMEM((seq, hid), F32), pltpu.VMEM((ct // LANES, seq, LANES), F32)],
        compiler_params=_params("arbitrary", "arbitrary"),
        name="filt_mlp",
    )(w1[0:1].T, w1[1:1 + HY_EMB_BANDS].T, w1[1 + HY_EMB_BANDS:].T, b1.T, fr1.T, w2.T, b2.T, fr2.T, w3, w3, w)


def _filt_dft_kernel(ce_ref, co_ref, sef_ref, sof_ref, s_ref, d_ref, w_ref, t_ref, wbf_ref, *, tm, seq):
    h = seq // 2
    wbf_ref[...] = w_ref[...].astype(BF16)
    rows = pl.ds(pl.multiple_of(pl.program_id(1) * tm, tm), tm)
    pc = _dot(ce_ref[rows, :], s_ref[0:h, :])
    qc = _dot(co_ref[rows, :], s_ref[h:seq, :])
    ps = _dot(sef_ref[rows, :], d_ref[0:h, :])
    qs = _dot(sof_ref[rows, :], d_ref[h:seq, :])
    row = pl.program_id(1) * tm + lax.broadcasted_iota(jnp.int32, pc.shape, 0)
    wk = jnp.where(row == 0, 1.0, 2.0) * (1.0 / (2 * seq))
    t_ref[0] = ((pc + qc) * wk).astype(BF16)
    t_ref[1] = jnp.where(row == 0, 0.0, -(ps + qs) * wk).astype(BF16)
    t_ref[2] = ((pc - qc) * wk).astype(BF16)
    t_ref[3] = jnp.where(row == 0, 0.0, (ps - qs) * wk).astype(BF16)


def _filt_dft(ce, co, sef, sof, s, d, w, w_rows, tm=512, tn=512):
    h = ce.shape[0]
    seq, cols = s.shape
    ni = h // tm
    wr = w_rows // (cols // tn * ni)
    tab = pl.BlockSpec((seq, tn), lambda j, i: (0, j))
    wspec = pl.BlockSpec((wr, w.shape[1]), lambda j, i: (j * ni + i, 0))
    return pl.pallas_call(
        functools.partial(_filt_dft_kernel, tm=tm, seq=seq),
        grid=(cols // tn, ni),
        in_specs=[_resident((h, h))] * 4 + [tab, tab, wspec],
        out_specs=[pl.BlockSpec((4, tm, tn), lambda j, i: (0, i, j)), wspec],
        out_shape=[jax.ShapeDtypeStruct((4, h, cols), BF16),
                   jax.ShapeDtypeStruct((w_rows, w.shape[1]), BF16)],
        compiler_params=_params("arbitrary", "arbitrary"),
        name="filt_dft",
    )(ce, co, sef, sof, s, d, w)


def _hy_fwd_kernel(ce_ref, co_ref, sef_ref, sof_ref, u_ref, t_ref, tmid_ref, y_ref, *, tm):
    h = ce_ref.shape[0]
    i = pl.program_id(1)
    rows = pl.ds(pl.multiple_of(i * tm, tm), tm)
    ue = u_ref[0, 0:h, :]
    uo = u_ref[0, h:2 * h, :]
    pc = _dot(ce_ref[rows, :], ue)
    qc = _dot(co_ref[rows, :], uo)
    ps = _dot(sef_ref[rows, :], ue)
    qs = _dot(sof_ref[rows, :], uo)
    r0 = (i * tm + lax.broadcasted_iota(jnp.int32, pc.shape, 0)) == 0
    tr1, ti1 = t_ref[0, rows, :].astype(F32), t_ref[1, rows, :].astype(F32)
    tr2, ti2 = t_ref[2, rows, :].astype(F32), t_ref[3, rows, :].astype(F32)
    a1, a2 = pc + qc, pc - qc
    b1, b2 = ps + qs, qs - ps
    yr1, yi1 = a1 * tr1 + b1 * ti1, b1 * tr1 - a1 * ti1
    yr2, yi2 = a2 * tr2 + b2 * ti2, b2 * tr2 - a2 * ti2
    trh, tih = tmid_ref[0:1, :], tmid_ref[1:2, :]
    y_ref[0, 0] = (yr1 + yr2).astype(BF16)
    y_ref[0, 1] = (yr1 - yr2).astype(BF16)
    y_ref[0, 2] = jnp.where(r0, ps * trh + qs * tih, yi1 - yi2).astype(BF16)
    y_ref[0, 3] = jnp.where(r0, qs * trh - ps * tih, yi1 + yi2).astype(BF16)


def _hy_fwd(mats, u3, u_col, tt, tmid, t_col, width, tm=512):
    bsz, seq, _ = u3.shape
    h = seq // 2
    return pl.pallas_call(
        functools.partial(_hy_fwd_kernel, tm=tm),
        grid=(bsz, h // tm),
        in_specs=[_resident((h, h))] * 4
                 + [pl.BlockSpec((1, seq, width), lambda b, i: (b, 0, u_col)),
                    pl.BlockSpec((4, h, width), lambda b, i: (0, 0, t_col), pipeline_mode=pl.Buffered(1)),
                    pl.BlockSpec((2, width), lambda b, i: (0, t_col))],
        out_specs=pl.BlockSpec((1, 4, tm, width), lambda b, i: (b, 0, i, 0)),
        out_shape=jax.ShapeDtypeStruct((bsz, 4, h, width), BF16),
        compiler_params=_params("arbitrary", "arbitrary"),
        name="hy_fwd",
    )(*mats, u3, tt, tmid)


def _hy_inv_kernel(ce_ref, seg_ref, cog_ref, sog_ref, y_ref, g_ref, u_ref, bias_ref, o_ref, *scr, tm, natural):
    i = pl.program_id(1)
    rows = pl.ds(pl.multiple_of(i * tm, tm), tm)
    ye = _dot(ce_ref[rows, :], y_ref[0, 0]) + _dot(seg_ref[rows, :], y_ref[0, 2])
    yo = _dot(cog_ref[rows, :], y_ref[0, 1]) + _dot(sog_ref[rows, :], y_ref[0, 3])
    bias = bias_ref[...]
    ze = g_ref[0, 0].astype(F32) * (ye + u_ref[0, 0].astype(F32) * bias)
    zo = g_ref[0, 1].astype(F32) * (yo + u_ref[0, 1].astype(F32) * bias)
    if natural:
        scr_ref, = scr
        for sl in range(ze.shape[1] // LANES):
            c = slice(sl * LANES, (sl + 1) * LANES)
            scr_ref[sl, pl.ds(0, tm, stride=2), :] = ze[:, c]
            scr_ref[sl, pl.ds(1, tm, stride=2), :] = zo[:, c]
        for sl in range(ze.shape[1] // LANES):
            o_ref[0, :, sl * LANES:(sl + 1) * LANES] = scr_ref[sl].astype(o_ref.dtype)
    else:
        o_ref[0, 0] = ze.astype(o_ref.dtype)
        o_ref[0, 1] = zo.astype(o_ref.dtype)


def _hy_inv(mats, y4, g4, g_col, u4, u_col, bias, width, natural, tm=512):
    bsz, _, h, _ = y4.shape
    par = lambda col: pl.BlockSpec((1, 2, tm, width), lambda b, i: (b, 0, i, col))
    if natural:
        out_spec = pl.BlockSpec((1, 2 * tm, width), lambda b, i: (b, i, 0))
        out_shape = jax.ShapeDtypeStruct((bsz, 2 * h, width), BF16)
        scratch = [pltpu.VMEM((width // LANES, 2 * tm, LANES), F32)]
    else:
        out_spec = par(0)
        out_shape = jax.ShapeDtypeStruct((bsz, 2, h, width), BF16)
        scratch = []
    return pl.pallas_call(
        functools.partial(_hy_inv_kernel, tm=tm, natural=natural),
        grid=(bsz, h // tm),
        in_specs=[_resident((h, h))] * 4
                 + [pl.BlockSpec((1, 4, h, width), lambda b, i: (b, 0, 0, 0)),
                    par(g_col), par(u_col),
                    pl.BlockSpec((1, width), lambda b, i: (0, 0))],
        out_specs=out_spec,
        out_shape=out_shape,
        scratch_shapes=scratch,
        compiler_params=_params("arbitrary", "arbitrary"),
        name="hy_inv",
    )(*mats, y4, g4, u4, bias)


def _cf_kernel(*refs, step_rows, n_cast):
    (am_ref, ap_ref, an_ref, bm_ref, bp_ref, bn_ref, w_ref, cb_ref, lg_ref, lb_ref) = refs[:10]
    cast_in = refs[10:10 + n_cast]
    o_ref = refs[10 + n_cast]
    cast_out = refs[11 + n_cast:11 + 2 * n_cast]
    us_ref, cs_ref, wb_ref = refs[11 + 2 * n_cast:]
    width = wb_ref.shape[-1]
    nsl = width // LANES
    j = pl.program_id(1)

    for src, dst in zip(cast_in, cast_out):
        dst[...] = src[...].astype(dst.dtype)

    @pl.when((pl.program_id(0) == 0) & (j == 0))
    def _():
        for t in range(CF_KERNEL):
            wb_ref[t] = jnp.broadcast_to(w_ref[t:t + 1, :], (8, width))

    glu = lambda a, b: a[0].astype(F32) * _sigmoid(b[0].astype(F32))
    u_prev = jnp.where(j == 0, 0.0, glu(ap_ref, bp_ref))
    u_main = glu(am_ref, bm_ref)
    u_next = jnp.where(j == pl.num_programs(1) - 1, 0.0, glu(an_ref, bn_ref))
    for sl in range(nsl):
        c = slice(sl * LANES, (sl + 1) * LANES)
        us_ref[sl, 0:CF_PAD, :] = u_prev[:, c]
        us_ref[sl, CF_PAD:CF_PAD + step_rows, :] = u_main[:, c]
        us_ref[sl, CF_PAD + step_rows:, :] = u_next[:, c]

    half = CF_KERNEL // 2
    nph = 4
    prow = step_rows // nph
    for sl in range(nsl):
        c = slice(sl * LANES, (sl + 1) * LANES)
        accs = [jnp.zeros((prow // 8, 8, LANES), F32) + cb_ref[:, c][None]] * nph
        for r in range(CF_KERNEL + nph - 1):
            tap = us_ref[sl, pl.ds(CF_PAD - half + r, prow, stride=nph), :].reshape(prow // 8, 8, LANES)
            for p in range(nph):
                if 0 <= r - p < CF_KERNEL:
                    accs[p] = accs[p] + wb_ref[r - p, :, c][None] * tap
        for p in range(nph):
            cs_ref[sl, pl.ds(p, prow, stride=nph), :] = accs[p].reshape(prow, LANES)

    tot = cs_ref[0]
    for sl in range(1, nsl):
        tot = tot + cs_ref[sl]
    mu = jnp.sum(tot, axis=-1, keepdims=True) * (1.0 / width)
    sq = None
    for sl in range(nsl):
        cen = cs_ref[sl] - mu
        sq = cen * cen if sq is None else sq + cen * cen
    rstd = lax.rsqrt(jnp.sum(sq, axis=-1, keepdims=True) * (1.0 / width) + EPS)
    for sl in range(nsl):
        c = slice(sl * LANES, (sl + 1) * LANES)
        y = (cs_ref[sl] - mu) * rstd * lg_ref[:, c] + lb_ref[:, c]
        o_ref[0, :, c] = (y * _sigmoid(y)).astype(o_ref.dtype)


def _cf_conv(proj3, a_col, b_col, w, cb, lg, lb, width, casts, step_rows=128):
    bsz, seq, _ = proj3.shape
    nsteps = seq // step_rows
    total = bsz * nsteps
    nblk = step_rows // CF_PAD
    last_blk = seq // CF_PAD - 1
    vec = pl.BlockSpec((1, width), lambda b, j: (0, 0))
    main = lambda col: pl.BlockSpec((1, step_rows, width), lambda b, j: (b, j, col))
    prev = lambda col: pl.BlockSpec((1, CF_PAD, width), lambda b, j: (b, jnp.maximum(j * nblk - 1, 0), col))
    nxt = lambda col: pl.BlockSpec((1, CF_PAD, width),
                                   lambda b, j: (b, jnp.minimum((j + 1) * nblk, last_blk), col))

    def cast_spec(c):
        n = next(n for n in (total, total // 2, total // 4) if c.shape[0] % (16 * n) == 0)
        every = total // n
        return pl.BlockSpec((c.shape[0] // n, c.shape[1]), lambda b, j: ((b * nsteps + j) // every, 0))
    cast_specs = [cast_spec(c) for c in casts]
    outs = pl.pallas_call(
        functools.partial(_cf_kernel, step_rows=step_rows, n_cast=len(casts)),
        grid=(bsz, nsteps),
        in_specs=[main(a_col), prev(a_col), nxt(a_col), main(b_col), prev(b_col), nxt(b_col),
                  pl.BlockSpec((CF_KERNEL, width), lambda b, j: (0, 0)),
                  vec, vec, vec] + cast_specs,
        out_specs=[pl.BlockSpec((1, step_rows, width), lambda b, j: (b, j, 0))] + cast_specs,
        out_shape=[jax.ShapeDtypeStruct((bsz, seq, width), BF16)]
                  + [jax.ShapeDtypeStruct(c.shape, BF16) for c in casts],
        scratch_shapes=[pltpu.VMEM((width // LANES, step_rows + 2 * CF_PAD, LANES), F32),
                        pltpu.VMEM((width // LANES, step_rows, LANES), F32),
                        pltpu.VMEM((CF_KERNEL, 8, width), F32)],
        compiler_params=_params("arbitrary", "arbitrary"),
        name="cf_conv",
    )(proj3, proj3, proj3, proj3, proj3, proj3, w, cb, lg, lb, *casts)
    return outs[0], outs[1:]


def _merge_kernel(ya_ref, yb_ref, ga0_ref, ga1_ref, gb0_ref, gb1_ref, x_ref, pa_ref, pb_ref, wo_ref,
                  g1_ref, x1_ref):
    a = _dot(ya_ref[...], pa_ref[...])
    b = _dot(yb_ref[...], pb_ref[...])
    w = ga0_ref.shape[1]
    m0 = _sigmoid(ga0_ref[...].astype(F32)) * a[:, :w] + _sigmoid(gb0_ref[...].astype(F32)) * b[:, :w]
    m1 = _sigmoid(ga1_ref[...].astype(F32)) * a[:, w:] + _sigmoid(gb1_ref[...].astype(F32)) * b[:, w:]
    o = _dot(m0.astype(BF16), wo_ref[0:w, :]) + _dot(m1.astype(BF16), wo_ref[w:2 * w, :])
    x1_ref[...] = x_ref[...] + o * lax.rsqrt(jnp.mean(o * o, axis=-1, keepdims=True) + EPS) * g1_ref[...]


def _merge(ya, yb, proj, ga_col, gb_col, x2d, pa, pb, wo, g1, tm=256):
    m, d = x2d.shape
    wa = ya.shape[1]
    assert d == 2 * wa
    const = lambda shape: pl.BlockSpec(shape, lambda i: (0, 0))
    gate = lambda col: pl.BlockSpec((tm, wa), lambda i: (i, col))
    return pl.pallas_call(
        _merge_kernel,
        grid=(m // tm,),
        in_specs=[pl.BlockSpec((tm, wa), lambda i: (i, 0)),
                  pl.BlockSpec((tm, wa), lambda i: (i, 0)),
                  gate(ga_col), gate(ga_col + 1), gate(gb_col), gate(gb_col + 1),
                  pl.BlockSpec((tm, d), lambda i: (i, 0)),
                  const((wa, d)), const((wa, d)), const((d, d)), const((1, d))],
        out_specs=pl.BlockSpec((tm, d), lambda i: (i, 0)),
        out_shape=jax.ShapeDtypeStruct((m, d), F32),
        compiler_params=_params("parallel"),
        name="merge",
    )(ya, yb, proj, proj, proj, proj, x2d, pa, pb, wo, g1)


def _ffn_kernel(x1_ref, gpre_ref, wg_ref, wu_ref, wd_ref, gpost_ref, o_ref, h_ref):
    j = pl.program_id(1)

    @pl.when(j == 0)
    def _():
        x1 = x1_ref[...]
        ms = jnp.mean(x1 * x1, axis=-1, keepdims=True)
        h_ref[...] = (x1 * lax.rsqrt(ms + EPS) * gpre_ref[...]).astype(BF16)
        o_ref[...] = jnp.zeros_like(o_ref)

    h = h_ref[...]
    gate = _dot(h, wg_ref[...])
    up = _dot(h, wu_ref[...])
    act = (gate * _sigmoid(gate) * up).astype(BF16)
    o_ref[...] += _dot(act, wd_ref[...])

    @pl.when(j == pl.num_programs(1) - 1)
    def _():
        a = o_ref[...]
        o_ref[...] = x1_ref[...] + a * lax.rsqrt(jnp.mean(a * a, axis=-1, keepdims=True) + EPS) * gpost_ref[...]


def _ffn(x1, gpre, wgu, wd, gpost, tm=1024, th=256):
    m, d = x1.shape
    hidden = wd.shape[0]
    nh = hidden // th
    return pl.pallas_call(
        _ffn_kernel,
        grid=(m // tm, nh),
        in_specs=[pl.BlockSpec((tm, d), lambda i, j: (i, 0)),
                  pl.BlockSpec((1, d), lambda i, j: (0, 0)),
                  pl.BlockSpec((d, th), lambda i, j: (0, j)),
                  pl.BlockSpec((d, th), lambda i, j: (0, nh + j)),
                  pl.BlockSpec((th, d), lambda i, j: (j, 0)),
                  pl.BlockSpec((1, d), lambda i, j: (0, 0))],
        out_specs=pl.BlockSpec((tm, d), lambda i, j: (i, 0)),
        out_shape=jax.ShapeDtypeStruct((m, d), F32),
        scratch_shapes=[pltpu.VMEM((tm, d), BF16)],
        compiler_params=_params("parallel", "arbitrary"),
        name="ffn",
    )(x1, gpre, wgu, wgu, wd, gpost)


def kernel(x, mix_pre_g, w_in, hy_conv_w, hy_conv_b, hy_filt_w1, hy_filt_b1, hy_filt_fr1, hy_filt_w2,
           hy_filt_b2, hy_filt_fr2, hy_filt_w3, hy_bias, hy_proj, cf_dw_w, cf_dw_b, cf_ln_g, cf_ln_b,
           cf_proj, w_out, mix_post_g, ffn_pre_g, ffn_w_gu, ffn_w_down, ffn_post_g):
    bsz, seq, d = x.shape
    depth = w_in.shape[0]
    hw = hy_proj.shape[1]
    cw = cf_proj.shape[1]
    assert hw == cw and d % hw == 0
    row = lambda v: v.reshape(1, -1)

    assert depth == 1
    x2d = x.reshape(bsz * seq, d)
    for l in range(depth):
        ce, co, sef, sof, seg, cog, sog = _dft_tables(seq)
        s, dm, tmid, w_in_b = _filt_mlp(seq, hw, hy_filt_w1[l], row(hy_filt_b1[l]), row(hy_filt_fr1[l]),
                                        hy_filt_w2[l], row(hy_filt_b2[l]), row(hy_filt_fr2[l]),
                                        hy_filt_w3[l], w_in[l], d // 2)
        tt, w_in_a = _filt_dft(ce, co, sef, sof, s, dm, w_in[l], d // 2)
        proj = _in_proj(x2d, row(mix_pre_g[l]), w_in_a, w_in_b)
        proj3 = proj.reshape(bsz, seq, -1)
        cf_a_col, cf_b_col = 3, 4
        ga_col, gb_col = 5, 5 + d // hw

        hy = _hy_conv3(proj3, hy_conv_w[l], row(hy_conv_b[l]), 3 * hw)
        hy4 = hy.reshape(bsz, 2, seq // 2, 3 * hw)
        fwd, inv = (ce, co, sef, sof), (ce, seg, cog, sog)
        y1 = _hy_fwd(fwd, hy, 0, tt, tmid, 0, hw)
        z4 = _hy_inv(inv, y1, hy4, 1, hy4, 0, row(hy_bias[l, 0]), hw, natural=False)
        y2 = _hy_fwd(fwd, z4.reshape(bsz, seq, hw), 0, tt, tmid, 1, hw)
        y_a = _hy_inv(inv, y2, hy4, 2, z4, 0, row(hy_bias[l, 1]), hw, natural=True)

        y_b, (hy_proj_bf, cf_proj_bf, w_out_bf, w_gu_bf, w_down_bf) = _cf_conv(
            proj3, cf_a_col, cf_b_col, cf_dw_w[l], row(cf_dw_b[l]), row(cf_ln_g[l]), row(cf_ln_b[l]), cw,
            [hy_proj[l], cf_proj[l], w_out[l], ffn_w_gu[l], ffn_w_down[l]])

        x1 = _merge(y_a.reshape(bsz * seq, hw), y_b.reshape(bsz * seq, cw), proj, ga_col, gb_col, x2d,
                    hy_proj_bf, cf_proj_bf, w_out_bf, row(mix_post_g[l]))
        x2d = _ffn(x1, row(ffn_pre_g[l]), w_gu_bf, w_down_bf, row(ffn_post_g[l]))
    return x2d.reshape(bsz, seq, d)
```

```python
import functools
import math

import jax
import jax.numpy as jnp
from jax import lax
from jax.experimental import pallas as pl
from jax.experimental.pallas import tpu as pltpu

F32 = jnp.float32
BF16 = jnp.bfloat16
EPS = 1e-6

HY_ORDER = 2
HY_EMB_BANDS = 16
HY_DECAY_TARGET = 1e-2
HY_MIN_DECAY = math.log(HY_DECAY_TARGET) / 1.5
HY_MAX_DECAY = math.log(HY_DECAY_TARGET) / 0.3
HY_SHORT = 3
CF_KERNEL = 31
CF_PAD = 16
LANES = 128

V7X_VMEM_LIMIT_BYTES = 56 * 1024 * 1024


def _params(*sem):
    return pltpu.CompilerParams(dimension_semantics=sem, vmem_limit_bytes=V7X_VMEM_LIMIT_BYTES)


def _dot(a, b):
    return jnp.dot(a, b, preferred_element_type=F32)


def _sigmoid(x):
    return 1.0 / (1.0 + jnp.exp(-x))


def _resident(shape):
    zeros = (0,) * len(shape)
    return pl.BlockSpec(shape, lambda *_: zeros, pipeline_mode=pl.Buffered(1))


def _parity_split_store(val, scr_ref, dst_ref):
    rows, cols = val.shape
    h = rows // 2
    for sl in range(cols // LANES):
        scr_ref[sl] = val[:, sl * LANES:(sl + 1) * LANES]
    for sl in range(cols // LANES):
        c = slice(sl * LANES, (sl + 1) * LANES)
        dst_ref[0:h, c] = scr_ref[sl, pl.ds(0, h, stride=2), :].astype(dst_ref.dtype)
        dst_ref[h:rows, c] = scr_ref[sl, pl.ds(1, h, stride=2), :].astype(dst_ref.dtype)


def _dft_kernel(ce_ref, co_ref, sef_ref, sof_ref, seg_ref, cog_ref, sog_ref,
                rc_ref, rs_ref, pc_ref, ps_ref, *, tm, h, chunk):
    i = pl.program_id(0)
    mask = 4 * h - 1
    scale = 2.0 * math.pi / (4 * h)

    @pl.when(i == 0)
    def _():
        def base(j, carry):
            r0 = pl.multiple_of(j * chunk, chunk)
            r = r0 + lax.broadcasted_iota(jnp.int32, (chunk, h), 0)
            c = lax.broadcasted_iota(jnp.int32, (chunk, h), 1)
            th_e = ((2 * r * c) & mask).astype(F32) * scale
            th_o = ((r * (2 * c + 1)) & mask).astype(F32) * scale
            rc_ref[pl.ds(r0, chunk), :] = jnp.cos(th_e)
            rs_ref[pl.ds(r0, chunk), :] = jnp.sin(th_e)
            pc_ref[pl.ds(r0, chunk), :] = jnp.cos(th_o)
            ps_ref[pl.ds(r0, chunk), :] = jnp.sin(th_o)
            return carry
        lax.fori_loop(0, tm // chunk, base, 0)

    k0 = i * tm
    c1 = lax.broadcasted_iota(jnp.int32, (1, h), 1)
    th = ((2 * k0 * c1) & mask).astype(F32) * scale
    ce0, se0 = jnp.cos(th), jnp.sin(th)
    th = ((k0 * (2 * c1 + 1)) & mask).astype(F32) * scale
    co0, so0 = jnp.cos(th), jnp.sin(th)
    th = (((2 * k0 + 1) * c1) & mask).astype(F32) * scale
    cg0, sg0 = jnp.cos(th), jnp.sin(th)

    def tile(j, carry):
        r0 = pl.multiple_of(j * chunk, chunk)
        rows = pl.ds(r0, chunk)
        rc, rs, pc, ps = rc_ref[rows, :], rs_ref[rows, :], pc_ref[rows, :], ps_ref[rows, :]
        row = k0 + r0 + lax.broadcasted_iota(jnp.int32, (chunk, h), 0)
        col = lax.broadcasted_iota(jnp.int32, (chunk, h), 1)
        alt_col = (1 - 2 * (col & 1)).astype(F32)
        alt_row = (1 - 2 * (row & 1)).astype(F32)
        se = rs * ce0 + rc * se0
        ce_ref[rows, :] = (rc * ce0 - rs * se0).astype(BF16)
        co_ref[rows, :] = (pc * co0 - ps * so0).astype(BF16)
        sef_ref[rows, :] = jnp.where(row == 0, alt_col, se).astype(BF16)
        sof_ref[rows, :] = jnp.where(row == 0, alt_col, ps * co0 + pc * so0).astype(BF16)
        seg_ref[rows, :] = jnp.where(col == 0, alt_row, se).astype(BF16)
        cog_ref[rows, :] = (rc * cg0 - rs * sg0).astype(BF16)
        sog_ref[rows, :] = jnp.where(col == 0, alt_row, rs * cg0 + rc * sg0).astype(BF16)
        return carry
    lax.fori_loop(0, tm // chunk, tile, 0)


def _dft_tables(seq, tm=256, chunk=32):
    h = seq // 2
    out = jax.ShapeDtypeStruct((h, h), BF16)
    spec = pl.BlockSpec((tm, h), lambda i: (i, 0))
    return pl.pallas_call(
        functools.partial(_dft_kernel, tm=tm, h=h, chunk=chunk),
        grid=(h // tm,),
        out_specs=[spec] * 7,
        out_shape=[out] * 7,
        scratch_shapes=[pltpu.VMEM((tm, h), F32)] * 4,
        compiler_params=_params("arbitrary"),
        name="dft_tables",
    )()


def _in_proj_kernel(x_ref, g_ref, wa_ref, wb_ref, o_ref, h_ref):
    @pl.when(pl.program_id(1) == 0)
    def _():
        x = x_ref[...]
        ms = jnp.mean(x * x, axis=-1, keepdims=True)
        h_ref[...] = (x * lax.rsqrt(ms + EPS) * g_ref[...]).astype(BF16)

    ka = wa_ref.shape[0]
    acc = _dot(h_ref[:, 0:ka], wa_ref[...]) + _dot(h_ref[:, ka:], wb_ref[...])
    o_ref[...] = acc.astype(o_ref.dtype)


def _in_proj(x2d, g, wa, wb, tm=1024, tn=1536):
    m, d = x2d.shape
    n = wa.shape[1]
    assert wa.shape[0] + wb.shape[0] == d
    return pl.pallas_call(
        _in_proj_kernel,
        grid=(m // tm, n // tn),
        in_specs=[pl.BlockSpec((tm, d), lambda i, j: (i, 0)),
                  pl.BlockSpec((1, d), lambda i, j: (0, 0)),
                  pl.BlockSpec((wa.shape[0], tn), lambda i, j: (0, j)),
                  pl.BlockSpec((wb.shape[0], tn), lambda i, j: (0, j))],
        out_specs=pl.BlockSpec((tm, tn), lambda i, j: (i, j)),
        out_shape=jax.ShapeDtypeStruct((m, n), BF16),
        scratch_shapes=[pltpu.VMEM((tm, d), BF16)],
        compiler_params=_params("parallel", "arbitrary"),
        name="in_proj",
    )(x2d, g, wa, wb)


def _conv3_kernel(p_ref, w_ref, b_ref, o_ref, scr_ref):
    rows, cols = p_ref.shape[1], p_ref.shape[2]
    h = rows // 2
    pad = 8
    zeros = jnp.zeros((pad, LANES), F32)
    for sl in range(cols // LANES):
        c = slice(sl * LANES, (sl + 1) * LANES)
        scr_ref[sl, 0:pad, :] = zeros
        scr_ref[sl, pad:pad + rows, :] = p_ref[0, :, c].astype(F32)
        scr_ref[sl, pad + rows:, :] = zeros
    for sl in range(cols // LANES):
        c = slice(sl * LANES, (sl + 1) * LANES)
        w0, w1, w2, b = w_ref[0:1, c], w_ref[1:2, c], w_ref[2:3, c], b_ref[:, c]
        xom = scr_ref[sl, pl.ds(pad - 1, h, stride=2), :]
        xe = scr_ref[sl, pl.ds(pad, h, stride=2), :]
        xo = scr_ref[sl, pl.ds(pad + 1, h, stride=2), :]
        xep = scr_ref[sl, pl.ds(pad + 2, h, stride=2), :]
        o_ref[0, 0:h, c] = (w0 * xom + w1 * xe + w2 * xo + b).astype(o_ref.dtype)
        o_ref[0, h:rows, c] = (w0 * xe + w1 * xo + w2 * xep + b).astype(o_ref.dtype)


def _hy_conv3(proj3, w, b, width, ct=1024):
    bsz, seq, _ = proj3.shape
    return pl.pallas_call(
        _conv3_kernel,
        grid=(bsz, width // ct),
        in_specs=[pl.BlockSpec((1, seq, ct), lambda i, j: (i, 0, j)),
                  pl.BlockSpec((HY_SHORT, ct), lambda i, j: (0, j)),
                  pl.BlockSpec((1, ct), lambda i, j: (0, j))],
        out_specs=pl.BlockSpec((1, seq, ct), lambda i, j: (i, 0, j)),
        out_shape=jax.ShapeDtypeStruct((bsz, seq, width), BF16),
        scratch_shapes=[pltpu.VMEM((ct // LANES, seq + 16, LANES), F32)],
        compiler_params=_params("parallel", "parallel"),
        name="hy_conv3",
    )(proj3, w, b)


def _filt_mlp_kernel(mlp_ref, w3f_ref, w3b_ref, w_ref, s_ref, d_ref, tmid_ref, wbf_ref, h2_ref, scr_ref,
                     *, seq, width, ct):
    jc = pl.program_id(1)
    wbf_ref[...] = w_ref[...].astype(BF16)
    n = lax.broadcasted_iota(jnp.int32, (seq, 1), 0).astype(F32)
    t = n / (seq - 1)

    @pl.when((pl.program_id(0) == 0) & (jc == 0))
    def _():
        nl = lax.broadcasted_iota(jnp.int32, (1, seq), 1).astype(F32)
        tl = nl / (seq - 1)
        wang = 2.0 * math.pi * nl / seq
        band = lax.broadcasted_iota(jnp.int32, (HY_EMB_BANDS, 1), 0).astype(F32)
        f = 1e-4 + band * ((HY_EMB_BANDS - 1 - 1e-4) / (HY_EMB_BANDS - 1))
        fw = f * wang
        p = mlp_ref[...]
        hid = p.shape[0]
        nb = HY_EMB_BANDS
        w1t, w1c, w1s = p[:, 0:1], p[:, 1:1 + nb], p[:, 1 + nb:1 + 2 * nb]
        c0 = 1 + 2 * nb
        b1, fr1, b2, fr2 = p[:, c0:c0 + 1], p[:, c0 + 1:c0 + 2], p[:, c0 + 2:c0 + 3], p[:, c0 + 3:c0 + 4]
        w2t = p[:, c0 + 4:c0 + 4 + hid]
        pre1 = w1t * tl + _dot(w1c, jnp.cos(fw)) - _dot(w1s, jnp.sin(fw)) + b1
        h1 = jnp.sin(fr1 * pre1)
        h2t = jnp.sin(fr2 * (_dot(w2t, h1) + b2))
        h2_ref[...] = h2t.T

    h2 = h2_ref[...]
    ch = (jc * ct + lax.broadcasted_iota(jnp.int32, (1, ct), 1)).astype(F32)
    delta = HY_MIN_DECAY + ch * ((HY_MAX_DECAY - HY_MIN_DECAY) / (width - 1))
    decay = jnp.exp(-t * jnp.abs(delta))
    kf = _dot(h2, w3f_ref[...]) * decay
    kb = _dot(h2, w3b_ref[...]) * decay
    row = lax.broadcasted_iota(jnp.int32, (seq, ct), 0)
    kb = jnp.where(row == 0, 0.0, kb)
    l1 = jnp.sum(jnp.abs(kf), axis=0, keepdims=True) + jnp.sum(jnp.abs(kb), axis=0, keepdims=True)
    inv = 1.0 / l1
    s = (kf + kb) * inv
    dm = (kf - kb) * inv
    phase = row & 3
    cmid = jnp.where(phase == 0, 1.0, jnp.where(phase == 2, -1.0, 0.0))
    smid = jnp.where(phase == 1, 1.0, jnp.where(phase == 3, -1.0, 0.0))
    wmid = 2.0 / (2 * seq)
    tmid_ref[0:1, :] = jnp.sum(s * cmid, axis=0, keepdims=True) * wmid
    tmid_ref[1:2, :] = jnp.sum(dm * smid, axis=0, keepdims=True) * (-wmid)
    _parity_split_store(s, scr_ref, s_ref)
    _parity_split_store(dm, scr_ref, d_ref)


def _pack_filter_mlp(w1, b1, fr1, w2, b2, fr2):
    cols = [w1.T, b1.reshape(-1, 1), fr1.reshape(-1, 1), b2.reshape(-1, 1), fr2.reshape(-1, 1), w2.T]
    packed = jnp.concatenate(cols, axis=1)
    pad = -packed.shape[1] % LANES
    return jnp.pad(packed, ((0, 0), (0, pad)))


def _filt_mlp(seq, width, w1, b1, fr1, w2, b2, fr2, w3, w, w_row0, ct=256):
    hid = w2.shape[0]
    mlp = _pack_filter_mlp(w1, b1, fr1, w2, b2, fr2)
    nct = width // ct
    steps = HY_ORDER * nct
    w_rows = w.shape[0] - w_row0
    wr = w_rows // steps
    assert w_row0 % wr == 0
    small = lambda shape: pl.BlockSpec(shape, lambda o, j: (0, 0))
    out_spec = pl.BlockSpec((seq, ct), lambda o, j: (0, o * nct + j))
    return pl.pallas_call(
        functools.partial(_filt_mlp_kernel, seq=seq, width=width, ct=ct),
        grid=(HY_ORDER, nct),
        in_specs=[small(mlp.shape),
                  pl.BlockSpec((hid, ct), lambda o, j: (0, o * 2 * nct + j)),
                  pl.BlockSpec((hid, ct), lambda o, j: (0, o * 2 * nct + nct + j)),
                  pl.BlockSpec((wr, w.shape[1]), lambda o, j: (w_row0 // wr + o * nct + j, 0))],
        out_specs=[out_spec, out_spec, pl.BlockSpec((2, ct), lambda o, j: (0, o * nct + j)),
                   pl.BlockSpec((wr, w.shape[1]), lambda o, j: (o * nct + j, 0))],
        out_shape=[jax.ShapeDtypeStruct((seq, HY_ORDER * width), BF16),
                   jax.ShapeDtypeStruct((seq, HY_ORDER * width), BF16),
                   jax.ShapeDtypeStruct((2, HY_ORDER * width), F32),
                   jax.ShapeDtypeStruct((w_rows, w.shape[1]), BF16)],
        scratch_shapes=[pltpu.VMEM((seq, hid), F32), pltpu.VMEM((ct // LANES, seq, LANES), F32)],
        compiler_params=_params("arbitrary", "arbitrary"),
        name="filt_mlp",
    )(mlp, w3, w3, w)


def _filt_dft_kernel(ce_ref, co_ref, sef_ref, sof_ref, s_ref, d_ref, w_ref, t_ref, wbf_ref, *, tm, seq):
    h = seq // 2
    wbf_ref[...] = w_ref[...].astype(BF16)
    rows = pl.ds(pl.multiple_of(pl.program_id(1) * tm, tm), tm)
    pc = _dot(ce_ref[rows, :], s_ref[0:h, :])
    qc = _dot(co_ref[rows, :], s_ref[h:seq, :])
    ps = _dot(sef_ref[rows, :], d_ref[0:h, :])
    qs = _dot(sof_ref[rows, :], d_ref[h:seq, :])
    row = pl.program_id(1) * tm + lax.broadcasted_iota(jnp.int32, pc.shape, 0)
    wk = jnp.where(row == 0, 1.0, 2.0) * (1.0 / (2 * seq))
    t_ref[0] = ((pc + qc) * wk).astype(BF16)
    t_ref[1] = jnp.where(row == 0, 0.0, -(ps + qs) * wk).astype(BF16)
    t_ref[2] = ((pc - qc) * wk).astype(BF16)
    t_ref[3] = jnp.where(row == 0, 0.0, (ps - qs) * wk).astype(BF16)


def _filt_dft(ce, co, sef, sof, s, d, w, w_rows, tm=512, tn=512):
    h = ce.shape[0]
    seq, cols = s.shape
    ni = h // tm
    wr = w_rows // (cols // tn * ni)
    tab = pl.BlockSpec((seq, tn), lambda j, i: (0, j))
    wspec = pl.BlockSpec((wr, w.shape[1]), lambda j, i: (j * ni + i, 0))
    return pl.pallas_call(
        functools.partial(_filt_dft_kernel, tm=tm, seq=seq),
        grid=(cols // tn, ni),
        in_specs=[_resident((h, h))] * 4 + [tab, tab, wspec],
        out_specs=[pl.BlockSpec((4, tm, tn), lambda j, i: (0, i, j)), wspec],
        out_shape=[jax.ShapeDtypeStruct((4, h, cols), BF16),
                   jax.ShapeDtypeStruct((w_rows, w.shape[1]), BF16)],
        compiler_params=_params("arbitrary", "arbitrary"),
        name="filt_dft",
    )(ce, co, sef, sof, s, d, w)


def _hy_fwd_kernel(ce_ref, co_ref, sef_ref, sof_ref, u_ref, t_ref, tmid_ref, y_ref, *, tm):
    h = ce_ref.shape[0]
    i = pl.program_id(1)
    rows = pl.ds(pl.multiple_of(i * tm, tm), tm)
    ue = u_ref[0, 0:h, :]
    uo = u_ref[0, h:2 * h, :]
    pc = _dot(ce_ref[rows, :], ue)
    qc = _dot(co_ref[rows, :], uo)
    ps = _dot(sef_ref[rows, :], ue)
    qs = _dot(sof_ref[rows, :], uo)
    r0 = (i * tm + lax.broadcasted_iota(jnp.int32, pc.shape, 0)) == 0
    tr1, ti1 = t_ref[0, rows, :].astype(F32), t_ref[1, rows, :].astype(F32)
    tr2, ti2 = t_ref[2, rows, :].astype(F32), t_ref[3, rows, :].astype(F32)
    a1, a2 = pc + qc, pc - qc
    b1, b2 = ps + qs, qs - ps
    yr1, yi1 = a1 * tr1 + b1 * ti1, b1 * tr1 - a1 * ti1
    yr2, yi2 = a2 * tr2 + b2 * ti2, b2 * tr2 - a2 * ti2
    trh, tih = tmid_ref[0:1, :], tmid_ref[1:2, :]
    y_ref[0, 0] = (yr1 + yr2).astype(BF16)
    y_ref[0, 1] = (yr1 - yr2).astype(BF16)
    y_ref[0, 2] = jnp.where(r0, ps * trh + qs * tih, yi1 - yi2).astype(BF16)
    y_ref[0, 3] = jnp.where(r0, qs * trh - ps * tih, yi1 + yi2).astype(BF16)


def _hy_fwd(mats, u3, u_col, tt, tmid, t_col, width, tm=512):
    bsz, seq, _ = u3.shape
    h = seq // 2
    return pl.pallas_call(
        functools.partial(_hy_fwd_kernel, tm=tm),
        grid=(bsz, h // tm),
        in_specs=[_resident((h, h))] * 4
                 + [pl.BlockSpec((1, seq, width), lambda b, i: (b, 0, u_col)),
                    pl.BlockSpec((4, h, width), lambda b, i: (0, 0, t_col), pipeline_mode=pl.Buffered(1)),
                    pl.BlockSpec((2, width), lambda b, i: (0, t_col))],
        out_specs=pl.BlockSpec((1, 4, tm, width), lambda b, i: (b, 0, i, 0)),
        out_shape=jax.ShapeDtypeStruct((bsz, 4, h, width), BF16),
        compiler_params=_params("arbitrary", "arbitrary"),
        name="hy_fwd",
    )(*mats, u3, tt, tmid)


def _hy_inv_kernel(ce_ref, seg_ref, cog_ref, sog_ref, y_ref, g_ref, u_ref, bias_ref, o_ref, *scr,
                   tm, natural, order):
    i = pl.program_id(1)
    rows = pl.ds(pl.multiple_of(i * tm, tm), tm)
    ye = _dot(ce_ref[rows, :], y_ref[0, 0]) + _dot(seg_ref[rows, :], y_ref[0, 2])
    yo = _dot(cog_ref[rows, :], y_ref[0, 1]) + _dot(sog_ref[rows, :], y_ref[0, 3])
    bias = bias_ref[order:order + 1, :]
    ze = g_ref[0, 0].astype(F32) * (ye + u_ref[0, 0].astype(F32) * bias)
    zo = g_ref[0, 1].astype(F32) * (yo + u_ref[0, 1].astype(F32) * bias)
    if natural:
        scr_ref, = scr
        for sl in range(ze.shape[1] // LANES):
            c = slice(sl * LANES, (sl + 1) * LANES)
            scr_ref[sl, pl.ds(0, tm, stride=2), :] = ze[:, c]
            scr_ref[sl, pl.ds(1, tm, stride=2), :] = zo[:, c]
        for sl in range(ze.shape[1] // LANES):
            o_ref[0, :, sl * LANES:(sl + 1) * LANES] = scr_ref[sl].astype(o_ref.dtype)
    else:
        o_ref[0, 0] = ze.astype(o_ref.dtype)
        o_ref[0, 1] = zo.astype(o_ref.dtype)


def _hy_inv(mats, y4, g4, g_col, u4, u_col, bias, order, width, natural, tm=512):
    bsz, _, h, _ = y4.shape
    par = lambda col: pl.BlockSpec((1, 2, tm, width), lambda b, i: (b, 0, i, col))
    if natural:
        out_spec = pl.BlockSpec((1, 2 * tm, width), lambda b, i: (b, i, 0))
        out_shape = jax.ShapeDtypeStruct((bsz, 2 * h, width), BF16)
        scratch = [pltpu.VMEM((width // LANES, 2 * tm, LANES), F32)]
    else:
        out_spec = par(0)
        out_shape = jax.ShapeDtypeStruct((bsz, 2, h, width), BF16)
        scratch = []
    return pl.pallas_call(
        functools.partial(_hy_inv_kernel, tm=tm, natural=natural, order=order),
        grid=(bsz, h // tm),
        in_specs=[_resident((h, h))] * 4
                 + [pl.BlockSpec((1, 4, h, width), lambda b, i: (b, 0, 0, 0)),
                    par(g_col), par(u_col),
                    pl.BlockSpec(bias.shape, lambda b, i: (0, 0))],
        out_specs=out_spec,
        out_shape=out_shape,
        scratch_shapes=scratch,
        compiler_params=_params("arbitrary", "arbitrary"),
        name="hy_inv",
    )(*mats, y4, g4, u4, bias)


def _cf_kernel(*refs, step_rows, n_cast):
    (am_ref, ap_ref, an_ref, bm_ref, bp_ref, bn_ref, w_ref, cb_ref, lg_ref, lb_ref) = refs[:10]
    cast_in = refs[10:10 + n_cast]
    o_ref = refs[10 + n_cast]
    cast_out = refs[11 + n_cast:11 + 2 * n_cast]
    us_ref, cs_ref, wb_ref = refs[11 + 2 * n_cast:]
    width = wb_ref.shape[-1]
    nsl = width // LANES
    j = pl.program_id(1)

    for src, dst in zip(cast_in, cast_out):
        dst[...] = src[...].astype(dst.dtype)

    @pl.when((pl.program_id(0) == 0) & (j == 0))
    def _():
        for t in range(CF_KERNEL):
            wb_ref[t] = jnp.broadcast_to(w_ref[t:t + 1, :], (8, width))

    glu = lambda a, b: a[0].astype(F32) * _sigmoid(b[0].astype(F32))
    u_prev = jnp.where(j == 0, 0.0, glu(ap_ref, bp_ref))
    u_main = glu(am_ref, bm_ref)
    u_next = jnp.where(j == pl.num_programs(1) - 1, 0.0, glu(an_ref, bn_ref))
    for sl in range(nsl):
        c = slice(sl * LANES, (sl + 1) * LANES)
        us_ref[sl, 0:CF_PAD, :] = u_prev[:, c]
        us_ref[sl, CF_PAD:CF_PAD + step_rows, :] = u_main[:, c]
        us_ref[sl, CF_PAD + step_rows:, :] = u_next[:, c]

    half = CF_KERNEL // 2
    nph = 4
    prow = step_rows // nph
    for sl in range(nsl):
        c = slice(sl * LANES, (sl + 1) * LANES)
        accs = [jnp.zeros((prow // 8, 8, LANES), F32) + cb_ref[:, c][None]] * nph
        for r in range(CF_KERNEL + nph - 1):
            tap = us_ref[sl, pl.ds(CF_PAD - half + r, prow, stride=nph), :].reshape(prow // 8, 8, LANES)
            for p in range(nph):
                if 0 <= r - p < CF_KERNEL:
                    accs[p] = accs[p] + wb_ref[r - p, :, c][None] * tap
        for p in range(nph):
            cs_ref[sl, pl.ds(p, prow, stride=nph), :] = accs[p].reshape(prow, LANES)

    tot = cs_ref[0]
    for sl in range(1, nsl):
        tot = tot + cs_ref[sl]
    mu = jnp.sum(tot, axis=-1, keepdims=True) * (1.0 / width)
    sq = None
    for sl in range(nsl):
        cen = cs_ref[sl] - mu
        sq = cen * cen if sq is None else sq + cen * cen
    rstd = lax.rsqrt(jnp.sum(sq, axis=-1, keepdims=True) * (1.0 / width) + EPS)
    for sl in range(nsl):
        c = slice(sl * LANES, (sl + 1) * LANES)
        y = (cs_ref[sl] - mu) * rstd * lg_ref[:, c] + lb_ref[:, c]
        o_ref[0, :, c] = (y * _sigmoid(y)).astype(o_ref.dtype)


def _cf_conv(proj3, a_col, b_col, w, cb, lg, lb, width, casts, step_rows=256):
    bsz, seq, _ = proj3.shape
    nsteps = seq // step_rows
    total = bsz * nsteps
    nblk = step_rows // CF_PAD
    last_blk = seq // CF_PAD - 1
    vec = pl.BlockSpec((1, width), lambda b, j: (0, 0))
    main = lambda col: pl.BlockSpec((1, step_rows, width), lambda b, j: (b, j, col))
    prev = lambda col: pl.BlockSpec((1, CF_PAD, width), lambda b, j: (b, jnp.maximum(j * nblk - 1, 0), col))
    nxt = lambda col: pl.BlockSpec((1, CF_PAD, width),
                                   lambda b, j: (b, jnp.minimum((j + 1) * nblk, last_blk), col))

    def cast_spec(c):
        n = next(n for n in (total, total // 2, total // 4) if c.shape[0] % (16 * n) == 0)
        every = total // n
        return pl.BlockSpec((c.shape[0] // n, c.shape[1]), lambda b, j: ((b * nsteps + j) // every, 0))
    cast_specs = [cast_spec(c) for c in casts]
    outs = pl.pallas_call(
        functools.partial(_cf_kernel, step_rows=step_rows, n_cast=len(casts)),
        grid=(bsz, nsteps),
        in_specs=[main(a_col), prev(a_col), nxt(a_col), main(b_col), prev(b_col), nxt(b_col),
                  pl.BlockSpec((CF_KERNEL, width), lambda b, j: (0, 0)),
                  vec, vec, vec] + cast_specs,
        out_specs=[pl.BlockSpec((1, step_rows, width), lambda b, j: (b, j, 0))] + cast_specs,
        out_shape=[jax.ShapeDtypeStruct((bsz, seq, width), BF16)]
                  + [jax.ShapeDtypeStruct(c.shape, BF16) for c in casts],
        scratch_shapes=[pltpu.VMEM((width // LANES, step_rows + 2 * CF_PAD, LANES), F32),
                        pltpu.VMEM((width // LANES, step_rows, LANES), F32),
                        pltpu.VMEM((CF_KERNEL, 8, width), F32)],
        compiler_params=_params("arbitrary", "arbitrary"),
        name="cf_conv",
    )(proj3, proj3, proj3, proj3, proj3, proj3, w, cb, lg, lb, *casts)
    return outs[0], outs[1:]


def _merge_kernel(ya_ref, yb_ref, ga0_ref, ga1_ref, gb0_ref, gb1_ref, x_ref, pa_ref, pb_ref, wo_ref,
                  g1_ref, x1_ref):
    a = _dot(ya_ref[...], pa_ref[...])
    b = _dot(yb_ref[...], pb_ref[...])
    w = ga0_ref.shape[1]
    m0 = _sigmoid(ga0_ref[...].astype(F32)) * a[:, :w] + _sigmoid(gb0_ref[...].astype(F32)) * b[:, :w]
    m1 = _sigmoid(ga1_ref[...].astype(F32)) * a[:, w:] + _sigmoid(gb1_ref[...].astype(F32)) * b[:, w:]
    o = _dot(m0.astype(BF16), wo_ref[0:w, :]) + _dot(m1.astype(BF16), wo_ref[w:2 * w, :])
    x1_ref[...] = x_ref[...] + o * lax.rsqrt(jnp.mean(o * o, axis=-1, keepdims=True) + EPS) * g1_ref[...]


def _merge(ya, yb, proj, ga_col, gb_col, x2d, pa, pb, wo, g1, tm=256):
    m, d = x2d.shape
    wa = ya.shape[1]
    assert d == 2 * wa
    const = lambda shape: pl.BlockSpec(shape, lambda i: (0, 0))
    gate = lambda col: pl.BlockSpec((tm, wa), lambda i: (i, col))
    return pl.pallas_call(
        _merge_kernel,
        grid=(m // tm,),
        in_specs=[pl.BlockSpec((tm, wa), lambda i: (i, 0)),
                  pl.BlockSpec((tm, wa), lambda i: (i, 0)),
                  gate(ga_col), gate(ga_col + 1), gate(gb_col), gate(gb_col + 1),
                  pl.BlockSpec((tm, d), lambda i: (i, 0)),
                  const((wa, d)), const((wa, d)), const((d, d)), const((1, d))],
        out_specs=pl.BlockSpec((tm, d), lambda i: (i, 0)),
        out_shape=jax.ShapeDtypeStruct((m, d), F32),
        compiler_params=_params("parallel"),
        name="merge",
    )(ya, yb, proj, proj, proj, proj, x2d, pa, pb, wo, g1)


def _ffn_kernel(x1_ref, gpre_ref, wg_ref, wu_ref, wd_ref, gpost_ref, o_ref, h_ref):
    j = pl.program_id(1)

    @pl.when(j == 0)
    def _():
        x1 = x1_ref[...]
        ms = jnp.mean(x1 * x1, axis=-1, keepdims=True)
        h_ref[...] = (x1 * lax.rsqrt(ms + EPS) * gpre_ref[...]).astype(BF16)
        o_ref[...] = jnp.zeros_like(o_ref)

    h = h_ref[...]
    gate = _dot(h, wg_ref[...])
    up = _dot(h, wu_ref[...])
    act = (gate * _sigmoid(gate) * up).astype(BF16)
    o_ref[...] += _dot(act, wd_ref[...])

    @pl.when(j == pl.num_programs(1) - 1)
    def _():
        a = o_ref[...]
        o_ref[...] = x1_ref[...] + a * lax.rsqrt(jnp.mean(a * a, axis=-1, keepdims=True) + EPS) * gpost_ref[...]


def _ffn(x1, gpre, wgu, wd, gpost, tm=1024, th=256):
    m, d = x1.shape
    hidden = wd.shape[0]
    nh = hidden // th
    return pl.pallas_call(
        _ffn_kernel,
        grid=(m // tm, nh),
        in_specs=[pl.BlockSpec((tm, d), lambda i, j: (i, 0)),
                  pl.BlockSpec((1, d), lambda i, j: (0, 0)),
                  pl.BlockSpec((d, th), lambda i, j: (0, j)),
                  pl.BlockSpec((d, th), lambda i, j: (0, nh + j)),
                  pl.BlockSpec((th, d), lambda i, j: (j, 0)),
                  pl.BlockSpec((1, d), lambda i, j: (0, 0))],
        out_specs=pl.BlockSpec((tm, d), lambda i, j: (i, 0)),
        out_shape=jax.ShapeDtypeStruct((m, d), F32),
        scratch_shapes=[pltpu.VMEM((tm, d), BF16)],
        compiler_params=_params("parallel", "arbitrary"),
        name="ffn",
    )(x1, gpre, wgu, wgu, wd, gpost)


def kernel(x, mix_pre_g, w_in, hy_conv_w, hy_conv_b, hy_filt_w1, hy_filt_b1, hy_filt_fr1, hy_filt_w2,
           hy_filt_b2, hy_filt_fr2, hy_filt_w3, hy_bias, hy_proj, cf_dw_w, cf_dw_b, cf_ln_g, cf_ln_b,
           cf_proj, w_out, mix_post_g, ffn_pre_g, ffn_w_gu, ffn_w_down, ffn_post_g):
    bsz, seq, d = x.shape
    depth = w_in.shape[0]
    hw = hy_proj.shape[1]
    cw = cf_proj.shape[1]
    assert hw == cw and d % hw == 0
    row = lambda v: v.reshape(1, -1)

    assert depth == 1
    x2d = x.reshape(bsz * seq, d)
    for l in range(depth):
        ce, co, sef, sof, seg, cog, sog = _dft_tables(seq)
        s, dm, tmid, w_in_b = _filt_mlp(seq, hw, hy_filt_w1[l], row(hy_filt_b1[l]), row(hy_filt_fr1[l]),
                                        hy_filt_w2[l], row(hy_filt_b2[l]), row(hy_filt_fr2[l]),
                                        hy_filt_w3[l], w_in[l], d // 2)
        tt, w_in_a = _filt_dft(ce, co, sef, sof, s, dm, w_in[l], d // 2)
        proj = _in_proj(x2d, row(mix_pre_g[l]), w_in_a, w_in_b)
        proj3 = proj.reshape(bsz, seq, -1)
        cf_a_col, cf_b_col = 3, 4
        ga_col, gb_col = 5, 5 + d // hw

        hy = _hy_conv3(proj3, hy_conv_w[l], row(hy_conv_b[l]), 3 * hw)
        hy4 = hy.reshape(bsz, 2, seq // 2, 3 * hw)
        fwd, inv = (ce, co, sef, sof), (ce, seg, cog, sog)
        y1 = _hy_fwd(fwd, hy, 0, tt, tmid, 0, hw)
        z4 = _hy_inv(inv, y1, hy4, 1, hy4, 0, hy_bias[l], 0, hw, natural=False)
        y2 = _hy_fwd(fwd, z4.reshape(bsz, seq, hw), 0, tt, tmid, 1, hw)
        y_a = _hy_inv(inv, y2, hy4, 2, z4, 0, hy_bias[l], 1, hw, natural=True)

        y_b, (hy_proj_bf, cf_proj_bf, w_out_bf, w_gu_bf, w_down_bf) = _cf_conv(
            proj3, cf_a_col, cf_b_col, cf_dw_w[l], row(cf_dw_b[l]), row(cf_ln_g[l]), row(cf_ln_b[l]), cw,
            [hy_proj[l], cf_proj[l], w_out[l], ffn_w_gu[l], ffn_w_down[l]])

        x1 = _merge(y_a.reshape(bsz * seq, hw), y_b.reshape(bsz * seq, cw), proj, ga_col, gb_col, x2d,
                    hy_proj_bf, cf_proj_bf, w_out_bf, row(mix_post_g[l]))
        x2d = _ffn(x1, row(ffn_pre_g[l]), w_gu_bf, w_down_bf, row(ffn_post_g[l]))
    return x2d.reshape(bsz, seq, d)
```

```python
import functools
import math

import jax
import jax.numpy as jnp
from jax import lax
from jax.experimental import pallas as pl
from jax.experimental.pallas import tpu as pltpu

F32 = jnp.float32
BF16 = jnp.bfloat16
EPS = 1e-6

HY_ORDER = 2
HY_EMB_BANDS = 16
HY_DECAY_TARGET = 1e-2
HY_MIN_DECAY = math.log(HY_DECAY_TARGET) / 1.5
HY_MAX_DECAY = math.log(HY_DECAY_TARGET) / 0.3
HY_SHORT = 3
CF_KERNEL = 31
CF_PAD = 16
LANES = 128

V7X_VMEM_LIMIT_BYTES = 56 * 1024 * 1024


def _params(*sem):
    return pltpu.CompilerParams(dimension_semantics=sem, vmem_limit_bytes=V7X_VMEM_LIMIT_BYTES)


def _dot(a, b):
    return jnp.dot(a, b, preferred_element_type=F32)


def _sigmoid(x):
    return 1.0 / (1.0 + jnp.exp(-x))


def _resident(shape):
    zeros = (0,) * len(shape)
    return pl.BlockSpec(shape, lambda *_: zeros, pipeline_mode=pl.Buffered(1))


def _parity_split_store(val, scr_ref, dst_ref):
    rows, cols = val.shape
    h = rows // 2
    for sl in range(cols // LANES):
        scr_ref[sl] = val[:, sl * LANES:(sl + 1) * LANES]
    for sl in range(cols // LANES):
        c = slice(sl * LANES, (sl + 1) * LANES)
        dst_ref[0:h, c] = scr_ref[sl, pl.ds(0, h, stride=2), :].astype(dst_ref.dtype)
        dst_ref[h:rows, c] = scr_ref[sl, pl.ds(1, h, stride=2), :].astype(dst_ref.dtype)


def _dft_kernel(w_ref, ce_ref, co_ref, sef_ref, sof_ref, seg_ref, cog_ref, sog_ref, wbf_ref,
                rc_ref, rs_ref, pc_ref, ps_ref, *, tm, h, chunk):
    i = pl.program_id(0)
    mask = 4 * h - 1
    scale = 2.0 * math.pi / (4 * h)
    wbf_ref[...] = w_ref[...].astype(BF16)

    @pl.when(i == 0)
    def _():
        def base(j, carry):
            r0 = pl.multiple_of(j * chunk, chunk)
            r = r0 + lax.broadcasted_iota(jnp.int32, (chunk, h), 0)
            c = lax.broadcasted_iota(jnp.int32, (chunk, h), 1)
            th_e = ((2 * r * c) & mask).astype(F32) * scale
            th_o = ((r * (2 * c + 1)) & mask).astype(F32) * scale
            rc_ref[pl.ds(r0, chunk), :] = jnp.cos(th_e)
            rs_ref[pl.ds(r0, chunk), :] = jnp.sin(th_e)
            pc_ref[pl.ds(r0, chunk), :] = jnp.cos(th_o)
            ps_ref[pl.ds(r0, chunk), :] = jnp.sin(th_o)
            return carry
        lax.fori_loop(0, tm // chunk, base, 0)

    k0 = i * tm
    c1 = lax.broadcasted_iota(jnp.int32, (1, h), 1)
    th = ((2 * k0 * c1) & mask).astype(F32) * scale
    ce0, se0 = jnp.cos(th), jnp.sin(th)
    th = ((k0 * (2 * c1 + 1)) & mask).astype(F32) * scale
    co0, so0 = jnp.cos(th), jnp.sin(th)
    th = (((2 * k0 + 1) * c1) & mask).astype(F32) * scale
    cg0, sg0 = jnp.cos(th), jnp.sin(th)

    def tile(j, carry):
        r0 = pl.multiple_of(j * chunk, chunk)
        rows = pl.ds(r0, chunk)
        rc, rs, pc, ps = rc_ref[rows, :], rs_ref[rows, :], pc_ref[rows, :], ps_ref[rows, :]
        row = k0 + r0 + lax.broadcasted_iota(jnp.int32, (chunk, h), 0)
        col = lax.broadcasted_iota(jnp.int32, (chunk, h), 1)
        alt_col = (1 - 2 * (col & 1)).astype(F32)
        alt_row = (1 - 2 * (row & 1)).astype(F32)
        se = rs * ce0 + rc * se0
        ce_ref[rows, :] = (rc * ce0 - rs * se0).astype(BF16)
        co_ref[rows, :] = (pc * co0 - ps * so0).astype(BF16)
        sef_ref[rows, :] = jnp.where(row == 0, alt_col, se).astype(BF16)
        sof_ref[rows, :] = jnp.where(row == 0, alt_col, ps * co0 + pc * so0).astype(BF16)
        seg_ref[rows, :] = jnp.where(col == 0, alt_row, se).astype(BF16)
        cog_ref[rows, :] = (rc * cg0 - rs * sg0).astype(BF16)
        sog_ref[rows, :] = jnp.where(col == 0, alt_row, rs * cg0 + rc * sg0).astype(BF16)
        return carry
    lax.fori_loop(0, tm // chunk, tile, 0)


def _cast_rows_spec(w, row0, rows, steps, step_of):
    wr = rows // steps
    assert rows % steps == 0 and wr % 16 == 0 and row0 % wr == 0
    return pl.BlockSpec((wr, w.shape[1]), lambda *ids: (row0 // wr + step_of(*ids), 0))


def _dft_tables(seq, w, w_row0, w_rows, tm=256, chunk=32):
    h = seq // 2
    steps = h // tm
    out = jax.ShapeDtypeStruct((h, h), BF16)
    spec = pl.BlockSpec((tm, h), lambda i: (i, 0))
    return pl.pallas_call(
        functools.partial(_dft_kernel, tm=tm, h=h, chunk=chunk),
        grid=(steps,),
        in_specs=[_cast_rows_spec(w, w_row0, w_rows, steps, lambda i: i)],
        out_specs=[spec] * 7 + [_cast_rows_spec(w, 0, w_rows, steps, lambda i: i)],
        out_shape=[out] * 7 + [jax.ShapeDtypeStruct((w_rows, w.shape[1]), BF16)],
        scratch_shapes=[pltpu.VMEM((tm, h), F32)] * 4,
        compiler_params=_params("arbitrary"),
        name="dft_tables",
    )(w)


def _in_proj_kernel(x_ref, g_ref, *refs):
    *w_refs, o_ref, h_ref = refs
    @pl.when(pl.program_id(1) == 0)
    def _():
        x = x_ref[...]
        ms = jnp.mean(x * x, axis=-1, keepdims=True)
        h_ref[...] = (x * lax.rsqrt(ms + EPS) * g_ref[...]).astype(BF16)

    acc, k0 = None, 0
    for w_ref in w_refs:
        k1 = k0 + w_ref.shape[0]
        part = _dot(h_ref[:, k0:k1], w_ref[...])
        acc = part if acc is None else acc + part
        k0 = k1
    o_ref[...] = acc.astype(o_ref.dtype)


def _in_proj(x2d, g, w_pieces, tm=1024, tn=1536):
    m, d = x2d.shape
    n = w_pieces[0].shape[1]
    assert sum(w.shape[0] for w in w_pieces) == d
    return pl.pallas_call(
        _in_proj_kernel,
        grid=(m // tm, n // tn),
        in_specs=[pl.BlockSpec((tm, d), lambda i, j: (i, 0)),
                  pl.BlockSpec((1, d), lambda i, j: (0, 0))]
                 + [pl.BlockSpec((w.shape[0], tn), lambda i, j: (0, j)) for w in w_pieces],
        out_specs=pl.BlockSpec((tm, tn), lambda i, j: (i, j)),
        out_shape=jax.ShapeDtypeStruct((m, n), BF16),
        scratch_shapes=[pltpu.VMEM((tm, d), BF16)],
        compiler_params=_params("parallel", "arbitrary"),
        name="in_proj",
    )(x2d, g, *w_pieces)


def _conv3_kernel(p_ref, w_ref, b_ref, o_ref, scr_ref):
    rows, cols = p_ref.shape[1], p_ref.shape[2]
    h = rows // 2
    pad = 8
    zeros = jnp.zeros((pad, LANES), F32)
    for sl in range(cols // LANES):
        c = slice(sl * LANES, (sl + 1) * LANES)
        scr_ref[sl, 0:pad, :] = zeros
        scr_ref[sl, pad:pad + rows, :] = p_ref[0, :, c].astype(F32)
        scr_ref[sl, pad + rows:, :] = zeros
    for sl in range(cols // LANES):
        c = slice(sl * LANES, (sl + 1) * LANES)
        w0, w1, w2, b = w_ref[0:1, c], w_ref[1:2, c], w_ref[2:3, c], b_ref[:, c]
        xom = scr_ref[sl, pl.ds(pad - 1, h, stride=2), :]
        xe = scr_ref[sl, pl.ds(pad, h, stride=2), :]
        xo = scr_ref[sl, pl.ds(pad + 1, h, stride=2), :]
        xep = scr_ref[sl, pl.ds(pad + 2, h, stride=2), :]
        o_ref[0, 0:h, c] = (w0 * xom + w1 * xe + w2 * xo + b).astype(o_ref.dtype)
        o_ref[0, h:rows, c] = (w0 * xe + w1 * xo + w2 * xep + b).astype(o_ref.dtype)


def _hy_conv3(proj3, w, b, width, ct=1024):
    bsz, seq, _ = proj3.shape
    return pl.pallas_call(
        _conv3_kernel,
        grid=(bsz, width // ct),
        in_specs=[pl.BlockSpec((1, seq, ct), lambda i, j: (i, 0, j)),
                  pl.BlockSpec((HY_SHORT, ct), lambda i, j: (0, j)),
                  pl.BlockSpec((1, ct), lambda i, j: (0, j))],
        out_specs=pl.BlockSpec((1, seq, ct), lambda i, j: (i, 0, j)),
        out_shape=jax.ShapeDtypeStruct((bsz, seq, width), BF16),
        scratch_shapes=[pltpu.VMEM((ct // LANES, seq + 16, LANES), F32)],
        compiler_params=_params("parallel", "parallel"),
        name="hy_conv3",
    )(proj3, w, b)


def _filt_mlp_kernel(mlp_ref, w3f_ref, w3b_ref, w_ref, s_ref, d_ref, tmid_ref, wbf_ref, h2_ref, scr_ref,
                     *, seq, width, ct):
    jc = pl.program_id(1)
    wbf_ref[...] = w_ref[...].astype(BF16)
    n = lax.broadcasted_iota(jnp.int32, (seq, 1), 0).astype(F32)
    t = n / (seq - 1)

    @pl.when((pl.program_id(0) == 0) & (jc == 0))
    def _():
        nl = lax.broadcasted_iota(jnp.int32, (1, seq), 1).astype(F32)
        tl = nl / (seq - 1)
        wang = 2.0 * math.pi * nl / seq
        band = lax.broadcasted_iota(jnp.int32, (HY_EMB_BANDS, 1), 0).astype(F32)
        f = 1e-4 + band * ((HY_EMB_BANDS - 1 - 1e-4) / (HY_EMB_BANDS - 1))
        fw = f * wang
        p = mlp_ref[...]
        hid = p.shape[0]
        nb = HY_EMB_BANDS
        w1t, w1c, w1s = p[:, 0:1], p[:, 1:1 + nb], p[:, 1 + nb:1 + 2 * nb]
        c0 = 1 + 2 * nb
        b1, fr1, b2, fr2 = p[:, c0:c0 + 1], p[:, c0 + 1:c0 + 2], p[:, c0 + 2:c0 + 3], p[:, c0 + 3:c0 + 4]
        w2t = p[:, c0 + 4:c0 + 4 + hid]
        pre1 = w1t * tl + _dot(w1c, jnp.cos(fw)) - _dot(w1s, jnp.sin(fw)) + b1
        h1 = jnp.sin(fr1 * pre1)
        h2t = jnp.sin(fr2 * (_dot(w2t, h1) + b2))
        h2_ref[...] = h2t.T

    h2 = h2_ref[...]
    ch = (jc * ct + lax.broadcasted_iota(jnp.int32, (1, ct), 1)).astype(F32)
    delta = HY_MIN_DECAY + ch * ((HY_MAX_DECAY - HY_MIN_DECAY) / (width - 1))
    decay = jnp.exp(-t * jnp.abs(delta))
    kf = _dot(h2, w3f_ref[...]) * decay
    kb = _dot(h2, w3b_ref[...]) * decay
    row = lax.broadcasted_iota(jnp.int32, (seq, ct), 0)
    kb = jnp.where(row == 0, 0.0, kb)
    l1 = jnp.sum(jnp.abs(kf), axis=0, keepdims=True) + jnp.sum(jnp.abs(kb), axis=0, keepdims=True)
    inv = 1.0 / l1
    s = (kf + kb) * inv
    dm = (kf - kb) * inv
    phase = row & 3
    cmid = jnp.where(phase == 0, 1.0, jnp.where(phase == 2, -1.0, 0.0))
    smid = jnp.where(phase == 1, 1.0, jnp.where(phase == 3, -1.0, 0.0))
    wmid = 2.0 / (2 * seq)
    tmid_ref[0:1, :] = jnp.sum(s * cmid, axis=0, keepdims=True) * wmid
    tmid_ref[1:2, :] = jnp.sum(dm * smid, axis=0, keepdims=True) * (-wmid)
    _parity_split_store(s, scr_ref, s_ref)
    _parity_split_store(dm, scr_ref, d_ref)


def _pack_filter_mlp(w1, b1, fr1, w2, b2, fr2):
    rows = jnp.concatenate([w1, b1.reshape(1, -1), fr1.reshape(1, -1), b2.reshape(1, -1), fr2.reshape(1, -1), w2],
                           axis=0)
    rows = jnp.pad(rows, ((0, -rows.shape[0] % LANES), (0, 0)))
    return rows.T


def _filt_mlp(seq, width, w1, b1, fr1, w2, b2, fr2, w3, w, w_row0, ct=256):
    hid = w2.shape[0]
    mlp = _pack_filter_mlp(w1, b1, fr1, w2, b2, fr2)
    nct = width // ct
    steps = HY_ORDER * nct
    w_rows = w.shape[0] - w_row0
    wr = w_rows // steps
    assert w_row0 % wr == 0
    small = lambda shape: pl.BlockSpec(shape, lambda o, j: (0, 0))
    out_spec = pl.BlockSpec((seq, ct), lambda o, j: (0, o * nct + j))
    return pl.pallas_call(
        functools.partial(_filt_mlp_kernel, seq=seq, width=width, ct=ct),
        grid=(HY_ORDER, nct),
        in_specs=[small(mlp.shape),
                  pl.BlockSpec((hid, ct), lambda o, j: (0, o * 2 * nct + j)),
                  pl.BlockSpec((hid, ct), lambda o, j: (0, o * 2 * nct + nct + j)),
                  pl.BlockSpec((wr, w.shape[1]), lambda o, j: (w_row0 // wr + o * nct + j, 0))],
        out_specs=[out_spec, out_spec, pl.BlockSpec((2, ct), lambda o, j: (0, o * nct + j)),
                   pl.BlockSpec((wr, w.shape[1]), lambda o, j: (o * nct + j, 0))],
        out_shape=[jax.ShapeDtypeStruct((seq, HY_ORDER * width), BF16),
                   jax.ShapeDtypeStruct((seq, HY_ORDER * width), BF16),
                   jax.ShapeDtypeStruct((2, HY_ORDER * width), F32),
                   jax.ShapeDtypeStruct((w_rows, w.shape[1]), BF16)],
        scratch_shapes=[pltpu.VMEM((seq, hid), F32), pltpu.VMEM((ct // LANES, seq, LANES), F32)],
        compiler_params=_params("arbitrary", "arbitrary"),
        name="filt_mlp",
    )(mlp, w3, w3, w)


def _filt_dft_kernel(ce_ref, co_ref, sef_ref, sof_ref, s_ref, d_ref, w_ref, t_ref, wbf_ref, *, tm, seq):
    h = seq // 2
    wbf_ref[...] = w_ref[...].astype(BF16)
    rows = pl.ds(pl.multiple_of(pl.program_id(1) * tm, tm), tm)
    pc = _dot(ce_ref[rows, :], s_ref[0:h, :])
    qc = _dot(co_ref[rows, :], s_ref[h:seq, :])
    ps = _dot(sef_ref[rows, :], d_ref[0:h, :])
    qs = _dot(sof_ref[rows, :], d_ref[h:seq, :])
    row = pl.program_id(1) * tm + lax.broadcasted_iota(jnp.int32, pc.shape, 0)
    wk = jnp.where(row == 0, 1.0, 2.0) * (1.0 / (2 * seq))
    t_ref[0] = ((pc + qc) * wk).astype(BF16)
    t_ref[1] = jnp.where(row == 0, 0.0, -(ps + qs) * wk).astype(BF16)
    t_ref[2] = ((pc - qc) * wk).astype(BF16)
    t_ref[3] = jnp.where(row == 0, 0.0, (ps - qs) * wk).astype(BF16)


def _filt_dft(ce, co, sef, sof, s, d, w, w_rows, tm=512, tn=512):
    h = ce.shape[0]
    seq, cols = s.shape
    ni = h // tm
    wr = w_rows // (cols // tn * ni)
    tab = pl.BlockSpec((seq, tn), lambda j, i: (0, j))
    wspec = pl.BlockSpec((wr, w.shape[1]), lambda j, i: (j * ni + i, 0))
    return pl.pallas_call(
        functools.partial(_filt_dft_kernel, tm=tm, seq=seq),
        grid=(cols // tn, ni),
        in_specs=[_resident((h, h))] * 4 + [tab, tab, wspec],
        out_specs=[pl.BlockSpec((4, tm, tn), lambda j, i: (0, i, j)), wspec],
        out_shape=[jax.ShapeDtypeStruct((4, h, cols), BF16),
                   jax.ShapeDtypeStruct((w_rows, w.shape[1]), BF16)],
        compiler_params=_params("arbitrary", "arbitrary"),
        name="filt_dft",
    )(ce, co, sef, sof, s, d, w)


def _hy_fwd_kernel(ce_ref, co_ref, sef_ref, sof_ref, u_ref, t_ref, tmid_ref, y_ref, *, tm):
    h = ce_ref.shape[0]
    i = pl.program_id(1)
    rows = pl.ds(pl.multiple_of(i * tm, tm), tm)
    ue = u_ref[0, 0:h, :]
    uo = u_ref[0, h:2 * h, :]
    pc = _dot(ce_ref[rows, :], ue)
    qc = _dot(co_ref[rows, :], uo)
    ps = _dot(sef_ref[rows, :], ue)
    qs = _dot(sof_ref[rows, :], uo)
    r0 = (i * tm + lax.broadcasted_iota(jnp.int32, pc.shape, 0)) == 0
    tr1, ti1 = t_ref[0, rows, :].astype(F32), t_ref[1, rows, :].astype(F32)
    tr2, ti2 = t_ref[2, rows, :].astype(F32), t_ref[3, rows, :].astype(F32)
    a1, a2 = pc + qc, pc - qc
    b1, b2 = ps + qs, qs - ps
    yr1, yi1 = a1 * tr1 + b1 * ti1, b1 * tr1 - a1 * ti1
    yr2, yi2 = a2 * tr2 + b2 * ti2, b2 * tr2 - a2 * ti2
    trh, tih = tmid_ref[0:1, :], tmid_ref[1:2, :]
    y_ref[0, 0] = (yr1 + yr2).astype(BF16)
    y_ref[0, 1] = (yr1 - yr2).astype(BF16)
    y_ref[0, 2] = jnp.where(r0, ps * trh + qs * tih, yi1 - yi2).astype(BF16)
    y_ref[0, 3] = jnp.where(r0, qs * trh - ps * tih, yi1 + yi2).astype(BF16)


def _hy_fwd(mats, u3, u_col, tt, tmid, t_col, width, tm=512):
    bsz, seq, _ = u3.shape
    h = seq // 2
    return pl.pallas_call(
        functools.partial(_hy_fwd_kernel, tm=tm),
        grid=(bsz, h // tm),
        in_specs=[_resident((h, h))] * 4
                 + [pl.BlockSpec((1, seq, width), lambda b, i: (b, 0, u_col)),
                    pl.BlockSpec((4, h, width), lambda b, i: (0, 0, t_col), pipeline_mode=pl.Buffered(1)),
                    pl.BlockSpec((2, width), lambda b, i: (0, t_col))],
        out_specs=pl.BlockSpec((1, 4, tm, width), lambda b, i: (b, 0, i, 0)),
        out_shape=jax.ShapeDtypeStruct((bsz, 4, h, width), BF16),
        compiler_params=_params("arbitrary", "arbitrary"),
        name="hy_fwd",
    )(*mats, u3, tt, tmid)


def _hy_inv_kernel(ce_ref, seg_ref, cog_ref, sog_ref, y_ref, g_ref, u_ref, bias_ref, o_ref, *scr,
                   tm, natural, order):
    i = pl.program_id(1)
    rows = pl.ds(pl.multiple_of(i * tm, tm), tm)
    ye = _dot(ce_ref[rows, :], y_ref[0, 0]) + _dot(seg_ref[rows, :], y_ref[0, 2])
    yo = _dot(cog_ref[rows, :], y_ref[0, 1]) + _dot(sog_ref[rows, :], y_ref[0, 3])
    bias = bias_ref[order:order + 1, :]
    ze = g_ref[0, 0].astype(F32) * (ye + u_ref[0, 0].astype(F32) * bias)
    zo = g_ref[0, 1].astype(F32) * (yo + u_ref[0, 1].astype(F32) * bias)
    if natural:
        scr_ref, = scr
        for sl in range(ze.shape[1] // LANES):
            c = slice(sl * LANES, (sl + 1) * LANES)
            scr_ref[sl, pl.ds(0, tm, stride=2), :] = ze[:, c]
            scr_ref[sl, pl.ds(1, tm, stride=2), :] = zo[:, c]
        for sl in range(ze.shape[1] // LANES):
            o_ref[0, :, sl * LANES:(sl + 1) * LANES] = scr_ref[sl].astype(o_ref.dtype)
    else:
        o_ref[0, 0] = ze.astype(o_ref.dtype)
        o_ref[0, 1] = zo.astype(o_ref.dtype)


def _hy_inv(mats, y4, g4, g_col, u4, u_col, bias, order, width, natural, tm=512):
    bsz, _, h, _ = y4.shape
    par = lambda col: pl.BlockSpec((1, 2, tm, width), lambda b, i: (b, 0, i, col))
    if natural:
        out_spec = pl.BlockSpec((1, 2 * tm, width), lambda b, i: (b, i, 0))
        out_shape = jax.ShapeDtypeStruct((bsz, 2 * h, width), BF16)
        scratch = [pltpu.VMEM((width // LANES, 2 * tm, LANES), F32)]
    else:
        out_spec = par(0)
        out_shape = jax.ShapeDtypeStruct((bsz, 2, h, width), BF16)
        scratch = []
    return pl.pallas_call(
        functools.partial(_hy_inv_kernel, tm=tm, natural=natural, order=order),
        grid=(bsz, h // tm),
        in_specs=[_resident((h, h))] * 4
                 + [pl.BlockSpec((1, 4, h, width), lambda b, i: (b, 0, 0, 0)),
                    par(g_col), par(u_col),
                    pl.BlockSpec(bias.shape, lambda b, i: (0, 0))],
        out_specs=out_spec,
        out_shape=out_shape,
        scratch_shapes=scratch,
        compiler_params=_params("arbitrary", "arbitrary"),
        name="hy_inv",
    )(*mats, y4, g4, u4, bias)


def _cf_kernel(*refs, step_rows, n_cast):
    (am_ref, ap_ref, an_ref, bm_ref, bp_ref, bn_ref, w_ref, cb_ref, lg_ref, lb_ref) = refs[:10]
    cast_in = refs[10:10 + n_cast]
    o_ref = refs[10 + n_cast]
    cast_out = refs[11 + n_cast:11 + 2 * n_cast]
    us_ref, cs_ref, wb_ref = refs[11 + 2 * n_cast:]
    width = wb_ref.shape[-1]
    nsl = width // LANES
    j = pl.program_id(1)

    for src, dst in zip(cast_in, cast_out):
        dst[...] = src[...].astype(dst.dtype)

    @pl.when((pl.program_id(0) == 0) & (j == 0))
    def _():
        for t in range(CF_KERNEL):
            wb_ref[t] = jnp.broadcast_to(w_ref[t:t + 1, :], (8, width))

    glu = lambda a, b: a[0].astype(F32) * _sigmoid(b[0].astype(F32))
    u_prev = jnp.where(j == 0, 0.0, glu(ap_ref, bp_ref))
    u_main = glu(am_ref, bm_ref)
    u_next = jnp.where(j == pl.num_programs(1) - 1, 0.0, glu(an_ref, bn_ref))
    for sl in range(nsl):
        c = slice(sl * LANES, (sl + 1) * LANES)
        us_ref[sl, 0:CF_PAD, :] = u_prev[:, c]
        us_ref[sl, CF_PAD:CF_PAD + step_rows, :] = u_main[:, c]
        us_ref[sl, CF_PAD + step_rows:, :] = u_next[:, c]

    half = CF_KERNEL // 2
    nph = 4
    prow = step_rows // nph
    for sl in range(nsl):
        c = slice(sl * LANES, (sl + 1) * LANES)
        accs = [jnp.zeros((prow // 8, 8, LANES), F32) + cb_ref[:, c][None]] * nph
        for r in range(CF_KERNEL + nph - 1):
            tap = us_ref[sl, pl.ds(CF_PAD - half + r, prow, stride=nph), :].reshape(prow // 8, 8, LANES)
            for p in range(nph):
                if 0 <= r - p < CF_KERNEL:
                    accs[p] = accs[p] + wb_ref[r - p, :, c][None] * tap
        for p in range(nph):
            cs_ref[sl, pl.ds(p, prow, stride=nph), :] = accs[p].reshape(prow, LANES)

    tot = cs_ref[0]
    for sl in range(1, nsl):
        tot = tot + cs_ref[sl]
    mu = jnp.sum(tot, axis=-1, keepdims=True) * (1.0 / width)
    sq = None
    for sl in range(nsl):
        cen = cs_ref[sl] - mu
        sq = cen * cen if sq is None else sq + cen * cen
    rstd = lax.rsqrt(jnp.sum(sq, axis=-1, keepdims=True) * (1.0 / width) + EPS)
    for sl in range(nsl):
        c = slice(sl * LANES, (sl + 1) * LANES)
        y = (cs_ref[sl] - mu) * rstd * lg_ref[:, c] + lb_ref[:, c]
        o_ref[0, :, c] = (y * _sigmoid(y)).astype(o_ref.dtype)


def _cf_conv(proj3, a_col, b_col, w, cb, lg, lb, width, casts, step_rows=256):
    bsz, seq, _ = proj3.shape
    nsteps = seq // step_rows
    total = bsz * nsteps
    nblk = step_rows // CF_PAD
    last_blk = seq // CF_PAD - 1
    vec = pl.BlockSpec((1, width), lambda b, j: (0, 0))
    main = lambda col: pl.BlockSpec((1, step_rows, width), lambda b, j: (b, j, col))
    prev = lambda col: pl.BlockSpec((1, CF_PAD, width), lambda b, j: (b, jnp.maximum(j * nblk - 1, 0), col))
    nxt = lambda col: pl.BlockSpec((1, CF_PAD, width),
                                   lambda b, j: (b, jnp.minimum((j + 1) * nblk, last_blk), col))

    def cast_spec(c):
        n = next(n for n in (total, total // 2, total // 4) if c.shape[0] % (16 * n) == 0)
        every = total // n
        return pl.BlockSpec((c.shape[0] // n, c.shape[1]), lambda b, j: ((b * nsteps + j) // every, 0))
    cast_specs = [cast_spec(c) for c in casts]
    outs = pl.pallas_call(
        functools.partial(_cf_kernel, step_rows=step_rows, n_cast=len(casts)),
        grid=(bsz, nsteps),
        in_specs=[main(a_col), prev(a_col), nxt(a_col), main(b_col), prev(b_col), nxt(b_col),
                  pl.BlockSpec((CF_KERNEL, width), lambda b, j: (0, 0)),
                  vec, vec, vec] + cast_specs,
        out_specs=[pl.BlockSpec((1, step_rows, width), lambda b, j: (b, j, 0))] + cast_specs,
        out_shape=[jax.ShapeDtypeStruct((bsz, seq, width), BF16)]
                  + [jax.ShapeDtypeStruct(c.shape, BF16) for c in casts],
        scratch_shapes=[pltpu.VMEM((width // LANES, step_rows + 2 * CF_PAD, LANES), F32),
                        pltpu.VMEM((width // LANES, step_rows, LANES), F32),
                        pltpu.VMEM((CF_KERNEL, 8, width), F32)],
        compiler_params=_params("arbitrary", "arbitrary"),
        name="cf_conv",
    )(proj3, proj3, proj3, proj3, proj3, proj3, w, cb, lg, lb, *casts)
    return outs[0], outs[1:]


def _merge_kernel(ya_ref, yb_ref, ga0_ref, ga1_ref, gb0_ref, gb1_ref, x_ref, pa_ref, pb_ref, wo_ref,
                  g1_ref, x1_ref):
    a = _dot(ya_ref[...], pa_ref[...])
    b = _dot(yb_ref[...], pb_ref[...])
    w = ga0_ref.shape[1]
    m0 = _sigmoid(ga0_ref[...].astype(F32)) * a[:, :w] + _sigmoid(gb0_ref[...].astype(F32)) * b[:, :w]
    m1 = _sigmoid(ga1_ref[...].astype(F32)) * a[:, w:] + _sigmoid(gb1_ref[...].astype(F32)) * b[:, w:]
    o = _dot(m0.astype(BF16), wo_ref[0:w, :]) + _dot(m1.astype(BF16), wo_ref[w:2 * w, :])
    x1_ref[...] = x_ref[...] + o * lax.rsqrt(jnp.mean(o * o, axis=-1, keepdims=True) + EPS) * g1_ref[...]


def _merge(ya, yb, proj, ga_col, gb_col, x2d, pa, pb, wo, g1, tm=256):
    m, d = x2d.shape
    wa = ya.shape[1]
    assert d == 2 * wa
    const = lambda shape: pl.BlockSpec(shape, lambda i: (0, 0))
    gate = lambda col: pl.BlockSpec((tm, wa), lambda i: (i, col))
    return pl.pallas_call(
        _merge_kernel,
        grid=(m // tm,),
        in_specs=[pl.BlockSpec((tm, wa), lambda i: (i, 0)),
                  pl.BlockSpec((tm, wa), lambda i: (i, 0)),
                  gate(ga_col), gate(ga_col + 1), gate(gb_col), gate(gb_col + 1),
                  pl.BlockSpec((tm, d), lambda i: (i, 0)),
                  const((wa, d)), const((wa, d)), const((d, d)), const((1, d))],
        out_specs=pl.BlockSpec((tm, d), lambda i: (i, 0)),
        out_shape=jax.ShapeDtypeStruct((m, d), F32),
        compiler_params=_params("parallel"),
        name="merge",
    )(ya, yb, proj, proj, proj, proj, x2d, pa, pb, wo, g1)


def _ffn_kernel(x1_ref, gpre_ref, wg_ref, wu_ref, wd_ref, gpost_ref, o_ref, h_ref):
    j = pl.program_id(1)

    @pl.when(j == 0)
    def _():
        x1 = x1_ref[...]
        ms = jnp.mean(x1 * x1, axis=-1, keepdims=True)
        h_ref[...] = (x1 * lax.rsqrt(ms + EPS) * gpre_ref[...]).astype(BF16)
        o_ref[...] = jnp.zeros_like(o_ref)

    h = h_ref[...]
    gate = _dot(h, wg_ref[...])
    up = _dot(h, wu_ref[...])
    act = (gate * _sigmoid(gate) * up).astype(BF16)
    o_ref[...] += _dot(act, wd_ref[...])

    @pl.when(j == pl.num_programs(1) - 1)
    def _():
        a = o_ref[...]
        o_ref[...] = x1_ref[...] + a * lax.rsqrt(jnp.mean(a * a, axis=-1, keepdims=True) + EPS) * gpost_ref[...]


def _ffn(x1, gpre, wgu, wd, gpost, tm=1024, th=256):
    m, d = x1.shape
    hidden = wd.shape[0]
    nh = hidden // th
    return pl.pallas_call(
        _ffn_kernel,
        grid=(m // tm, nh),
        in_specs=[pl.BlockSpec((tm, d), lambda i, j: (i, 0)),
                  pl.BlockSpec((1, d), lambda i, j: (0, 0)),
                  pl.BlockSpec((d, th), lambda i, j: (0, j)),
                  pl.BlockSpec((d, th), lambda i, j: (0, nh + j)),
                  pl.BlockSpec((th, d), lambda i, j: (j, 0)),
                  pl.BlockSpec((1, d), lambda i, j: (0, 0))],
        out_specs=pl.BlockSpec((tm, d), lambda i, j: (i, 0)),
        out_shape=jax.ShapeDtypeStruct((m, d), F32),
        scratch_shapes=[pltpu.VMEM((tm, d), BF16)],
        compiler_params=_params("parallel", "arbitrary"),
        name="ffn",
    )(x1, gpre, wgu, wgu, wd, gpost)


def kernel(x, mix_pre_g, w_in, hy_conv_w, hy_conv_b, hy_filt_w1, hy_filt_b1, hy_filt_fr1, hy_filt_w2,
           hy_filt_b2, hy_filt_fr2, hy_filt_w3, hy_bias, hy_proj, cf_dw_w, cf_dw_b, cf_ln_g, cf_ln_b,
           cf_proj, w_out, mix_post_g, ffn_pre_g, ffn_w_gu, ffn_w_down, ffn_post_g):
    bsz, seq, d = x.shape
    depth = w_in.shape[0]
    hw = hy_proj.shape[1]
    cw = cf_proj.shape[1]
    assert hw == cw and d % hw == 0
    row = lambda v: v.reshape(1, -1)

    assert depth == 1
    x2d = x.reshape(bsz * seq, d)
    for l in range(depth):
        ce, co, sef, sof, seg, cog, sog, w_in_b = _dft_tables(seq, w_in[l], 3 * d // 8, d // 8)
        s, dm, tmid, w_in_c = _filt_mlp(seq, hw, hy_filt_w1[l], row(hy_filt_b1[l]), row(hy_filt_fr1[l]),
                                        hy_filt_w2[l], row(hy_filt_b2[l]), row(hy_filt_fr2[l]),
                                        hy_filt_w3[l], w_in[l], d // 2)
        tt, w_in_a = _filt_dft(ce, co, sef, sof, s, dm, w_in[l], 3 * d // 8)
        proj = _in_proj(x2d, row(mix_pre_g[l]), [w_in_a, w_in_b, w_in_c])
        proj3 = proj.reshape(bsz, seq, -1)
        cf_a_col, cf_b_col = 3, 4
        ga_col, gb_col = 5, 5 + d // hw

        hy = _hy_conv3(proj3, hy_conv_w[l], row(hy_conv_b[l]), 3 * hw)
        hy4 = hy.reshape(bsz, 2, seq // 2, 3 * hw)
        fwd, inv = (ce, co, sef, sof), (ce, seg, cog, sog)
        y1 = _hy_fwd(fwd, hy, 0, tt, tmid, 0, hw)
        z4 = _hy_inv(inv, y1, hy4, 1, hy4, 0, hy_bias[l], 0, hw, natural=False)
        y2 = _hy_fwd(fwd, z4.reshape(bsz, seq, hw), 0, tt, tmid, 1, hw)
        y_a = _hy_inv(inv, y2, hy4, 2, z4, 0, hy_bias[l], 1, hw, natural=True)

        y_b, (hy_proj_bf, cf_proj_bf, w_out_bf, w_gu_bf, w_down_bf) = _cf_conv(
            proj3, cf_a_col, cf_b_col, cf_dw_w[l], row(cf_dw_b[l]), row(cf_ln_g[l]), row(cf_ln_b[l]), cw,
            [hy_proj[l], cf_proj[l], w_out[l], ffn_w_gu[l], ffn_w_down[l]])

        x1 = _merge(y_a.reshape(bsz * seq, hw), y_b.reshape(bsz * seq, cw), proj, ga_col, gb_col, x2d,
                    hy_proj_bf, cf_proj_bf, w_out_bf, row(mix_post_g[l]))
        x2d = _ffn(x1, row(ffn_pre_g[l]), w_gu_bf, w_down_bf, row(ffn_post_g[l]))
    return x2d.reshape(bsz, seq, d)
```

```python
import functools
import math

import jax
import jax.numpy as jnp
from jax import lax
from jax.experimental import pallas as pl
from jax.experimental.pallas import tpu as pltpu

F32 = jnp.float32
BF16 = jnp.bfloat16
EPS = 1e-6

HY_ORDER = 2
HY_EMB_BANDS = 16
HY_DECAY_TARGET = 1e-2
HY_MIN_DECAY = math.log(HY_DECAY_TARGET) / 1.5
HY_MAX_DECAY = math.log(HY_DECAY_TARGET) / 0.3
HY_SHORT = 3
CF_KERNEL = 31
CF_PAD = 16
LANES = 128

V7X_VMEM_LIMIT_BYTES = 56 * 1024 * 1024


def _params(*sem):
    return pltpu.CompilerParams(dimension_semantics=sem, vmem_limit_bytes=V7X_VMEM_LIMIT_BYTES)


def _dot(a, b):
    return jnp.dot(a, b, preferred_element_type=F32)


def _sigmoid(x):
    return 1.0 / (1.0 + jnp.exp(-x))


def _resident(shape):
    zeros = (0,) * len(shape)
    return pl.BlockSpec(shape, lambda *_: zeros, pipeline_mode=pl.Buffered(1))


def _parity_split_store(val, scr_ref, dst_ref):
    rows, cols = val.shape
    h = rows // 2
    for sl in range(cols // LANES):
        scr_ref[sl] = val[:, sl * LANES:(sl + 1) * LANES]
    for sl in range(cols // LANES):
        c = slice(sl * LANES, (sl + 1) * LANES)
        dst_ref[0:h, c] = scr_ref[sl, pl.ds(0, h, stride=2), :].astype(dst_ref.dtype)
        dst_ref[h:rows, c] = scr_ref[sl, pl.ds(1, h, stride=2), :].astype(dst_ref.dtype)


def _dft_kernel(ce_ref, co_ref, sef_ref, sof_ref, seg_ref, cog_ref, sog_ref,
                rc_ref, rs_ref, pc_ref, ps_ref, *, tm, h, chunk):
    i = pl.program_id(0)
    mask = 4 * h - 1
    scale = 2.0 * math.pi / (4 * h)

    @pl.when(i == 0)
    def _():
        def base(j, carry):
            r0 = pl.multiple_of(j * chunk, chunk)
            r = r0 + lax.broadcasted_iota(jnp.int32, (chunk, h), 0)
            c = lax.broadcasted_iota(jnp.int32, (chunk, h), 1)
            th_e = ((2 * r * c) & mask).astype(F32) * scale
            th_o = ((r * (2 * c + 1)) & mask).astype(F32) * scale
            rc_ref[pl.ds(r0, chunk), :] = jnp.cos(th_e)
            rs_ref[pl.ds(r0, chunk), :] = jnp.sin(th_e)
            pc_ref[pl.ds(r0, chunk), :] = jnp.cos(th_o)
            ps_ref[pl.ds(r0, chunk), :] = jnp.sin(th_o)
            return carry
        lax.fori_loop(0, tm // chunk, base, 0)

    k0 = i * tm
    c1 = lax.broadcasted_iota(jnp.int32, (1, h), 1)
    th = ((2 * k0 * c1) & mask).astype(F32) * scale
    ce0, se0 = jnp.cos(th), jnp.sin(th)
    th = ((k0 * (2 * c1 + 1)) & mask).astype(F32) * scale
    co0, so0 = jnp.cos(th), jnp.sin(th)
    th = (((2 * k0 + 1) * c1) & mask).astype(F32) * scale
    cg0, sg0 = jnp.cos(th), jnp.sin(th)

    def tile(j, carry):
        r0 = pl.multiple_of(j * chunk, chunk)
        rows = pl.ds(r0, chunk)
        rc, rs, pc, ps = rc_ref[rows, :], rs_ref[rows, :], pc_ref[rows, :], ps_ref[rows, :]
        row = k0 + r0 + lax.broadcasted_iota(jnp.int32, (chunk, h), 0)
        col = lax.broadcasted_iota(jnp.int32, (chunk, h), 1)
        alt_col = (1 - 2 * (col & 1)).astype(F32)
        alt_row = (1 - 2 * (row & 1)).astype(F32)
        se = rs * ce0 + rc * se0
        ce_ref[rows, :] = (rc * ce0 - rs * se0).astype(BF16)
        co_ref[rows, :] = (pc * co0 - ps * so0).astype(BF16)
        sef_ref[rows, :] = jnp.where(row == 0, alt_col, se).astype(BF16)
        sof_ref[rows, :] = jnp.where(row == 0, alt_col, ps * co0 + pc * so0).astype(BF16)
        seg_ref[rows, :] = jnp.where(col == 0, alt_row, se).astype(BF16)
        cog_ref[rows, :] = (rc * cg0 - rs * sg0).astype(BF16)
        sog_ref[rows, :] = jnp.where(col == 0, alt_row, rs * cg0 + rc * sg0).astype(BF16)
        return carry
    lax.fori_loop(0, tm // chunk, tile, 0)


def _dft_tables(seq, tm=256, chunk=32):
    h = seq // 2
    out = jax.ShapeDtypeStruct((h, h), BF16)
    spec = pl.BlockSpec((tm, h), lambda i: (i, 0))
    return pl.pallas_call(
        functools.partial(_dft_kernel, tm=tm, h=h, chunk=chunk),
        grid=(h // tm,),
        out_specs=[spec] * 7,
        out_shape=[out] * 7,
        scratch_shapes=[pltpu.VMEM((tm, h), F32)] * 4,
        compiler_params=_params("arbitrary"),
        name="dft_tables",
    )()


def _in_proj_kernel(x_ref, g_ref, wa_ref, wb_ref, o_ref, h_ref):
    @pl.when(pl.program_id(1) == 0)
    def _():
        x = x_ref[...]
        ms = jnp.mean(x * x, axis=-1, keepdims=True)
        h_ref[...] = (x * lax.rsqrt(ms + EPS) * g_ref[...]).astype(BF16)

    ka = wa_ref.shape[0]
    acc = _dot(h_ref[:, 0:ka], wa_ref[...]) + _dot(h_ref[:, ka:], wb_ref[...])
    o_ref[...] = acc.astype(o_ref.dtype)


def _in_proj(x2d, g, wa, wb, tm=1024, tn=1536):
    m, d = x2d.shape
    n = wa.shape[1]
    assert wa.shape[0] + wb.shape[0] == d
    return pl.pallas_call(
        _in_proj_kernel,
        grid=(m // tm, n // tn),
        in_specs=[pl.BlockSpec((tm, d), lambda i, j: (i, 0)),
                  pl.BlockSpec((1, d), lambda i, j: (0, 0)),
                  pl.BlockSpec((wa.shape[0], tn), lambda i, j: (0, j)),
                  pl.BlockSpec((wb.shape[0], tn), lambda i, j: (0, j))],
        out_specs=pl.BlockSpec((tm, tn), lambda i, j: (i, j)),
        out_shape=jax.ShapeDtypeStruct((m, n), BF16),
        scratch_shapes=[pltpu.VMEM((tm, d), BF16)],
        compiler_params=_params("parallel", "arbitrary"),
        name="in_proj",
    )(x2d, g, wa, wb)


def _conv3_kernel(p_ref, w_ref, b_ref, o_ref, scr_ref):
    rows, cols = p_ref.shape[1], p_ref.shape[2]
    h = rows // 2
    pad = 8
    zeros = jnp.zeros((pad, LANES), F32)
    for sl in range(cols // LANES):
        c = slice(sl * LANES, (sl + 1) * LANES)
        scr_ref[sl, 0:pad, :] = zeros
        scr_ref[sl, pad:pad + rows, :] = p_ref[0, :, c].astype(F32)
        scr_ref[sl, pad + rows:, :] = zeros
    for sl in range(cols // LANES):
        c = slice(sl * LANES, (sl + 1) * LANES)
        w0, w1, w2, b = w_ref[0:1, c], w_ref[1:2, c], w_ref[2:3, c], b_ref[:, c]
        xom = scr_ref[sl, pl.ds(pad - 1, h, stride=2), :]
        xe = scr_ref[sl, pl.ds(pad, h, stride=2), :]
        xo = scr_ref[sl, pl.ds(pad + 1, h, stride=2), :]
        xep = scr_ref[sl, pl.ds(pad + 2, h, stride=2), :]
        o_ref[0, 0:h, c] = (w0 * xom + w1 * xe + w2 * xo + b).astype(o_ref.dtype)
        o_ref[0, h:rows, c] = (w0 * xe + w1 * xo + w2 * xep + b).astype(o_ref.dtype)


def _hy_conv3(proj3, w, b, width, ct=1024):
    bsz, seq, _ = proj3.shape
    return pl.pallas_call(
        _conv3_kernel,
        grid=(bsz, width // ct),
        in_specs=[pl.BlockSpec((1, seq, ct), lambda i, j: (i, 0, j)),
                  pl.BlockSpec((HY_SHORT, ct), lambda i, j: (0, j)),
                  pl.BlockSpec((1, ct), lambda i, j: (0, j))],
        out_specs=pl.BlockSpec((1, seq, ct), lambda i, j: (i, 0, j)),
        out_shape=jax.ShapeDtypeStruct((bsz, seq, width), BF16),
        scratch_shapes=[pltpu.VMEM((ct // LANES, seq + 16, LANES), F32)],
        compiler_params=_params("parallel", "parallel"),
        name="hy_conv3",
    )(proj3, w, b)


def _filt_mlp_kernel(mlp_ref, w3f_ref, w3b_ref, w_ref, s_ref, d_ref, tmid_ref, wbf_ref, h2_ref, scr_ref,
                     *, seq, width, ct):
    jc = pl.program_id(1)
    wbf_ref[...] = w_ref[...].astype(BF16)
    n = lax.broadcasted_iota(jnp.int32, (seq, 1), 0).astype(F32)
    t = n / (seq - 1)

    @pl.when((pl.program_id(0) == 0) & (jc == 0))
    def _():
        nl = lax.broadcasted_iota(jnp.int32, (1, seq), 1).astype(F32)
        tl = nl / (seq - 1)
        wang = 2.0 * math.pi * nl / seq
        band = lax.broadcasted_iota(jnp.int32, (HY_EMB_BANDS, 1), 0).astype(F32)
        f = 1e-4 + band * ((HY_EMB_BANDS - 1 - 1e-4) / (HY_EMB_BANDS - 1))
        fw = f * wang
        p = mlp_ref[...]
        hid = p.shape[0]
        nb = HY_EMB_BANDS
        w1t, w1c, w1s = p[:, 0:1], p[:, 1:1 + nb], p[:, 1 + nb:1 + 2 * nb]
        c0 = 1 + 2 * nb
        b1, fr1, b2, fr2 = p[:, c0:c0 + 1], p[:, c0 + 1:c0 + 2], p[:, c0 + 2:c0 + 3], p[:, c0 + 3:c0 + 4]
        w2t = p[:, c0 + 4:c0 + 4 + hid]
        pre1 = w1t * tl + _dot(w1c, jnp.cos(fw)) - _dot(w1s, jnp.sin(fw)) + b1
        h1 = jnp.sin(fr1 * pre1)
        h2t = jnp.sin(fr2 * (_dot(w2t, h1) + b2))
        h2_ref[...] = h2t.T

    h2 = h2_ref[...]
    ch = (jc * ct + lax.broadcasted_iota(jnp.int32, (1, ct), 1)).astype(F32)
    delta = HY_MIN_DECAY + ch * ((HY_MAX_DECAY - HY_MIN_DECAY) / (width - 1))
    decay = jnp.exp(-t * jnp.abs(delta))
    kf = _dot(h2, w3f_ref[...]) * decay
    kb = _dot(h2, w3b_ref[...]) * decay
    row = lax.broadcasted_iota(jnp.int32, (seq, ct), 0)
    kb = jnp.where(row == 0, 0.0, kb)
    l1 = jnp.sum(jnp.abs(kf), axis=0, keepdims=True) + jnp.sum(jnp.abs(kb), axis=0, keepdims=True)
    inv = 1.0 / l1
    s = (kf + kb) * inv
    dm = (kf - kb) * inv
    phase = row & 3
    cmid = jnp.where(phase == 0, 1.0, jnp.where(phase == 2, -1.0, 0.0))
    smid = jnp.where(phase == 1, 1.0, jnp.where(phase == 3, -1.0, 0.0))
    wmid = 2.0 / (2 * seq)
    tmid_ref[0:1, :] = jnp.sum(s * cmid, axis=0, keepdims=True) * wmid
    tmid_ref[1:2, :] = jnp.sum(dm * smid, axis=0, keepdims=True) * (-wmid)
    _parity_split_store(s, scr_ref, s_ref)
    _parity_split_store(dm, scr_ref, d_ref)


def _pack_filter_mlp(w1, b1, fr1, w2, b2, fr2):
    cols = [w1.T, b1.reshape(-1, 1), fr1.reshape(-1, 1), b2.reshape(-1, 1), fr2.reshape(-1, 1), w2.T]
    packed = jnp.concatenate(cols, axis=1)
    pad = -packed.shape[1] % LANES
    return jnp.pad(packed, ((0, 0), (0, pad)))


def _filt_mlp(seq, width, w1, b1, fr1, w2, b2, fr2, w3, w, w_row0, ct=256):
    hid = w2.shape[0]
    mlp = _pack_filter_mlp(w1, b1, fr1, w2, b2, fr2)
    nct = width // ct
    steps = HY_ORDER * nct
    w_rows = w.shape[0] - w_row0
    wr = w_rows // steps
    assert w_row0 % wr == 0
    small = lambda shape: pl.BlockSpec(shape, lambda o, j: (0, 0))
    out_spec = pl.BlockSpec((seq, ct), lambda o, j: (0, o * nct + j))
    return pl.pallas_call(
        functools.partial(_filt_mlp_kernel, seq=seq, width=width, ct=ct),
        grid=(HY_ORDER, nct),
        in_specs=[small(mlp.shape),
                  pl.BlockSpec((hid, ct), lambda o, j: (0, o * 2 * nct + j)),
                  pl.BlockSpec((hid, ct), lambda o, j: (0, o * 2 * nct + nct + j)),
                  pl.BlockSpec((wr, w.shape[1]), lambda o, j: (w_row0 // wr + o * nct + j, 0))],
        out_specs=[out_spec, out_spec, pl.BlockSpec((2, ct), lambda o, j: (0, o * nct + j)),
                   pl.BlockSpec((wr, w.shape[1]), lambda o, j: (o * nct + j, 0))],
        out_shape=[jax.ShapeDtypeStruct((seq, HY_ORDER * width), BF16),
                   jax.ShapeDtypeStruct((seq, HY_ORDER * width), BF16),
                   jax.ShapeDtypeStruct((2, HY_ORDER * width), F32),
                   jax.ShapeDtypeStruct((w_rows, w.shape[1]), BF16)],
        scratch_shapes=[pltpu.VMEM((seq, hid), F32), pltpu.VMEM((ct // LANES, seq, LANES), F32)],
        compiler_params=_params("arbitrary", "arbitrary"),
        name="filt_mlp",
    )(mlp, w3, w3, w)


def _filt_dft_kernel(ce_ref, co_ref, sef_ref, sof_ref, s_ref, d_ref, w_ref, t_ref, wbf_ref, *, tm, seq):
    h = seq // 2
    wbf_ref[...] = w_ref[...].astype(BF16)
    rows = pl.ds(pl.multiple_of(pl.program_id(1) * tm, tm), tm)
    pc = _dot(ce_ref[rows, :], s_ref[0:h, :])
    qc = _dot(co_ref[rows, :], s_ref[h:seq, :])
    ps = _dot(sef_ref[rows, :], d_ref[0:h, :])
    qs = _dot(sof_ref[rows, :], d_ref[h:seq, :])
    row = pl.program_id(1) * tm + lax.broadcasted_iota(jnp.int32, pc.shape, 0)
    wk = jnp.where(row == 0, 1.0, 2.0) * (1.0 / (2 * seq))
    t_ref[0] = ((pc + qc) * wk).astype(BF16)
    t_ref[1] = jnp.where(row == 0, 0.0, -(ps + qs) * wk).astype(BF16)
    t_ref[2] = ((pc - qc) * wk).astype(BF16)
    t_ref[3] = jnp.where(row == 0, 0.0, (ps - qs) * wk).astype(BF16)


def _filt_dft(ce, co, sef, sof, s, d, w, w_rows, tm=512, tn=512):
    h = ce.shape[0]
    seq, cols = s.shape
    ni = h // tm
    wr = w_rows // (cols // tn * ni)
    tab = pl.BlockSpec((seq, tn), lambda j, i: (0, j))
    wspec = pl.BlockSpec((wr, w.shape[1]), lambda j, i: (j * ni + i, 0))
    return pl.pallas_call(
        functools.partial(_filt_dft_kernel, tm=tm, seq=seq),
        grid=(cols // tn, ni),
        in_specs=[_resident((h, h))] * 4 + [tab, tab, wspec],
        out_specs=[pl.BlockSpec((4, tm, tn), lambda j, i: (0, i, j)), wspec],
        out_shape=[jax.ShapeDtypeStruct((4, h, cols), BF16),
                   jax.ShapeDtypeStruct((w_rows, w.shape[1]), BF16)],
        compiler_params=_params("arbitrary", "arbitrary"),
        name="filt_dft",
    )(ce, co, sef, sof, s, d, w)


def _hy_fwd_kernel(ce_ref, co_ref, sef_ref, sof_ref, u_ref, t_ref, tmid_ref, y_ref, *, tm):
    h = ce_ref.shape[0]
    i = pl.program_id(1)
    rows = pl.ds(pl.multiple_of(i * tm, tm), tm)
    ue = u_ref[0, 0:h, :]
    uo = u_ref[0, h:2 * h, :]
    pc = _dot(ce_ref[rows, :], ue)
    qc = _dot(co_ref[rows, :], uo)
    ps = _dot(sef_ref[rows, :], ue)
    qs = _dot(sof_ref[rows, :], uo)
    r0 = (i * tm + lax.broadcasted_iota(jnp.int32, pc.shape, 0)) == 0
    tr1, ti1 = t_ref[0, rows, :].astype(F32), t_ref[1, rows, :].astype(F32)
    tr2, ti2 = t_ref[2, rows, :].astype(F32), t_ref[3, rows, :].astype(F32)
    a1, a2 = pc + qc, pc - qc
    b1, b2 = ps + qs, qs - ps
    yr1, yi1 = a1 * tr1 + b1 * ti1, b1 * tr1 - a1 * ti1
    yr2, yi2 = a2 * tr2 + b2 * ti2, b2 * tr2 - a2 * ti2
    trh, tih = tmid_ref[0:1, :], tmid_ref[1:2, :]
    y_ref[0, 0] = (yr1 + yr2).astype(BF16)
    y_ref[0, 1] = (yr1 - yr2).astype(BF16)
    y_ref[0, 2] = jnp.where(r0, ps * trh + qs * tih, yi1 - yi2).astype(BF16)
    y_ref[0, 3] = jnp.where(r0, qs * trh - ps * tih, yi1 + yi2).astype(BF16)


def _hy_fwd(mats, u3, u_col, tt, tmid, t_col, width, tm=512):
    bsz, seq, _ = u3.shape
    h = seq // 2
    return pl.pallas_call(
        functools.partial(_hy_fwd_kernel, tm=tm),
        grid=(bsz, h // tm),
        in_specs=[_resident((h, h))] * 4
                 + [pl.BlockSpec((1, seq, width), lambda b, i: (b, 0, u_col)),
                    pl.BlockSpec((4, h, width), lambda b, i: (0, 0, t_col), pipeline_mode=pl.Buffered(1)),
                    pl.BlockSpec((2, width), lambda b, i: (0, t_col))],
        out_specs=pl.BlockSpec((1, 4, tm, width), lambda b, i: (b, 0, i, 0)),
        out_shape=jax.ShapeDtypeStruct((bsz, 4, h, width), BF16),
        compiler_params=_params("arbitrary", "arbitrary"),
        name="hy_fwd",
    )(*mats, u3, tt, tmid)


def _hy_inv_kernel(ce_ref, seg_ref, cog_ref, sog_ref, y_ref, g_ref, u_ref, bias_ref, o_ref, *scr,
                   tm, natural, order):
    i = pl.program_id(1)
    rows = pl.ds(pl.multiple_of(i * tm, tm), tm)
    ye = _dot(ce_ref[rows, :], y_ref[0, 0]) + _dot(seg_ref[rows, :], y_ref[0, 2])
    yo = _dot(cog_ref[rows, :], y_ref[0, 1]) + _dot(sog_ref[rows, :], y_ref[0, 3])
    bias = bias_ref[order:order + 1, :]
    ze = g_ref[0, 0].astype(F32) * (ye + u_ref[0, 0].astype(F32) * bias)
    zo = g_ref[0, 1].astype(F32) * (yo + u_ref[0, 1].astype(F32) * bias)
    if natural:
        scr_ref, = scr
        for sl in range(ze.shape[1] // LANES):
            c = slice(sl * LANES, (sl + 1) * LANES)
            scr_ref[sl, pl.ds(0, tm, stride=2), :] = ze[:, c]
            scr_ref[sl, pl.ds(1, tm, stride=2), :] = zo[:, c]
        for sl in range(ze.shape[1] // LANES):
            o_ref[0, :, sl * LANES:(sl + 1) * LANES] = scr_ref[sl].astype(o_ref.dtype)
    else:
        o_ref[0, 0] = ze.astype(o_ref.dtype)
        o_ref[0, 1] = zo.astype(o_ref.dtype)


def _hy_inv(mats, y4, g4, g_col, u4, u_col, bias, order, width, natural, tm=512):
    bsz, _, h, _ = y4.shape
    par = lambda col: pl.BlockSpec((1, 2, tm, width), lambda b, i: (b, 0, i, col))
    if natural:
        out_spec = pl.BlockSpec((1, 2 * tm, width), lambda b, i: (b, i, 0))
        out_shape = jax.ShapeDtypeStruct((bsz, 2 * h, width), BF16)
        scratch = [pltpu.VMEM((width // LANES, 2 * tm, LANES), F32)]
    else:
        out_spec = par(0)
        out_shape = jax.ShapeDtypeStruct((bsz, 2, h, width), BF16)
        scratch = []
    return pl.pallas_call(
        functools.partial(_hy_inv_kernel, tm=tm, natural=natural, order=order),
        grid=(bsz, h // tm),
        in_specs=[_resident((h, h))] * 4
                 + [pl.BlockSpec((1, 4, h, width), lambda b, i: (b, 0, 0, 0)),
                    par(g_col), par(u_col),
                    pl.BlockSpec(bias.shape, lambda b, i: (0, 0))],
        out_specs=out_spec,
        out_shape=out_shape,
        scratch_shapes=scratch,
        compiler_params=_params("arbitrary", "arbitrary"),
        name="hy_inv",
    )(*mats, y4, g4, u4, bias)


def _cf_kernel(*refs, step_rows, n_cast):
    (am_ref, ap_ref, an_ref, bm_ref, bp_ref, bn_ref, w_ref, cb_ref, lg_ref, lb_ref) = refs[:10]
    cast_in = refs[10:10 + n_cast]
    o_ref = refs[10 + n_cast]
    cast_out = refs[11 + n_cast:11 + 2 * n_cast]
    us_ref, cs_ref, wb_ref = refs[11 + 2 * n_cast:]
    width = wb_ref.shape[-1]
    nsl = width // LANES
    j = pl.program_id(1)

    for src, dst in zip(cast_in, cast_out):
        dst[...] = src[...].astype(dst.dtype)

    @pl.when((pl.program_id(0) == 0) & (j == 0))
    def _():
        for t in range(CF_KERNEL):
            wb_ref[t] = jnp.broadcast_to(w_ref[t:t + 1, :], (8, width))

    glu = lambda a, b: a[0].astype(F32) * _sigmoid(b[0].astype(F32))
    u_prev = jnp.where(j == 0, 0.0, glu(ap_ref, bp_ref))
    u_main = glu(am_ref, bm_ref)
    u_next = jnp.where(j == pl.num_programs(1) - 1, 0.0, glu(an_ref, bn_ref))
    for sl in range(nsl):
        c = slice(sl * LANES, (sl + 1) * LANES)
        us_ref[sl, 0:CF_PAD, :] = u_prev[:, c]
        us_ref[sl, CF_PAD:CF_PAD + step_rows, :] = u_main[:, c]
        us_ref[sl, CF_PAD + step_rows:, :] = u_next[:, c]

    half = CF_KERNEL // 2
    nph = 4
    prow = step_rows // nph
    for sl in range(nsl):
        c = slice(sl * LANES, (sl + 1) * LANES)
        accs = [jnp.zeros((prow // 8, 8, LANES), F32) + cb_ref[:, c][None]] * nph
        for r in range(CF_KERNEL + nph - 1):
            tap = us_ref[sl, pl.ds(CF_PAD - half + r, prow, stride=nph), :].reshape(prow // 8, 8, LANES)
            for p in range(nph):
                if 0 <= r - p < CF_KERNEL:
                    accs[p] = accs[p] + wb_ref[r - p, :, c][None] * tap
        for p in range(nph):
            cs_ref[sl, pl.ds(p, prow, stride=nph), :] = accs[p].reshape(prow, LANES)

    tot = cs_ref[0]
    for sl in range(1, nsl):
        tot = tot + cs_ref[sl]
    mu = jnp.sum(tot, axis=-1, keepdims=True) * (1.0 / width)
    sq = None
    for sl in range(nsl):
        cen = cs_ref[sl] - mu
        sq = cen * cen if sq is None else sq + cen * cen
    rstd = lax.rsqrt(jnp.sum(sq, axis=-1, keepdims=True) * (1.0 / width) + EPS)
    for sl in range(nsl):
        c = slice(sl * LANES, (sl + 1) * LANES)
        y = (cs_ref[sl] - mu) * rstd * lg_ref[:, c] + lb_ref[:, c]
        o_ref[0, :, c] = (y * _sigmoid(y)).astype(o_ref.dtype)


def _cf_conv(proj3, a_col, b_col, w, cb, lg, lb, width, casts, step_rows=256):
    bsz, seq, _ = proj3.shape
    nsteps = seq // step_rows
    total = bsz * nsteps
    nblk = step_rows // CF_PAD
    last_blk = seq // CF_PAD - 1
    vec = pl.BlockSpec((1, width), lambda b, j: (0, 0))
    main = lambda col: pl.BlockSpec((1, step_rows, width), lambda b, j: (b, j, col))
    prev = lambda col: pl.BlockSpec((1, CF_PAD, width), lambda b, j: (b, jnp.maximum(j * nblk - 1, 0), col))
    nxt = lambda col: pl.BlockSpec((1, CF_PAD, width),
                                   lambda b, j: (b, jnp.minimum((j + 1) * nblk, last_blk), col))

    def cast_spec(c):
        n = next(n for n in (total, total // 2, total // 4) if c.shape[0] % (16 * n) == 0)
        every = total // n
        return pl.BlockSpec((c.shape[0] // n, c.shape[1]), lambda b, j: ((b * nsteps + j) // every, 0))
    cast_specs = [cast_spec(c) for c in casts]
    outs = pl.pallas_call(
        functools.partial(_cf_kernel, step_rows=step_rows, n_cast=len(casts)),
        grid=(bsz, nsteps),
        in_specs=[main(a_col), prev(a_col), nxt(a_col), main(b_col), prev(b_col), nxt(b_col),
                  pl.BlockSpec((CF_KERNEL, width), lambda b, j: (0, 0)),
                  vec, vec, vec] + cast_specs,
        out_specs=[pl.BlockSpec((1, step_rows, width), lambda b, j: (b, j, 0))] + cast_specs,
        out_shape=[jax.ShapeDtypeStruct((bsz, seq, width), BF16)]
                  + [jax.ShapeDtypeStruct(c.shape, BF16) for c in casts],
        scratch_shapes=[pltpu.VMEM((width // LANES, step_rows + 2 * CF_PAD, LANES), F32),
                        pltpu.VMEM((width // LANES, step_rows, LANES), F32),
                        pltpu.VMEM((CF_KERNEL, 8, width), F32)],
        compiler_params=_params("arbitrary", "arbitrary"),
        name="cf_conv",
    )(proj3, proj3, proj3, proj3, proj3, proj3, w, cb, lg, lb, *casts)
    return outs[0], outs[1:]


def _merge_kernel(ya_ref, yb_ref, ga0_ref, ga1_ref, gb0_ref, gb1_ref, x_ref, pa_ref, pb_ref, wo_ref,
                  g1_ref, x1_ref):
    w = ga0_ref.shape[1]
    ya, yb = ya_ref[...], yb_ref[...]
    m0 = (_sigmoid(ga0_ref[...].astype(F32)) * _dot(ya, pa_ref[:, 0:w])
          + _sigmoid(gb0_ref[...].astype(F32)) * _dot(yb, pb_ref[:, 0:w])).astype(BF16)
    m1 = (_sigmoid(ga1_ref[...].astype(F32)) * _dot(ya, pa_ref[:, w:2 * w])
          + _sigmoid(gb1_ref[...].astype(F32)) * _dot(yb, pb_ref[:, w:2 * w])).astype(BF16)
    o = _dot(m0, wo_ref[0:w, :]) + _dot(m1, wo_ref[w:2 * w, :])
    x1_ref[...] = x_ref[...] + o * lax.rsqrt(jnp.mean(o * o, axis=-1, keepdims=True) + EPS) * g1_ref[...]


def _merge(ya, yb, proj, ga_col, gb_col, x2d, pa, pb, wo, g1, tm=512):
    m, d = x2d.shape
    wa = ya.shape[1]
    assert d == 2 * wa
    gate = lambda col: pl.BlockSpec((tm, wa), lambda i: (i, col))
    return pl.pallas_call(
        _merge_kernel,
        grid=(m // tm,),
        in_specs=[pl.BlockSpec((tm, wa), lambda i: (i, 0)),
                  pl.BlockSpec((tm, wa), lambda i: (i, 0)),
                  gate(ga_col), gate(ga_col + 1), gate(gb_col), gate(gb_col + 1),
                  pl.BlockSpec((tm, d), lambda i: (i, 0)),
                  _resident((wa, d)), _resident((wa, d)), _resident((d, d)), _resident((1, d))],
        out_specs=pl.BlockSpec((tm, d), lambda i: (i, 0)),
        out_shape=jax.ShapeDtypeStruct((m, d), F32),
        compiler_params=_params("parallel"),
        name="merge",
    )(ya, yb, proj, proj, proj, proj, x2d, pa, pb, wo, g1)


def _ffn_kernel(x1_ref, gpre_ref, wg_ref, wu_ref, wd_ref, gpost_ref, o_ref, h_ref):
    j = pl.program_id(1)

    @pl.when(j == 0)
    def _():
        x1 = x1_ref[...]
        ms = jnp.mean(x1 * x1, axis=-1, keepdims=True)
        h_ref[...] = (x1 * lax.rsqrt(ms + EPS) * gpre_ref[...]).astype(BF16)
        o_ref[...] = jnp.zeros_like(o_ref)

    h = h_ref[...]
    gate = _dot(h, wg_ref[...])
    up = _dot(h, wu_ref[...])
    act = (gate * _sigmoid(gate) * up).astype(BF16)
    o_ref[...] += _dot(act, wd_ref[...])

    @pl.when(j == pl.num_programs(1) - 1)
    def _():
        a = o_ref[...]
        o_ref[...] = x1_ref[...] + a * lax.rsqrt(jnp.mean(a * a, axis=-1, keepdims=True) + EPS) * gpost_ref[...]


def _ffn(x1, gpre, wgu, wd, gpost, tm=1024, th=256):
    m, d = x1.shape
    hidden = wd.shape[0]
    nh = hidden // th
    return pl.pallas_call(
        _ffn_kernel,
        grid=(m // tm, nh),
        in_specs=[pl.BlockSpec((tm, d), lambda i, j: (i, 0)),
                  pl.BlockSpec((1, d), lambda i, j: (0, 0)),
                  pl.BlockSpec((d, th), lambda i, j: (0, j)),
                  pl.BlockSpec((d, th), lambda i, j: (0, nh + j)),
                  pl.BlockSpec((th, d), lambda i, j: (j, 0)),
                  pl.BlockSpec((1, d), lambda i, j: (0, 0))],
        out_specs=pl.BlockSpec((tm, d), lambda i, j: (i, 0)),
        out_shape=jax.ShapeDtypeStruct((m, d), F32),
        scratch_shapes=[pltpu.VMEM((tm, d), BF16)],
        compiler_params=_params("parallel", "arbitrary"),
        name="ffn",
    )(x1, gpre, wgu, wgu, wd, gpost)


def kernel(x, mix_pre_g, w_in, hy_conv_w, hy_conv_b, hy_filt_w1, hy_filt_b1, hy_filt_fr1, hy_filt_w2,
           hy_filt_b2, hy_filt_fr2, hy_filt_w3, hy_bias, hy_proj, cf_dw_w, cf_dw_b, cf_ln_g, cf_ln_b,
           cf_proj, w_out, mix_post_g, ffn_pre_g, ffn_w_gu, ffn_w_down, ffn_post_g):
    bsz, seq, d = x.shape
    depth = w_in.shape[0]
    hw = hy_proj.shape[1]
    cw = cf_proj.shape[1]
    assert hw == cw and d % hw == 0
    row = lambda v: v.reshape(1, -1)

    assert depth == 1
    x2d = x.reshape(bsz * seq, d)
    for l in range(depth):
        ce, co, sef, sof, seg, cog, sog = _dft_tables(seq)
        s, dm, tmid, w_in_b = _filt_mlp(seq, hw, hy_filt_w1[l], row(hy_filt_b1[l]), row(hy_filt_fr1[l]),
                                        hy_filt_w2[l], row(hy_filt_b2[l]), row(hy_filt_fr2[l]),
                                        hy_filt_w3[l], w_in[l], d // 2)
        tt, w_in_a = _filt_dft(ce, co, sef, sof, s, dm, w_in[l], d // 2)
        proj = _in_proj(x2d, row(mix_pre_g[l]), w_in_a, w_in_b)
        proj3 = proj.reshape(bsz, seq, -1)
        cf_a_col, cf_b_col = 3, 4
        ga_col, gb_col = 5, 5 + d // hw

        hy = _hy_conv3(proj3, hy_conv_w[l], row(hy_conv_b[l]), 3 * hw)
        hy4 = hy.reshape(bsz, 2, seq // 2, 3 * hw)
        fwd, inv = (ce, co, sef, sof), (ce, seg, cog, sog)
        y1 = _hy_fwd(fwd, hy, 0, tt, tmid, 0, hw)
        z4 = _hy_inv(inv, y1, hy4, 1, hy4, 0, hy_bias[l], 0, hw, natural=False)
        y2 = _hy_fwd(fwd, z4.reshape(bsz, seq, hw), 0, tt, tmid, 1, hw)
        y_a = _hy_inv(inv, y2, hy4, 2, z4, 0, hy_bias[l], 1, hw, natural=True)

        y_b, (hy_proj_bf, cf_proj_bf, w_out_bf, w_gu_bf, w_down_bf) = _cf_conv(
            proj3, cf_a_col, cf_b_col, cf_dw_w[l], row(cf_dw_b[l]), row(cf_ln_g[l]), row(cf_ln_b[l]), cw,
            [hy_proj[l], cf_proj[l], w_out[l], ffn_w_gu[l], ffn_w_down[l]])

        x1 = _merge(y_a.reshape(bsz * seq, hw), y_b.reshape(bsz * seq, cw), proj, ga_col, gb_col, x2d,
                    hy_proj_bf, cf_proj_bf, w_out_bf, row(mix_post_g[l]))
        x2d = _ffn(x1, row(ffn_pre_g[l]), w_gu_bf, w_down_bf, row(ffn_post_g[l]))
    return x2d.reshape(bsz, seq, d)
```

```python
import functools
import math

import jax
import jax.numpy as jnp
from jax import lax
from jax.experimental import pallas as pl
from jax.experimental.pallas import tpu as pltpu

F32 = jnp.float32
BF16 = jnp.bfloat16
EPS = 1e-6

HY_ORDER = 2
HY_EMB_BANDS = 16
HY_DECAY_TARGET = 1e-2
HY_MIN_DECAY = math.log(HY_DECAY_TARGET) / 1.5
HY_MAX_DECAY = math.log(HY_DECAY_TARGET) / 0.3
HY_SHORT = 3
CF_KERNEL = 31
CF_PAD = 16
LANES = 128

V7X_VMEM_LIMIT_BYTES = 56 * 1024 * 1024
V7X_VMEM_LIMIT_FFN_BYTES = 63 * 1024 * 1024


def _params(*sem, vmem_limit_bytes=V7X_VMEM_LIMIT_BYTES):
    return pltpu.CompilerParams(dimension_semantics=sem, vmem_limit_bytes=vmem_limit_bytes)


def _dot(a, b):
    return jnp.dot(a, b, preferred_element_type=F32)


def _sigmoid(x):
    return 1.0 / (1.0 + jnp.exp(-x))


def _resident(shape):
    zeros = (0,) * len(shape)
    return pl.BlockSpec(shape, lambda *_: zeros, pipeline_mode=pl.Buffered(1))


def _parity_split_store(val, scr_ref, dst_ref):
    rows, cols = val.shape
    h = rows // 2
    for sl in range(cols // LANES):
        scr_ref[sl] = val[:, sl * LANES:(sl + 1) * LANES]
    for sl in range(cols // LANES):
        c = slice(sl * LANES, (sl + 1) * LANES)
        dst_ref[0:h, c] = scr_ref[sl, pl.ds(0, h, stride=2), :].astype(dst_ref.dtype)
        dst_ref[h:rows, c] = scr_ref[sl, pl.ds(1, h, stride=2), :].astype(dst_ref.dtype)


def _dft_kernel(ce_ref, co_ref, sef_ref, sof_ref, seg_ref, cog_ref, sog_ref,
                rc_ref, rs_ref, pc_ref, ps_ref, *, tm, h, chunk):
    i = pl.program_id(0)
    mask = 4 * h - 1
    scale = 2.0 * math.pi / (4 * h)

    @pl.when(i == 0)
    def _():
        def base(j, carry):
            r0 = pl.multiple_of(j * chunk, chunk)
            r = r0 + lax.broadcasted_iota(jnp.int32, (chunk, h), 0)
            c = lax.broadcasted_iota(jnp.int32, (chunk, h), 1)
            th_e = ((2 * r * c) & mask).astype(F32) * scale
            th_o = ((r * (2 * c + 1)) & mask).astype(F32) * scale
            rc_ref[pl.ds(r0, chunk), :] = jnp.cos(th_e)
            rs_ref[pl.ds(r0, chunk), :] = jnp.sin(th_e)
            pc_ref[pl.ds(r0, chunk), :] = jnp.cos(th_o)
            ps_ref[pl.ds(r0, chunk), :] = jnp.sin(th_o)
            return carry
        lax.fori_loop(0, tm // chunk, base, 0)

    k0 = i * tm
    c1 = lax.broadcasted_iota(jnp.int32, (1, h), 1)
    th = ((2 * k0 * c1) & mask).astype(F32) * scale
    ce0, se0 = jnp.cos(th), jnp.sin(th)
    th = ((k0 * (2 * c1 + 1)) & mask).astype(F32) * scale
    co0, so0 = jnp.cos(th), jnp.sin(th)
    th = (((2 * k0 + 1) * c1) & mask).astype(F32) * scale
    cg0, sg0 = jnp.cos(th), jnp.sin(th)

    def tile(j, carry):
        r0 = pl.multiple_of(j * chunk, chunk)
        rows = pl.ds(r0, chunk)
        rc, rs, pc, ps = rc_ref[rows, :], rs_ref[rows, :], pc_ref[rows, :], ps_ref[rows, :]
        row = k0 + r0 + lax.broadcasted_iota(jnp.int32, (chunk, h), 0)
        col = lax.broadcasted_iota(jnp.int32, (chunk, h), 1)
        alt_col = (1 - 2 * (col & 1)).astype(F32)
        alt_row = (1 - 2 * (row & 1)).astype(F32)
        se = rs * ce0 + rc * se0
        ce_ref[rows, :] = (rc * ce0 - rs * se0).astype(BF16)
        co_ref[rows, :] = (pc * co0 - ps * so0).astype(BF16)
        sef_ref[rows, :] = jnp.where(row == 0, alt_col, se).astype(BF16)
        sof_ref[rows, :] = jnp.where(row == 0, alt_col, ps * co0 + pc * so0).astype(BF16)
        seg_ref[rows, :] = jnp.where(col == 0, alt_row, se).astype(BF16)
        cog_ref[rows, :] = (rc * cg0 - rs * sg0).astype(BF16)
        sog_ref[rows, :] = jnp.where(col == 0, alt_row, rs * cg0 + rc * sg0).astype(BF16)
        return carry
    lax.fori_loop(0, tm // chunk, tile, 0)


def _dft_tables(seq, tm=256, chunk=32):
    h = seq // 2
    out = jax.ShapeDtypeStruct((h, h), BF16)
    spec = pl.BlockSpec((tm, h), lambda i: (i, 0))
    return pl.pallas_call(
        functools.partial(_dft_kernel, tm=tm, h=h, chunk=chunk),
        grid=(h // tm,),
        out_specs=[spec] * 7,
        out_shape=[out] * 7,
        scratch_shapes=[pltpu.VMEM((tm, h), F32)] * 4,
        compiler_params=_params("arbitrary"),
        name="dft_tables",
    )()


def _in_proj_kernel(x_ref, g_ref, wa_ref, wb_ref, o_ref, h_ref):
    @pl.when(pl.program_id(1) == 0)
    def _():
        x = x_ref[...]
        ms = jnp.mean(x * x, axis=-1, keepdims=True)
        h_ref[...] = (x * lax.rsqrt(ms + EPS) * g_ref[...]).astype(BF16)

    ka = wa_ref.shape[0]
    acc = _dot(h_ref[:, 0:ka], wa_ref[...]) + _dot(h_ref[:, ka:], wb_ref[...])
    o_ref[...] = acc.astype(o_ref.dtype)


def _in_proj(x2d, g, wa, wb, tm=1024, tn=1536):
    m, d = x2d.shape
    n = wa.shape[1]
    assert wa.shape[0] + wb.shape[0] == d
    return pl.pallas_call(
        _in_proj_kernel,
        grid=(m // tm, n // tn),
        in_specs=[pl.BlockSpec((tm, d), lambda i, j: (i, 0)),
                  pl.BlockSpec((1, d), lambda i, j: (0, 0)),
                  pl.BlockSpec((wa.shape[0], tn), lambda i, j: (0, j)),
                  pl.BlockSpec((wb.shape[0], tn), lambda i, j: (0, j))],
        out_specs=pl.BlockSpec((tm, tn), lambda i, j: (i, j)),
        out_shape=jax.ShapeDtypeStruct((m, n), BF16),
        scratch_shapes=[pltpu.VMEM((tm, d), BF16)],
        compiler_params=_params("parallel", "arbitrary"),
        name="in_proj",
    )(x2d, g, wa, wb)


def _conv3_kernel(p_ref, w_ref, b_ref, o_ref, scr_ref):
    rows, cols = p_ref.shape[1], p_ref.shape[2]
    h = rows // 2
    pad = 8
    zeros = jnp.zeros((pad, LANES), F32)
    for sl in range(cols // LANES):
        c = slice(sl * LANES, (sl + 1) * LANES)
        scr_ref[sl, 0:pad, :] = zeros
        scr_ref[sl, pad:pad + rows, :] = p_ref[0, :, c].astype(F32)
        scr_ref[sl, pad + rows:, :] = zeros
    for sl in range(cols // LANES):
        c = slice(sl * LANES, (sl + 1) * LANES)
        w0, w1, w2, b = w_ref[0:1, c], w_ref[1:2, c], w_ref[2:3, c], b_ref[:, c]
        xom = scr_ref[sl, pl.ds(pad - 1, h, stride=2), :]
        xe = scr_ref[sl, pl.ds(pad, h, stride=2), :]
        xo = scr_ref[sl, pl.ds(pad + 1, h, stride=2), :]
        xep = scr_ref[sl, pl.ds(pad + 2, h, stride=2), :]
        o_ref[0, 0:h, c] = (w0 * xom + w1 * xe + w2 * xo + b).astype(o_ref.dtype)
        o_ref[0, h:rows, c] = (w0 * xe + w1 * xo + w2 * xep + b).astype(o_ref.dtype)


def _hy_conv3(proj3, w, b, width, ct=1024):
    bsz, seq, _ = proj3.shape
    return pl.pallas_call(
        _conv3_kernel,
        grid=(bsz, width // ct),
        in_specs=[pl.BlockSpec((1, seq, ct), lambda i, j: (i, 0, j)),
                  pl.BlockSpec((HY_SHORT, ct), lambda i, j: (0, j)),
                  pl.BlockSpec((1, ct), lambda i, j: (0, j))],
        out_specs=pl.BlockSpec((1, seq, ct), lambda i, j: (i, 0, j)),
        out_shape=jax.ShapeDtypeStruct((bsz, seq, width), BF16),
        scratch_shapes=[pltpu.VMEM((ct // LANES, seq + 16, LANES), F32)],
        compiler_params=_params("parallel", "parallel"),
        name="hy_conv3",
    )(proj3, w, b)


def _filt_mlp_kernel(mlp_ref, w3f_ref, w3b_ref, w_ref, s_ref, d_ref, tmid_ref, wbf_ref, h2_ref, scr_ref,
                     *, seq, width, ct):
    jc = pl.program_id(1)
    wbf_ref[...] = w_ref[...].astype(BF16)
    n = lax.broadcasted_iota(jnp.int32, (seq, 1), 0).astype(F32)
    t = n / (seq - 1)

    @pl.when((pl.program_id(0) == 0) & (jc == 0))
    def _():
        nl = lax.broadcasted_iota(jnp.int32, (1, seq), 1).astype(F32)
        tl = nl / (seq - 1)
        wang = 2.0 * math.pi * nl / seq
        band = lax.broadcasted_iota(jnp.int32, (HY_EMB_BANDS, 1), 0).astype(F32)
        f = 1e-4 + band * ((HY_EMB_BANDS - 1 - 1e-4) / (HY_EMB_BANDS - 1))
        fw = f * wang
        p = mlp_ref[...]
        hid = p.shape[0]
        nb = HY_EMB_BANDS
        w1t, w1c, w1s = p[:, 0:1], p[:, 1:1 + nb], p[:, 1 + nb:1 + 2 * nb]
        c0 = 1 + 2 * nb
        b1, fr1, b2, fr2 = p[:, c0:c0 + 1], p[:, c0 + 1:c0 + 2], p[:, c0 + 2:c0 + 3], p[:, c0 + 3:c0 + 4]
        w2t = p[:, c0 + 4:c0 + 4 + hid]
        pre1 = w1t * tl + _dot(w1c, jnp.cos(fw)) - _dot(w1s, jnp.sin(fw)) + b1
        h1 = jnp.sin(fr1 * pre1)
        h2t = jnp.sin(fr2 * (_dot(w2t, h1) + b2))
        h2_ref[...] = h2t.T

    h2 = h2_ref[...]
    ch = (jc * ct + lax.broadcasted_iota(jnp.int32, (1, ct), 1)).astype(F32)
    delta = HY_MIN_DECAY + ch * ((HY_MAX_DECAY - HY_MIN_DECAY) / (width - 1))
    decay = jnp.exp(-t * jnp.abs(delta))
    kf = _dot(h2, w3f_ref[...]) * decay
    kb = _dot(h2, w3b_ref[...]) * decay
    row = lax.broadcasted_iota(jnp.int32, (seq, ct), 0)
    kb = jnp.where(row == 0, 0.0, kb)
    l1 = jnp.sum(jnp.abs(kf), axis=0, keepdims=True) + jnp.sum(jnp.abs(kb), axis=0, keepdims=True)
    inv = 1.0 / l1
    s = (kf + kb) * inv
    dm = (kf - kb) * inv
    phase = row & 3
    cmid = jnp.where(phase == 0, 1.0, jnp.where(phase == 2, -1.0, 0.0))
    smid = jnp.where(phase == 1, 1.0, jnp.where(phase == 3, -1.0, 0.0))
    wmid = 2.0 / (2 * seq)
    tmid_ref[0:1, :] = jnp.sum(s * cmid, axis=0, keepdims=True) * wmid
    tmid_ref[1:2, :] = jnp.sum(dm * smid, axis=0, keepdims=True) * (-wmid)
    _parity_split_store(s, scr_ref, s_ref)
    _parity_split_store(dm, scr_ref, d_ref)


def _pack_filter_mlp(w1, b1, fr1, w2, b2, fr2):
    cols = [w1.T, b1.reshape(-1, 1), fr1.reshape(-1, 1), b2.reshape(-1, 1), fr2.reshape(-1, 1), w2.T]
    packed = jnp.concatenate(cols, axis=1)
    pad = -packed.shape[1] % LANES
    return jnp.pad(packed, ((0, 0), (0, pad)))


def _filt_mlp(seq, width, w1, b1, fr1, w2, b2, fr2, w3, w, w_row0, ct=256):
    hid = w2.shape[0]
    mlp = _pack_filter_mlp(w1, b1, fr1, w2, b2, fr2)
    nct = width // ct
    steps = HY_ORDER * nct
    w_rows = w.shape[0] - w_row0
    wr = w_rows // steps
    assert w_row0 % wr == 0
    small = lambda shape: pl.BlockSpec(shape, lambda o, j: (0, 0))
    out_spec = pl.BlockSpec((seq, ct), lambda o, j: (0, o * nct + j))
    return pl.pallas_call(
        functools.partial(_filt_mlp_kernel, seq=seq, width=width, ct=ct),
        grid=(HY_ORDER, nct),
        in_specs=[small(mlp.shape),
                  pl.BlockSpec((hid, ct), lambda o, j: (0, o * 2 * nct + j)),
                  pl.BlockSpec((hid, ct), lambda o, j: (0, o * 2 * nct + nct + j)),
                  pl.BlockSpec((wr, w.shape[1]), lambda o, j: (w_row0 // wr + o * nct + j, 0))],
        out_specs=[out_spec, out_spec, pl.BlockSpec((2, ct), lambda o, j: (0, o * nct + j)),
                   pl.BlockSpec((wr, w.shape[1]), lambda o, j: (o * nct + j, 0))],
        out_shape=[jax.ShapeDtypeStruct((seq, HY_ORDER * width), BF16),
                   jax.ShapeDtypeStruct((seq, HY_ORDER * width), BF16),
                   jax.ShapeDtypeStruct((2, HY_ORDER * width), F32),
                   jax.ShapeDtypeStruct((w_rows, w.shape[1]), BF16)],
        scratch_shapes=[pltpu.VMEM((seq, hid), F32), pltpu.VMEM((ct // LANES, seq, LANES), F32)],
        compiler_params=_params("arbitrary", "arbitrary"),
        name="filt_mlp",
    )(mlp, w3, w3, w)


def _filt_dft_kernel(ce_ref, co_ref, sef_ref, sof_ref, s_ref, d_ref, w_ref, t_ref, wbf_ref, *, tm, seq):
    h = seq // 2
    wbf_ref[...] = w_ref[...].astype(BF16)
    rows = pl.ds(pl.multiple_of(pl.program_id(1) * tm, tm), tm)
    pc = _dot(ce_ref[rows, :], s_ref[0:h, :])
    qc = _dot(co_ref[rows, :], s_ref[h:seq, :])
    ps = _dot(sef_ref[rows, :], d_ref[0:h, :])
    qs = _dot(sof_ref[rows, :], d_ref[h:seq, :])
    row = pl.program_id(1) * tm + lax.broadcasted_iota(jnp.int32, pc.shape, 0)
    wk = jnp.where(row == 0, 1.0, 2.0) * (1.0 / (2 * seq))
    t_ref[0] = ((pc + qc) * wk).astype(BF16)
    t_ref[1] = jnp.where(row == 0, 0.0, -(ps + qs) * wk).astype(BF16)
    t_ref[2] = ((pc - qc) * wk).astype(BF16)
    t_ref[3] = jnp.where(row == 0, 0.0, (ps - qs) * wk).astype(BF16)


def _filt_dft(ce, co, sef, sof, s, d, w, w_rows, tm=512, tn=512):
    h = ce.shape[0]
    seq, cols = s.shape
    ni = h // tm
    wr = w_rows // (cols // tn * ni)
    tab = pl.BlockSpec((seq, tn), lambda j, i: (0, j))
    wspec = pl.BlockSpec((wr, w.shape[1]), lambda j, i: (j * ni + i, 0))
    return pl.pallas_call(
        functools.partial(_filt_dft_kernel, tm=tm, seq=seq),
        grid=(cols // tn, ni),
        in_specs=[_resident((h, h))] * 4 + [tab, tab, wspec],
        out_specs=[pl.BlockSpec((4, tm, tn), lambda j, i: (0, i, j)), wspec],
        out_shape=[jax.ShapeDtypeStruct((4, h, cols), BF16),
                   jax.ShapeDtypeStruct((w_rows, w.shape[1]), BF16)],
        compiler_params=_params("arbitrary", "arbitrary"),
        name="filt_dft",
    )(ce, co, sef, sof, s, d, w)


def _hy_fwd_kernel(ce_ref, co_ref, sef_ref, sof_ref, u_ref, t_ref, tmid_ref, y_ref, *, tm):
    h = ce_ref.shape[0]
    i = pl.program_id(1)
    rows = pl.ds(pl.multiple_of(i * tm, tm), tm)
    ue = u_ref[0, 0:h, :]
    uo = u_ref[0, h:2 * h, :]
    pc = _dot(ce_ref[rows, :], ue)
    qc = _dot(co_ref[rows, :], uo)
    ps = _dot(sef_ref[rows, :], ue)
    qs = _dot(sof_ref[rows, :], uo)
    r0 = (i * tm + lax.broadcasted_iota(jnp.int32, pc.shape, 0)) == 0
    tr1, ti1 = t_ref[0, rows, :].astype(F32), t_ref[1, rows, :].astype(F32)
    tr2, ti2 = t_ref[2, rows, :].astype(F32), t_ref[3, rows, :].astype(F32)
    a1, a2 = pc + qc, pc - qc
    b1, b2 = ps + qs, qs - ps
    yr1, yi1 = a1 * tr1 + b1 * ti1, b1 * tr1 - a1 * ti1
    yr2, yi2 = a2 * tr2 + b2 * ti2, b2 * tr2 - a2 * ti2
    trh, tih = tmid_ref[0:1, :], tmid_ref[1:2, :]
    y_ref[0, 0] = (yr1 + yr2).astype(BF16)
    y_ref[0, 1] = (yr1 - yr2).astype(BF16)
    y_ref[0, 2] = jnp.where(r0, ps * trh + qs * tih, yi1 - yi2).astype(BF16)
    y_ref[0, 3] = jnp.where(r0, qs * trh - ps * tih, yi1 + yi2).astype(BF16)


def _hy_fwd(mats, u3, u_col, tt, tmid, t_col, width, tm=512):
    bsz, seq, _ = u3.shape
    h = seq // 2
    return pl.pallas_call(
        functools.partial(_hy_fwd_kernel, tm=tm),
        grid=(bsz, h // tm),
        in_specs=[_resident((h, h))] * 4
                 + [pl.BlockSpec((1, seq, width), lambda b, i: (b, 0, u_col)),
                    pl.BlockSpec((4, h, width), lambda b, i: (0, 0, t_col), pipeline_mode=pl.Buffered(1)),
                    pl.BlockSpec((2, width), lambda b, i: (0, t_col))],
        out_specs=pl.BlockSpec((1, 4, tm, width), lambda b, i: (b, 0, i, 0)),
        out_shape=jax.ShapeDtypeStruct((bsz, 4, h, width), BF16),
        compiler_params=_params("arbitrary", "arbitrary"),
        name="hy_fwd",
    )(*mats, u3, tt, tmid)


def _hy_inv_kernel(ce_ref, seg_ref, cog_ref, sog_ref, y_ref, g_ref, u_ref, bias_ref, o_ref, *scr,
                   tm, natural, order):
    i = pl.program_id(1)
    rows = pl.ds(pl.multiple_of(i * tm, tm), tm)
    ye = _dot(ce_ref[rows, :], y_ref[0, 0]) + _dot(seg_ref[rows, :], y_ref[0, 2])
    yo = _dot(cog_ref[rows, :], y_ref[0, 1]) + _dot(sog_ref[rows, :], y_ref[0, 3])
    bias = bias_ref[order:order + 1, :]
    ze = g_ref[0, 0].astype(F32) * (ye + u_ref[0, 0].astype(F32) * bias)
    zo = g_ref[0, 1].astype(F32) * (yo + u_ref[0, 1].astype(F32) * bias)
    if natural:
        scr_ref, = scr
        for sl in range(ze.shape[1] // LANES):
            c = slice(sl * LANES, (sl + 1) * LANES)
            scr_ref[sl, pl.ds(0, tm, stride=2), :] = ze[:, c]
            scr_ref[sl, pl.ds(1, tm, stride=2), :] = zo[:, c]
        for sl in range(ze.shape[1] // LANES):
            o_ref[0, :, sl * LANES:(sl + 1) * LANES] = scr_ref[sl].astype(o_ref.dtype)
    else:
        o_ref[0, 0] = ze.astype(o_ref.dtype)
        o_ref[0, 1] = zo.astype(o_ref.dtype)


def _hy_inv(mats, y4, g4, g_col, u4, u_col, bias, order, width, natural, tm=512):
    bsz, _, h, _ = y4.shape
    par = lambda col: pl.BlockSpec((1, 2, tm, width), lambda b, i: (b, 0, i, col))
    if natural:
        out_spec = pl.BlockSpec((1, 2 * tm, width), lambda b, i: (b, i, 0))
        out_shape = jax.ShapeDtypeStruct((bsz, 2 * h, width), BF16)
        scratch = [pltpu.VMEM((width // LANES, 2 * tm, LANES), F32)]
    else:
        out_spec = par(0)
        out_shape = jax.ShapeDtypeStruct((bsz, 2, h, width), BF16)
        scratch = []
    return pl.pallas_call(
        functools.partial(_hy_inv_kernel, tm=tm, natural=natural, order=order),
        grid=(bsz, h // tm),
        in_specs=[_resident((h, h))] * 4
                 + [pl.BlockSpec((1, 4, h, width), lambda b, i: (b, 0, 0, 0)),
                    par(g_col), par(u_col),
                    pl.BlockSpec(bias.shape, lambda b, i: (0, 0))],
        out_specs=out_spec,
        out_shape=out_shape,
        scratch_shapes=scratch,
        compiler_params=_params("arbitrary", "arbitrary"),
        name="hy_inv",
    )(*mats, y4, g4, u4, bias)


def _cf_kernel(*refs, step_rows, n_cast):
    (am_ref, ap_ref, an_ref, bm_ref, bp_ref, bn_ref, w_ref, cb_ref, lg_ref, lb_ref) = refs[:10]
    cast_in = refs[10:10 + n_cast]
    o_ref = refs[10 + n_cast]
    cast_out = refs[11 + n_cast:11 + 2 * n_cast]
    us_ref, cs_ref, wb_ref = refs[11 + 2 * n_cast:]
    width = wb_ref.shape[-1]
    nsl = width // LANES
    j = pl.program_id(1)

    for src, dst in zip(cast_in, cast_out):
        dst[...] = src[...].astype(dst.dtype)

    @pl.when((pl.program_id(0) == 0) & (j == 0))
    def _():
        for t in range(CF_KERNEL):
            wb_ref[t] = jnp.broadcast_to(w_ref[t:t + 1, :], (8, width))

    glu = lambda a, b: a[0].astype(F32) * _sigmoid(b[0].astype(F32))
    u_prev = jnp.where(j == 0, 0.0, glu(ap_ref, bp_ref))
    u_main = glu(am_ref, bm_ref)
    u_next = jnp.where(j == pl.num_programs(1) - 1, 0.0, glu(an_ref, bn_ref))
    for sl in range(nsl):
        c = slice(sl * LANES, (sl + 1) * LANES)
        us_ref[sl, 0:CF_PAD, :] = u_prev[:, c]
        us_ref[sl, CF_PAD:CF_PAD + step_rows, :] = u_main[:, c]
        us_ref[sl, CF_PAD + step_rows:, :] = u_next[:, c]

    half = CF_KERNEL // 2
    nph = 4
    prow = step_rows // nph
    for sl in range(nsl):
        c = slice(sl * LANES, (sl + 1) * LANES)
        accs = [jnp.zeros((prow // 8, 8, LANES), F32) + cb_ref[:, c][None]] * nph
        for r in range(CF_KERNEL + nph - 1):
            tap = us_ref[sl, pl.ds(CF_PAD - half + r, prow, stride=nph), :].reshape(prow // 8, 8, LANES)
            for p in range(nph):
                if 0 <= r - p < CF_KERNEL:
                    accs[p] = accs[p] + wb_ref[r - p, :, c][None] * tap
        for p in range(nph):
            cs_ref[sl, pl.ds(p, prow, stride=nph), :] = accs[p].reshape(prow, LANES)

    tot = cs_ref[0]
    for sl in range(1, nsl):
        tot = tot + cs_ref[sl]
    mu = jnp.sum(tot, axis=-1, keepdims=True) * (1.0 / width)
    sq = None
    for sl in range(nsl):
        cen = cs_ref[sl] - mu
        sq = cen * cen if sq is None else sq + cen * cen
    rstd = lax.rsqrt(jnp.sum(sq, axis=-1, keepdims=True) * (1.0 / width) + EPS)
    for sl in range(nsl):
        c = slice(sl * LANES, (sl + 1) * LANES)
        y = (cs_ref[sl] - mu) * rstd * lg_ref[:, c] + lb_ref[:, c]
        o_ref[0, :, c] = (y * _sigmoid(y)).astype(o_ref.dtype)


def _cf_conv(proj3, a_col, b_col, w, cb, lg, lb, width, casts, step_rows=256):
    bsz, seq, _ = proj3.shape
    nsteps = seq // step_rows
    total = bsz * nsteps
    nblk = step_rows // CF_PAD
    last_blk = seq // CF_PAD - 1
    vec = pl.BlockSpec((1, width), lambda b, j: (0, 0))
    main = lambda col: pl.BlockSpec((1, step_rows, width), lambda b, j: (b, j, col))
    prev = lambda col: pl.BlockSpec((1, CF_PAD, width), lambda b, j: (b, jnp.maximum(j * nblk - 1, 0), col))
    nxt = lambda col: pl.BlockSpec((1, CF_PAD, width),
                                   lambda b, j: (b, jnp.minimum((j + 1) * nblk, last_blk), col))

    def cast_spec(c):
        n = next(n for n in (total, total // 2, total // 4) if c.shape[0] % (16 * n) == 0)
        every = total // n
        return pl.BlockSpec((c.shape[0] // n, c.shape[1]), lambda b, j: ((b * nsteps + j) // every, 0))
    cast_specs = [cast_spec(c) for c in casts]
    outs = pl.pallas_call(
        functools.partial(_cf_kernel, step_rows=step_rows, n_cast=len(casts)),
        grid=(bsz, nsteps),
        in_specs=[main(a_col), prev(a_col), nxt(a_col), main(b_col), prev(b_col), nxt(b_col),
                  pl.BlockSpec((CF_KERNEL, width), lambda b, j: (0, 0)),
                  vec, vec, vec] + cast_specs,
        out_specs=[pl.BlockSpec((1, step_rows, width), lambda b, j: (b, j, 0))] + cast_specs,
        out_shape=[jax.ShapeDtypeStruct((bsz, seq, width), BF16)]
                  + [jax.ShapeDtypeStruct(c.shape, BF16) for c in casts],
        scratch_shapes=[pltpu.VMEM((width // LANES, step_rows + 2 * CF_PAD, LANES), F32),
                        pltpu.VMEM((width // LANES, step_rows, LANES), F32),
                        pltpu.VMEM((CF_KERNEL, 8, width), F32)],
        compiler_params=_params("arbitrary", "arbitrary"),
        name="cf_conv",
    )(proj3, proj3, proj3, proj3, proj3, proj3, w, cb, lg, lb, *casts)
    return outs[0], outs[1:]


def _merge_kernel(ya_ref, yb_ref, ga0_ref, ga1_ref, gb0_ref, gb1_ref, x_ref, pa_ref, pb_ref, wo_ref,
                  g1_ref, x1_ref):
    a = _dot(ya_ref[...], pa_ref[...])
    b = _dot(yb_ref[...], pb_ref[...])
    w = ga0_ref.shape[1]
    m0 = _sigmoid(ga0_ref[...].astype(F32)) * a[:, :w] + _sigmoid(gb0_ref[...].astype(F32)) * b[:, :w]
    m1 = _sigmoid(ga1_ref[...].astype(F32)) * a[:, w:] + _sigmoid(gb1_ref[...].astype(F32)) * b[:, w:]
    o = _dot(m0.astype(BF16), wo_ref[0:w, :]) + _dot(m1.astype(BF16), wo_ref[w:2 * w, :])
    x1_ref[...] = x_ref[...] + o * lax.rsqrt(jnp.mean(o * o, axis=-1, keepdims=True) + EPS) * g1_ref[...]


def _merge(ya, yb, proj, ga_col, gb_col, x2d, pa, pb, wo, g1, tm=256):
    m, d = x2d.shape
    wa = ya.shape[1]
    assert d == 2 * wa
    const = lambda shape: pl.BlockSpec(shape, lambda i: (0, 0))
    gate = lambda col: pl.BlockSpec((tm, wa), lambda i: (i, col))
    return pl.pallas_call(
        _merge_kernel,
        grid=(m // tm,),
        in_specs=[pl.BlockSpec((tm, wa), lambda i: (i, 0)),
                  pl.BlockSpec((tm, wa), lambda i: (i, 0)),
                  gate(ga_col), gate(ga_col + 1), gate(gb_col), gate(gb_col + 1),
                  pl.BlockSpec((tm, d), lambda i: (i, 0)),
                  const((wa, d)), const((wa, d)), const((d, d)), const((1, d))],
        out_specs=pl.BlockSpec((tm, d), lambda i: (i, 0)),
        out_shape=jax.ShapeDtypeStruct((m, d), F32),
        compiler_params=_params("parallel"),
        name="merge",
    )(ya, yb, proj, proj, proj, proj, x2d, pa, pb, wo, g1)


def _ffn_kernel(x1_ref, gpre_ref, wg_ref, wu_ref, wd_ref, gpost_ref, o_ref, h_ref):
    j = pl.program_id(1)

    @pl.when(j == 0)
    def _():
        x1 = x1_ref[...]
        ms = jnp.mean(x1 * x1, axis=-1, keepdims=True)
        h_ref[...] = (x1 * lax.rsqrt(ms + EPS) * gpre_ref[...]).astype(BF16)
        o_ref[...] = jnp.zeros_like(o_ref)

    h = h_ref[...]
    gate = _dot(h, wg_ref[...])
    up = _dot(h, wu_ref[...])
    act = (gate * _sigmoid(gate) * up).astype(BF16)
    o_ref[...] += _dot(act, wd_ref[...])

    @pl.when(j == pl.num_programs(1) - 1)
    def _():
        a = o_ref[...]
        o_ref[...] = x1_ref[...] + a * lax.rsqrt(jnp.mean(a * a, axis=-1, keepdims=True) + EPS) * gpost_ref[...]


def _ffn(x1, gpre, wgu, wd, gpost, tm=1024, th=512):
    m, d = x1.shape
    hidden = wd.shape[0]
    nh = hidden // th
    return pl.pallas_call(
        _ffn_kernel,
        grid=(m // tm, nh),
        in_specs=[pl.BlockSpec((tm, d), lambda i, j: (i, 0)),
                  pl.BlockSpec((1, d), lambda i, j: (0, 0)),
                  pl.BlockSpec((d, th), lambda i, j: (0, j)),
                  pl.BlockSpec((d, th), lambda i, j: (0, nh + j)),
                  pl.BlockSpec((th, d), lambda i, j: (j, 0)),
                  pl.BlockSpec((1, d), lambda i, j: (0, 0))],
        out_specs=pl.BlockSpec((tm, d), lambda i, j: (i, 0)),
        out_shape=jax.ShapeDtypeStruct((m, d), F32),
        scratch_shapes=[pltpu.VMEM((tm, d), BF16)],
        compiler_params=_params("parallel", "arbitrary", vmem_limit_bytes=V7X_VMEM_LIMIT_FFN_BYTES),
        name="ffn",
    )(x1, gpre, wgu, wgu, wd, gpost)


def kernel(x, mix_pre_g, w_in, hy_conv_w, hy_conv_b, hy_filt_w1, hy_filt_b1, hy_filt_fr1, hy_filt_w2,
           hy_filt_b2, hy_filt_fr2, hy_filt_w3, hy_bias, hy_proj, cf_dw_w, cf_dw_b, cf_ln_g, cf_ln_b,
           cf_proj, w_out, mix_post_g, ffn_pre_g, ffn_w_gu, ffn_w_down, ffn_post_g):
    bsz, seq, d = x.shape
    depth = w_in.shape[0]
    hw = hy_proj.shape[1]
    cw = cf_proj.shape[1]
    assert hw == cw and d % hw == 0
    row = lambda v: v.reshape(1, -1)

    assert depth == 1
    x2d = x.reshape(bsz * seq, d)
    for l in range(depth):
        ce, co, sef, sof, seg, cog, sog = _dft_tables(seq)
        s, dm, tmid, w_in_b = _filt_mlp(seq, hw, hy_filt_w1[l], row(hy_filt_b1[l]), row(hy_filt_fr1[l]),
                                        hy_filt_w2[l], row(hy_filt_b2[l]), row(hy_filt_fr2[l]),
                                        hy_filt_w3[l], w_in[l], d // 2)
        tt, w_in_a = _filt_dft(ce, co, sef, sof, s, dm, w_in[l], d // 2)
        proj = _in_proj(x2d, row(mix_pre_g[l]), w_in_a, w_in_b)
        proj3 = proj.reshape(bsz, seq, -1)
        cf_a_col, cf_b_col = 3, 4
        ga_col, gb_col = 5, 5 + d // hw

        hy = _hy_conv3(proj3, hy_conv_w[l], row(hy_conv_b[l]), 3 * hw)
        hy4 = hy.reshape(bsz, 2, seq // 2, 3 * hw)
        fwd, inv = (ce, co, sef, sof), (ce, seg, cog, sog)
        y1 = _hy_fwd(fwd, hy, 0, tt, tmid, 0, hw)
        z4 = _hy_inv(inv, y1, hy4, 1, hy4, 0, hy_bias[l], 0, hw, natural=False)
        y2 = _hy_fwd(fwd, z4.reshape(bsz, seq, hw), 0, tt, tmid, 1, hw)
        y_a = _hy_inv(inv, y2, hy4, 2, z4, 0, hy_bias[l], 1, hw, natural=True)

        y_b, (hy_proj_bf, cf_proj_bf, w_out_bf, w_gu_bf, w_down_bf) = _cf_conv(
            proj3, cf_a_col, cf_b_col, cf_dw_w[l], row(cf_dw_b[l]), row(cf_ln_g[l]), row(cf_ln_b[l]), cw,
            [hy_proj[l], cf_proj[l], w_out[l], ffn_w_gu[l], ffn_w_down[l]])

        x1 = _merge(y_a.reshape(bsz * seq, hw), y_b.reshape(bsz * seq, cw), proj, ga_col, gb_col, x2d,
                    hy_proj_bf, cf_proj_bf, w_out_bf, row(mix_post_g[l]))
        x2d = _ffn(x1, row(ffn_pre_g[l]), w_gu_bf, w_down_bf, row(ffn_post_g[l]))
    return x2d.reshape(bsz, seq, d)
```

```python
import functools
import math

import jax
import jax.numpy as jnp
from jax import lax
from jax.experimental import pallas as pl
from jax.experimental.pallas import tpu as pltpu

F32 = jnp.float32
BF16 = jnp.bfloat16
EPS = 1e-6

HY_ORDER = 2
HY_EMB_BANDS = 16
HY_DECAY_TARGET = 1e-2
HY_MIN_DECAY = math.log(HY_DECAY_TARGET) / 1.5
HY_MAX_DECAY = math.log(HY_DECAY_TARGET) / 0.3
HY_SHORT = 3
CF_KERNEL = 31
CF_PAD = 16
LANES = 128

V7X_VMEM_LIMIT_BYTES = 56 * 1024 * 1024
V7X_VMEM_LIMIT_FFN_BYTES = 63 * 1024 * 1024


def _params(*sem, vmem_limit_bytes=V7X_VMEM_LIMIT_BYTES):
    return pltpu.CompilerParams(dimension_semantics=sem, vmem_limit_bytes=vmem_limit_bytes)


def _dot(a, b):
    return jnp.dot(a, b, preferred_element_type=F32)


def _sigmoid(x):
    return 1.0 / (1.0 + jnp.exp(-x))


def _resident(shape):
    zeros = (0,) * len(shape)
    return pl.BlockSpec(shape, lambda *_: zeros, pipeline_mode=pl.Buffered(1))


def _parity_split_store(val, scr_ref, dst_ref):
    rows, cols = val.shape
    h = rows // 2
    for sl in range(cols // LANES):
        scr_ref[sl] = val[:, sl * LANES:(sl + 1) * LANES]
    for sl in range(cols // LANES):
        c = slice(sl * LANES, (sl + 1) * LANES)
        dst_ref[0:h, c] = scr_ref[sl, pl.ds(0, h, stride=2), :].astype(dst_ref.dtype)
        dst_ref[h:rows, c] = scr_ref[sl, pl.ds(1, h, stride=2), :].astype(dst_ref.dtype)


def _dft_kernel(ce_ref, co_ref, sef_ref, sof_ref, seg_ref, cog_ref, sog_ref,
                rc_ref, rs_ref, pc_ref, ps_ref, *, tm, h, chunk):
    i = pl.program_id(0)
    mask = 4 * h - 1
    scale = 2.0 * math.pi / (4 * h)

    @pl.when(i == 0)
    def _():
        def base(j, carry):
            r0 = pl.multiple_of(j * chunk, chunk)
            r = r0 + lax.broadcasted_iota(jnp.int32, (chunk, h), 0)
            c = lax.broadcasted_iota(jnp.int32, (chunk, h), 1)
            th_e = ((2 * r * c) & mask).astype(F32) * scale
            th_o = ((r * (2 * c + 1)) & mask).astype(F32) * scale
            rc_ref[pl.ds(r0, chunk), :] = jnp.cos(th_e)
            rs_ref[pl.ds(r0, chunk), :] = jnp.sin(th_e)
            pc_ref[pl.ds(r0, chunk), :] = jnp.cos(th_o)
            ps_ref[pl.ds(r0, chunk), :] = jnp.sin(th_o)
            return carry
        lax.fori_loop(0, tm // chunk, base, 0)

    k0 = i * tm
    c1 = lax.broadcasted_iota(jnp.int32, (1, h), 1)
    th = ((2 * k0 * c1) & mask).astype(F32) * scale
    ce0, se0 = jnp.cos(th), jnp.sin(th)
    th = ((k0 * (2 * c1 + 1)) & mask).astype(F32) * scale
    co0, so0 = jnp.cos(th), jnp.sin(th)
    th = (((2 * k0 + 1) * c1) & mask).astype(F32) * scale
    cg0, sg0 = jnp.cos(th), jnp.sin(th)

    def tile(j, carry):
        r0 = pl.multiple_of(j * chunk, chunk)
        rows = pl.ds(r0, chunk)
        rc, rs, pc, ps = rc_ref[rows, :], rs_ref[rows, :], pc_ref[rows, :], ps_ref[rows, :]
        row = k0 + r0 + lax.broadcasted_iota(jnp.int32, (chunk, h), 0)
        col = lax.broadcasted_iota(jnp.int32, (chunk, h), 1)
        alt_col = (1 - 2 * (col & 1)).astype(F32)
        alt_row = (1 - 2 * (row & 1)).astype(F32)
        se = rs * ce0 + rc * se0
        ce_ref[rows, :] = (rc * ce0 - rs * se0).astype(BF16)
        co_ref[rows, :] = (pc * co0 - ps * so0).astype(BF16)
        sef_ref[rows, :] = jnp.where(row == 0, alt_col, se).astype(BF16)
        sof_ref[rows, :] = jnp.where(row == 0, alt_col, ps * co0 + pc * so0).astype(BF16)
        seg_ref[rows, :] = jnp.where(col == 0, alt_row, se).astype(BF16)
        cog_ref[rows, :] = (rc * cg0 - rs * sg0).astype(BF16)
        sog_ref[rows, :] = jnp.where(col == 0, alt_row, rs * cg0 + rc * sg0).astype(BF16)
        return carry
    lax.fori_loop(0, tm // chunk, tile, 0)


def _dft_tables(seq, tm=256, chunk=32):
    h = seq // 2
    out = jax.ShapeDtypeStruct((h, h), BF16)
    spec = pl.BlockSpec((tm, h), lambda i: (i, 0))
    return pl.pallas_call(
        functools.partial(_dft_kernel, tm=tm, h=h, chunk=chunk),
        grid=(h // tm,),
        out_specs=[spec] * 7,
        out_shape=[out] * 7,
        scratch_shapes=[pltpu.VMEM((tm, h), F32)] * 4,
        compiler_params=_params("arbitrary"),
        name="dft_tables",
    )()


def _in_proj_kernel(x_ref, g_ref, wa_ref, wb_ref, o_ref, h_ref):
    @pl.when(pl.program_id(1) == 0)
    def _():
        x = x_ref[...]
        ms = jnp.mean(x * x, axis=-1, keepdims=True)
        h_ref[...] = (x * lax.rsqrt(ms + EPS) * g_ref[...]).astype(BF16)

    ka = wa_ref.shape[0]
    acc = _dot(h_ref[:, 0:ka], wa_ref[...]) + _dot(h_ref[:, ka:], wb_ref[...])
    o_ref[...] = acc.astype(o_ref.dtype)


def _in_proj(x2d, g, wa, wb, tm=1024, tn=2304):
    m, d = x2d.shape
    n = wa.shape[1]
    assert wa.shape[0] + wb.shape[0] == d
    return pl.pallas_call(
        _in_proj_kernel,
        grid=(m // tm, n // tn),
        in_specs=[pl.BlockSpec((tm, d), lambda i, j: (i, 0)),
                  pl.BlockSpec((1, d), lambda i, j: (0, 0)),
                  pl.BlockSpec((wa.shape[0], tn), lambda i, j: (0, j)),
                  pl.BlockSpec((wb.shape[0], tn), lambda i, j: (0, j))],
        out_specs=pl.BlockSpec((tm, tn), lambda i, j: (i, j)),
        out_shape=jax.ShapeDtypeStruct((m, n), BF16),
        scratch_shapes=[pltpu.VMEM((tm, d), BF16)],
        compiler_params=_params("parallel", "arbitrary", vmem_limit_bytes=V7X_VMEM_LIMIT_FFN_BYTES),
        name="in_proj",
    )(x2d, g, wa, wb)


def _conv3_kernel(p_ref, w_ref, b_ref, o_ref, scr_ref):
    rows, cols = p_ref.shape[1], p_ref.shape[2]
    h = rows // 2
    pad = 8
    zeros = jnp.zeros((pad, LANES), F32)
    for sl in range(cols // LANES):
        c = slice(sl * LANES, (sl + 1) * LANES)
        scr_ref[sl, 0:pad, :] = zeros
        scr_ref[sl, pad:pad + rows, :] = p_ref[0, :, c].astype(F32)
        scr_ref[sl, pad + rows:, :] = zeros
    for sl in range(cols // LANES):
        c = slice(sl * LANES, (sl + 1) * LANES)
        w0, w1, w2, b = w_ref[0:1, c], w_ref[1:2, c], w_ref[2:3, c], b_ref[:, c]
        xom = scr_ref[sl, pl.ds(pad - 1, h, stride=2), :]
        xe = scr_ref[sl, pl.ds(pad, h, stride=2), :]
        xo = scr_ref[sl, pl.ds(pad + 1, h, stride=2), :]
        xep = scr_ref[sl, pl.ds(pad + 2, h, stride=2), :]
        o_ref[0, 0:h, c] = (w0 * xom + w1 * xe + w2 * xo + b).astype(o_ref.dtype)
        o_ref[0, h:rows, c] = (w0 * xe + w1 * xo + w2 * xep + b).astype(o_ref.dtype)


def _hy_conv3(proj3, w, b, width, ct=1024):
    bsz, seq, _ = proj3.shape
    return pl.pallas_call(
        _conv3_kernel,
        grid=(bsz, width // ct),
        in_specs=[pl.BlockSpec((1, seq, ct), lambda i, j: (i, 0, j)),
                  pl.BlockSpec((HY_SHORT, ct), lambda i, j: (0, j)),
                  pl.BlockSpec((1, ct), lambda i, j: (0, j))],
        out_specs=pl.BlockSpec((1, seq, ct), lambda i, j: (i, 0, j)),
        out_shape=jax.ShapeDtypeStruct((bsz, seq, width), BF16),
        scratch_shapes=[pltpu.VMEM((ct // LANES, seq + 16, LANES), F32)],
        compiler_params=_params("parallel", "parallel"),
        name="hy_conv3",
    )(proj3, w, b)


def _filt_mlp_kernel(mlp_ref, w3f_ref, w3b_ref, w_ref, s_ref, d_ref, tmid_ref, wbf_ref, h2_ref, scr_ref,
                     *, seq, width, ct):
    jc = pl.program_id(1)
    wbf_ref[...] = w_ref[...].astype(BF16)
    n = lax.broadcasted_iota(jnp.int32, (seq, 1), 0).astype(F32)
    t = n / (seq - 1)

    @pl.when((pl.program_id(0) == 0) & (jc == 0))
    def _():
        nl = lax.broadcasted_iota(jnp.int32, (1, seq), 1).astype(F32)
        tl = nl / (seq - 1)
        wang = 2.0 * math.pi * nl / seq
        band = lax.broadcasted_iota(jnp.int32, (HY_EMB_BANDS, 1), 0).astype(F32)
        f = 1e-4 + band * ((HY_EMB_BANDS - 1 - 1e-4) / (HY_EMB_BANDS - 1))
        fw = f * wang
        p = mlp_ref[...]
        hid = p.shape[0]
        nb = HY_EMB_BANDS
        w1t, w1c, w1s = p[:, 0:1], p[:, 1:1 + nb], p[:, 1 + nb:1 + 2 * nb]
        c0 = 1 + 2 * nb
        b1, fr1, b2, fr2 = p[:, c0:c0 + 1], p[:, c0 + 1:c0 + 2], p[:, c0 + 2:c0 + 3], p[:, c0 + 3:c0 + 4]
        w2t = p[:, c0 + 4:c0 + 4 + hid]
        pre1 = w1t * tl + _dot(w1c, jnp.cos(fw)) - _dot(w1s, jnp.sin(fw)) + b1
        h1 = jnp.sin(fr1 * pre1)
        h2t = jnp.sin(fr2 * (_dot(w2t, h1) + b2))
        h2_ref[...] = h2t.T

    h2 = h2_ref[...]
    ch = (jc * ct + lax.broadcasted_iota(jnp.int32, (1, ct), 1)).astype(F32)
    delta = HY_MIN_DECAY + ch * ((HY_MAX_DECAY - HY_MIN_DECAY) / (width - 1))
    decay = jnp.exp(-t * jnp.abs(delta))
    kf = _dot(h2, w3f_ref[...]) * decay
    kb = _dot(h2, w3b_ref[...]) * decay
    row = lax.broadcasted_iota(jnp.int32, (seq, ct), 0)
    kb = jnp.where(row == 0, 0.0, kb)
    l1 = jnp.sum(jnp.abs(kf), axis=0, keepdims=True) + jnp.sum(jnp.abs(kb), axis=0, keepdims=True)
    inv = 1.0 / l1
    s = (kf + kb) * inv
    dm = (kf - kb) * inv
    phase = row & 3
    cmid = jnp.where(phase == 0, 1.0, jnp.where(phase == 2, -1.0, 0.0))
    smid = jnp.where(phase == 1, 1.0, jnp.where(phase == 3, -1.0, 0.0))
    wmid = 2.0 / (2 * seq)
    tmid_ref[0:1, :] = jnp.sum(s * cmid, axis=0, keepdims=True) * wmid
    tmid_ref[1:2, :] = jnp.sum(dm * smid, axis=0, keepdims=True) * (-wmid)
    _parity_split_store(s, scr_ref, s_ref)
    _parity_split_store(dm, scr_ref, d_ref)


def _pack_filter_mlp(w1, b1, fr1, w2, b2, fr2):
    cols = [w1.T, b1.reshape(-1, 1), fr1.reshape(-1, 1), b2.reshape(-1, 1), fr2.reshape(-1, 1), w2.T]
    packed = jnp.concatenate(cols, axis=1)
    pad = -packed.shape[1] % LANES
    return jnp.pad(packed, ((0, 0), (0, pad)))


def _filt_mlp(seq, width, w1, b1, fr1, w2, b2, fr2, w3, w, w_row0, ct=256):
    hid = w2.shape[0]
    mlp = _pack_filter_mlp(w1, b1, fr1, w2, b2, fr2)
    nct = width // ct
    steps = HY_ORDER * nct
    w_rows = w.shape[0] - w_row0
    wr = w_rows // steps
    assert w_row0 % wr == 0
    small = lambda shape: pl.BlockSpec(shape, lambda o, j: (0, 0))
    out_spec = pl.BlockSpec((seq, ct), lambda o, j: (0, o * nct + j))
    return pl.pallas_call(
        functools.partial(_filt_mlp_kernel, seq=seq, width=width, ct=ct),
        grid=(HY_ORDER, nct),
        in_specs=[small(mlp.shape),
                  pl.BlockSpec((hid, ct), lambda o, j: (0, o * 2 * nct + j)),
                  pl.BlockSpec((hid, ct), lambda o, j: (0, o * 2 * nct + nct + j)),
                  pl.BlockSpec((wr, w.shape[1]), lambda o, j: (w_row0 // wr + o * nct + j, 0))],
        out_specs=[out_spec, out_spec, pl.BlockSpec((2, ct), lambda o, j: (0, o * nct + j)),
                   pl.BlockSpec((wr, w.shape[1]), lambda o, j: (o * nct + j, 0))],
        out_shape=[jax.ShapeDtypeStruct((seq, HY_ORDER * width), BF16),
                   jax.ShapeDtypeStruct((seq, HY_ORDER * width), BF16),
                   jax.ShapeDtypeStruct((2, HY_ORDER * width), F32),
                   jax.ShapeDtypeStruct((w_rows, w.shape[1]), BF16)],
        scratch_shapes=[pltpu.VMEM((seq, hid), F32), pltpu.VMEM((ct // LANES, seq, LANES), F32)],
        compiler_params=_params("arbitrary", "arbitrary"),
        name="filt_mlp",
    )(mlp, w3, w3, w)


def _filt_dft_kernel(ce_ref, co_ref, sef_ref, sof_ref, s_ref, d_ref, w_ref, t_ref, wbf_ref, *, tm, seq):
    h = seq // 2
    wbf_ref[...] = w_ref[...].astype(BF16)
    rows = pl.ds(pl.multiple_of(pl.program_id(1) * tm, tm), tm)
    pc = _dot(ce_ref[rows, :], s_ref[0:h, :])
    qc = _dot(co_ref[rows, :], s_ref[h:seq, :])
    ps = _dot(sef_ref[rows, :], d_ref[0:h, :])
    qs = _dot(sof_ref[rows, :], d_ref[h:seq, :])
    row = pl.program_id(1) * tm + lax.broadcasted_iota(jnp.int32, pc.shape, 0)
    wk = jnp.where(row == 0, 1.0, 2.0) * (1.0 / (2 * seq))
    t_ref[0] = ((pc + qc) * wk).astype(BF16)
    t_ref[1] = jnp.where(row == 0, 0.0, -(ps + qs) * wk).astype(BF16)
    t_ref[2] = ((pc - qc) * wk).astype(BF16)
    t_ref[3] = jnp.where(row == 0, 0.0, (ps - qs) * wk).astype(BF16)


def _filt_dft(ce, co, sef, sof, s, d, w, w_rows, tm=512, tn=512):
    h = ce.shape[0]
    seq, cols = s.shape
    ni = h // tm
    wr = w_rows // (cols // tn * ni)
    tab = pl.BlockSpec((seq, tn), lambda j, i: (0, j))
    wspec = pl.BlockSpec((wr, w.shape[1]), lambda j, i: (j * ni + i, 0))
    return pl.pallas_call(
        functools.partial(_filt_dft_kernel, tm=tm, seq=seq),
        grid=(cols // tn, ni),
        in_specs=[_resident((h, h))] * 4 + [tab, tab, wspec],
        out_specs=[pl.BlockSpec((4, tm, tn), lambda j, i: (0, i, j)), wspec],
        out_shape=[jax.ShapeDtypeStruct((4, h, cols), BF16),
                   jax.ShapeDtypeStruct((w_rows, w.shape[1]), BF16)],
        compiler_params=_params("arbitrary", "arbitrary"),
        name="filt_dft",
    )(ce, co, sef, sof, s, d, w)


def _hy_fwd_kernel(ce_ref, co_ref, sef_ref, sof_ref, u_ref, t_ref, tmid_ref, y_ref, *, tm):
    h = ce_ref.shape[0]
    i = pl.program_id(1)
    rows = pl.ds(pl.multiple_of(i * tm, tm), tm)
    ue = u_ref[0, 0:h, :]
    uo = u_ref[0, h:2 * h, :]
    pc = _dot(ce_ref[rows, :], ue)
    qc = _dot(co_ref[rows, :], uo)
    ps = _dot(sef_ref[rows, :], ue)
    qs = _dot(sof_ref[rows, :], uo)
    r0 = (i * tm + lax.broadcasted_iota(jnp.int32, pc.shape, 0)) == 0
    tr1, ti1 = t_ref[0, rows, :].astype(F32), t_ref[1, rows, :].astype(F32)
    tr2, ti2 = t_ref[2, rows, :].astype(F32), t_ref[3, rows, :].astype(F32)
    a1, a2 = pc + qc, pc - qc
    b1, b2 = ps + qs, qs - ps
    yr1, yi1 = a1 * tr1 + b1 * ti1, b1 * tr1 - a1 * ti1
    yr2, yi2 = a2 * tr2 + b2 * ti2, b2 * tr2 - a2 * ti2
    trh, tih = tmid_ref[0:1, :], tmid_ref[1:2, :]
    y_ref[0, 0] = (yr1 + yr2).astype(BF16)
    y_ref[0, 1] = (yr1 - yr2).astype(BF16)
    y_ref[0, 2] = jnp.where(r0, ps * trh + qs * tih, yi1 - yi2).astype(BF16)
    y_ref[0, 3] = jnp.where(r0, qs * trh - ps * tih, yi1 + yi2).astype(BF16)


def _hy_fwd(mats, u3, u_col, tt, tmid, t_col, width, tm=512):
    bsz, seq, _ = u3.shape
    h = seq // 2
    return pl.pallas_call(
        functools.partial(_hy_fwd_kernel, tm=tm),
        grid=(bsz, h // tm),
        in_specs=[_resident((h, h))] * 4
                 + [pl.BlockSpec((1, seq, width), lambda b, i: (b, 0, u_col)),
                    pl.BlockSpec((4, h, width), lambda b, i: (0, 0, t_col), pipeline_mode=pl.Buffered(1)),
                    pl.BlockSpec((2, width), lambda b, i: (0, t_col))],
        out_specs=pl.BlockSpec((1, 4, tm, width), lambda b, i: (b, 0, i, 0)),
        out_shape=jax.ShapeDtypeStruct((bsz, 4, h, width), BF16),
        compiler_params=_params("arbitrary", "arbitrary"),
        name="hy_fwd",
    )(*mats, u3, tt, tmid)


def _hy_inv_kernel(ce_ref, seg_ref, cog_ref, sog_ref, y_ref, g_ref, u_ref, bias_ref, o_ref, *scr,
                   tm, natural, order):
    i = pl.program_id(1)
    rows = pl.ds(pl.multiple_of(i * tm, tm), tm)
    ye = _dot(ce_ref[rows, :], y_ref[0, 0]) + _dot(seg_ref[rows, :], y_ref[0, 2])
    yo = _dot(cog_ref[rows, :], y_ref[0, 1]) + _dot(sog_ref[rows, :], y_ref[0, 3])
    bias = bias_ref[order:order + 1, :]
    ze = g_ref[0, 0].astype(F32) * (ye + u_ref[0, 0].astype(F32) * bias)
    zo = g_ref[0, 1].astype(F32) * (yo + u_ref[0, 1].astype(F32) * bias)
    if natural:
        scr_ref, = scr
        for sl in range(ze.shape[1] // LANES):
            c = slice(sl * LANES, (sl + 1) * LANES)
            scr_ref[sl, pl.ds(0, tm, stride=2), :] = ze[:, c]
            scr_ref[sl, pl.ds(1, tm, stride=2), :] = zo[:, c]
        for sl in range(ze.shape[1] // LANES):
            o_ref[0, :, sl * LANES:(sl + 1) * LANES] = scr_ref[sl].astype(o_ref.dtype)
    else:
        o_ref[0, 0] = ze.astype(o_ref.dtype)
        o_ref[0, 1] = zo.astype(o_ref.dtype)


def _hy_inv(mats, y4, g4, g_col, u4, u_col, bias, order, width, natural, tm=512):
    bsz, _, h, _ = y4.shape
    par = lambda col: pl.BlockSpec((1, 2, tm, width), lambda b, i: (b, 0, i, col))
    if natural:
        out_spec = pl.BlockSpec((1, 2 * tm, width), lambda b, i: (b, i, 0))
        out_shape = jax.ShapeDtypeStruct((bsz, 2 * h, width), BF16)
        scratch = [pltpu.VMEM((width // LANES, 2 * tm, LANES), F32)]
    else:
        out_spec = par(0)
        out_shape = jax.ShapeDtypeStruct((bsz, 2, h, width), BF16)
        scratch = []
    return pl.pallas_call(
        functools.partial(_hy_inv_kernel, tm=tm, natural=natural, order=order),
        grid=(bsz, h // tm),
        in_specs=[_resident((h, h))] * 4
                 + [pl.BlockSpec((1, 4, h, width), lambda b, i: (b, 0, 0, 0)),
                    par(g_col), par(u_col),
                    pl.BlockSpec(bias.shape, lambda b, i: (0, 0))],
        out_specs=out_spec,
        out_shape=out_shape,
        scratch_shapes=scratch,
        compiler_params=_params("arbitrary", "arbitrary"),
        name="hy_inv",
    )(*mats, y4, g4, u4, bias)


def _cf_kernel(*refs, step_rows, n_cast):
    (am_ref, ap_ref, an_ref, bm_ref, bp_ref, bn_ref, w_ref, cb_ref, lg_ref, lb_ref) = refs[:10]
    cast_in = refs[10:10 + n_cast]
    o_ref = refs[10 + n_cast]
    cast_out = refs[11 + n_cast:11 + 2 * n_cast]
    us_ref, cs_ref, wb_ref = refs[11 + 2 * n_cast:]
    width = wb_ref.shape[-1]
    nsl = width // LANES
    j = pl.program_id(1)

    for src, dst in zip(cast_in, cast_out):
        dst[...] = src[...].astype(dst.dtype)

    @pl.when((pl.program_id(0) == 0) & (j == 0))
    def _():
        for t in range(CF_KERNEL):
            wb_ref[t] = jnp.broadcast_to(w_ref[t:t + 1, :], (8, width))

    glu = lambda a, b: a[0].astype(F32) * _sigmoid(b[0].astype(F32))
    u_prev = jnp.where(j == 0, 0.0, glu(ap_ref, bp_ref))
    u_main = glu(am_ref, bm_ref)
    u_next = jnp.where(j == pl.num_programs(1) - 1, 0.0, glu(an_ref, bn_ref))
    for sl in range(nsl):
        c = slice(sl * LANES, (sl + 1) * LANES)
        us_ref[sl, 0:CF_PAD, :] = u_prev[:, c]
        us_ref[sl, CF_PAD:CF_PAD + step_rows, :] = u_main[:, c]
        us_ref[sl, CF_PAD + step_rows:, :] = u_next[:, c]

    half = CF_KERNEL // 2
    nph = 4
    prow = step_rows // nph
    for sl in range(nsl):
        c = slice(sl * LANES, (sl + 1) * LANES)
        accs = [jnp.zeros((prow // 8, 8, LANES), F32) + cb_ref[:, c][None]] * nph
        for r in range(CF_KERNEL + nph - 1):
            tap = us_ref[sl, pl.ds(CF_PAD - half + r, prow, stride=nph), :].reshape(prow // 8, 8, LANES)
            for p in range(nph):
                if 0 <= r - p < CF_KERNEL:
                    accs[p] = accs[p] + wb_ref[r - p, :, c][None] * tap
        for p in range(nph):
            cs_ref[sl, pl.ds(p, prow, stride=nph), :] = accs[p].reshape(prow, LANES)

    tot = cs_ref[0]
    for sl in range(1, nsl):
        tot = tot + cs_ref[sl]
    mu = jnp.sum(tot, axis=-1, keepdims=True) * (1.0 / width)
    sq = None
    for sl in range(nsl):
        cen = cs_ref[sl] - mu
        sq = cen * cen if sq is None else sq + cen * cen
    rstd = lax.rsqrt(jnp.sum(sq, axis=-1, keepdims=True) * (1.0 / width) + EPS)
    for sl in range(nsl):
        c = slice(sl * LANES, (sl + 1) * LANES)
        y = (cs_ref[sl] - mu) * rstd * lg_ref[:, c] + lb_ref[:, c]
        o_ref[0, :, c] = (y * _sigmoid(y)).astype(o_ref.dtype)


def _cf_conv(proj3, a_col, b_col, w, cb, lg, lb, width, casts, step_rows=256):
    bsz, seq, _ = proj3.shape
    nsteps = seq // step_rows
    total = bsz * nsteps
    nblk = step_rows // CF_PAD
    last_blk = seq // CF_PAD - 1
    vec = pl.BlockSpec((1, width), lambda b, j: (0, 0))
    main = lambda col: pl.BlockSpec((1, step_rows, width), lambda b, j: (b, j, col))
    prev = lambda col: pl.BlockSpec((1, CF_PAD, width), lambda b, j: (b, jnp.maximum(j * nblk - 1, 0), col))
    nxt = lambda col: pl.BlockSpec((1, CF_PAD, width),
                                   lambda b, j: (b, jnp.minimum((j + 1) * nblk, last_blk), col))

    def cast_spec(c):
        n = next(n for n in (total, total // 2, total // 4) if c.shape[0] % (16 * n) == 0)
        every = total // n
        return pl.BlockSpec((c.shape[0] // n, c.shape[1]), lambda b, j: ((b * nsteps + j) // every, 0))
    cast_specs = [cast_spec(c) for c in casts]
    outs = pl.pallas_call(
        functools.partial(_cf_kernel, step_rows=step_rows, n_cast=len(casts)),
        grid=(bsz, nsteps),
        in_specs=[main(a_col), prev(a_col), nxt(a_col), main(b_col), prev(b_col), nxt(b_col),
                  pl.BlockSpec((CF_KERNEL, width), lambda b, j: (0, 0)),
                  vec, vec, vec] + cast_specs,
        out_specs=[pl.BlockSpec((1, step_rows, width), lambda b, j: (b, j, 0))] + cast_specs,
        out_shape=[jax.ShapeDtypeStruct((bsz, seq, width), BF16)]
                  + [jax.ShapeDtypeStruct(c.shape, BF16) for c in casts],
        scratch_shapes=[pltpu.VMEM((width // LANES, step_rows + 2 * CF_PAD, LANES), F32),
                        pltpu.VMEM((width // LANES, step_rows, LANES), F32),
                        pltpu.VMEM((CF_KERNEL, 8, width), F32)],
        compiler_params=_params("arbitrary", "arbitrary"),
        name="cf_conv",
    )(proj3, proj3, proj3, proj3, proj3, proj3, w, cb, lg, lb, *casts)
    return outs[0], outs[1:]


def _merge_kernel(ya_ref, yb_ref, ga0_ref, ga1_ref, gb0_ref, gb1_ref, x_ref, pa_ref, pb_ref, wo_ref,
                  g1_ref, x1_ref):
    a = _dot(ya_ref[...], pa_ref[...])
    b = _dot(yb_ref[...], pb_ref[...])
    w = ga0_ref.shape[1]
    m0 = _sigmoid(ga0_ref[...].astype(F32)) * a[:, :w] + _sigmoid(gb0_ref[...].astype(F32)) * b[:, :w]
    m1 = _sigmoid(ga1_ref[...].astype(F32)) * a[:, w:] + _sigmoid(gb1_ref[...].astype(F32)) * b[:, w:]
    o = _dot(m0.astype(BF16), wo_ref[0:w, :]) + _dot(m1.astype(BF16), wo_ref[w:2 * w, :])
    x1_ref[...] = x_ref[...] + o * lax.rsqrt(jnp.mean(o * o, axis=-1, keepdims=True) + EPS) * g1_ref[...]


def _merge(ya, yb, proj, ga_col, gb_col, x2d, pa, pb, wo, g1, tm=256):
    m, d = x2d.shape
    wa = ya.shape[1]
    assert d == 2 * wa
    const = lambda shape: pl.BlockSpec(shape, lambda i: (0, 0))
    gate = lambda col: pl.BlockSpec((tm, wa), lambda i: (i, col))
    return pl.pallas_call(
        _merge_kernel,
        grid=(m // tm,),
        in_specs=[pl.BlockSpec((tm, wa), lambda i: (i, 0)),
                  pl.BlockSpec((tm, wa), lambda i: (i, 0)),
                  gate(ga_col), gate(ga_col + 1), gate(gb_col), gate(gb_col + 1),
                  pl.BlockSpec((tm, d), lambda i: (i, 0)),
                  const((wa, d)), const((wa, d)), const((d, d)), const((1, d))],
        out_specs=pl.BlockSpec((tm, d), lambda i: (i, 0)),
        out_shape=jax.ShapeDtypeStruct((m, d), F32),
        compiler_params=_params("parallel"),
        name="merge",
    )(ya, yb, proj, proj, proj, proj, x2d, pa, pb, wo, g1)


def _ffn_kernel(x1_ref, gpre_ref, wg_ref, wu_ref, wd_ref, gpost_ref, o_ref, h_ref):
    j = pl.program_id(1)

    @pl.when(j == 0)
    def _():
        x1 = x1_ref[...]
        ms = jnp.mean(x1 * x1, axis=-1, keepdims=True)
        h_ref[...] = (x1 * lax.rsqrt(ms + EPS) * gpre_ref[...]).astype(BF16)
        o_ref[...] = jnp.zeros_like(o_ref)

    h = h_ref[...]
    gate = _dot(h, wg_ref[...])
    up = _dot(h, wu_ref[...])
    act = (gate * _sigmoid(gate) * up).astype(BF16)
    o_ref[...] += _dot(act, wd_ref[...])

    @pl.when(j == pl.num_programs(1) - 1)
    def _():
        a = o_ref[...]
        o_ref[...] = x1_ref[...] + a * lax.rsqrt(jnp.mean(a * a, axis=-1, keepdims=True) + EPS) * gpost_ref[...]


def _ffn(x1, gpre, wgu, wd, gpost, tm=1024, th=512):
    m, d = x1.shape
    hidden = wd.shape[0]
    nh = hidden // th
    return pl.pallas_call(
        _ffn_kernel,
        grid=(m // tm, nh),
        in_specs=[pl.BlockSpec((tm, d), lambda i, j: (i, 0)),
                  pl.BlockSpec((1, d), lambda i, j: (0, 0)),
                  pl.BlockSpec((d, th), lambda i, j: (0, j)),
                  pl.BlockSpec((d, th), lambda i, j: (0, nh + j)),
                  pl.BlockSpec((th, d), lambda i, j: (j, 0)),
                  pl.BlockSpec((1, d), lambda i, j: (0, 0))],
        out_specs=pl.BlockSpec((tm, d), lambda i, j: (i, 0)),
        out_shape=jax.ShapeDtypeStruct((m, d), F32),
        scratch_shapes=[pltpu.VMEM((tm, d), BF16)],
        compiler_params=_params("parallel", "arbitrary", vmem_limit_bytes=V7X_VMEM_LIMIT_FFN_BYTES),
        name="ffn",
    )(x1, gpre, wgu, wgu, wd, gpost)


def kernel(x, mix_pre_g, w_in, hy_conv_w, hy_conv_b, hy_filt_w1, hy_filt_b1, hy_filt_fr1, hy_filt_w2,
           hy_filt_b2, hy_filt_fr2, hy_filt_w3, hy_bias, hy_proj, cf_dw_w, cf_dw_b, cf_ln_g, cf_ln_b,
           cf_proj, w_out, mix_post_g, ffn_pre_g, ffn_w_gu, ffn_w_down, ffn_post_g):
    bsz, seq, d = x.shape
    depth = w_in.shape[0]
    hw = hy_proj.shape[1]
    cw = cf_proj.shape[1]
    assert hw == cw and d % hw == 0
    row = lambda v: v.reshape(1, -1)

    assert depth == 1
    x2d = x.reshape(bsz * seq, d)
    for l in range(depth):
        ce, co, sef, sof, seg, cog, sog = _dft_tables(seq)
        s, dm, tmid, w_in_b = _filt_mlp(seq, hw, hy_filt_w1[l], row(hy_filt_b1[l]), row(hy_filt_fr1[l]),
                                        hy_filt_w2[l], row(hy_filt_b2[l]), row(hy_filt_fr2[l]),
                                        hy_filt_w3[l], w_in[l], d // 2)
        tt, w_in_a = _filt_dft(ce, co, sef, sof, s, dm, w_in[l], d // 2)
        proj = _in_proj(x2d, row(mix_pre_g[l]), w_in_a, w_in_b)
        proj3 = proj.reshape(bsz, seq, -1)
        cf_a_col, cf_b_col = 3, 4
        ga_col, gb_col = 5, 5 + d // hw

        hy = _hy_conv3(proj3, hy_conv_w[l], row(hy_conv_b[l]), 3 * hw)
        hy4 = hy.reshape(bsz, 2, seq // 2, 3 * hw)
        fwd, inv = (ce, co, sef, sof), (ce, seg, cog, sog)
        y1 = _hy_fwd(fwd, hy, 0, tt, tmid, 0, hw)
        z4 = _hy_inv(inv, y1, hy4, 1, hy4, 0, hy_bias[l], 0, hw, natural=False)
        y2 = _hy_fwd(fwd, z4.reshape(bsz, seq, hw), 0, tt, tmid, 1, hw)
        y_a = _hy_inv(inv, y2, hy4, 2, z4, 0, hy_bias[l], 1, hw, natural=True)

        y_b, (hy_proj_bf, cf_proj_bf, w_out_bf, w_gu_bf, w_down_bf) = _cf_conv(
            proj3, cf_a_col, cf_b_col, cf_dw_w[l], row(cf_dw_b[l]), row(cf_ln_g[l]), row(cf_ln_b[l]), cw,
            [hy_proj[l], cf_proj[l], w_out[l], ffn_w_gu[l], ffn_w_down[l]])

        x1 = _merge(y_a.reshape(bsz * seq, hw), y_b.reshape(bsz * seq, cw), proj, ga_col, gb_col, x2d,
                    hy_proj_bf, cf_proj_bf, w_out_bf, row(mix_post_g[l]))
        x2d = _ffn(x1, row(ffn_pre_g[l]), w_gu_bf, w_down_bf, row(ffn_post_g[l]))
    return x2d.reshape(bsz, seq, d)
```

```python
import functools
import math

import jax
import jax.numpy as jnp
from jax import lax
from jax.experimental import pallas as pl
from jax.experimental.pallas import tpu as pltpu

F32 = jnp.float32
BF16 = jnp.bfloat16
EPS = 1e-6

HY_ORDER = 2
HY_EMB_BANDS = 16
HY_DECAY_TARGET = 1e-2
HY_MIN_DECAY = math.log(HY_DECAY_TARGET) / 1.5
HY_MAX_DECAY = math.log(HY_DECAY_TARGET) / 0.3
HY_SHORT = 3
CF_KERNEL = 31
CF_PAD = 16
LANES = 128

V7X_VMEM_LIMIT_BYTES = 56 * 1024 * 1024
V7X_VMEM_LIMIT_LARGE_BYTES = 63 * 1024 * 1024


def _params(*sem, vmem_limit_bytes=V7X_VMEM_LIMIT_BYTES):
    return pltpu.CompilerParams(dimension_semantics=sem, vmem_limit_bytes=vmem_limit_bytes)


def _dot(a, b):
    return jnp.dot(a, b, preferred_element_type=F32)


def _sigmoid(x):
    return 1.0 / (1.0 + jnp.exp(-x))


def _resident(shape):
    zeros = (0,) * len(shape)
    return pl.BlockSpec(shape, lambda *_: zeros, pipeline_mode=pl.Buffered(1))


def _parity_split_store(val, scr_ref, dst_ref):
    rows, cols = val.shape
    h = rows // 2
    for sl in range(cols // LANES):
        scr_ref[sl] = val[:, sl * LANES:(sl + 1) * LANES]
    for sl in range(cols // LANES):
        c = slice(sl * LANES, (sl + 1) * LANES)
        dst_ref[0:h, c] = scr_ref[sl, pl.ds(0, h, stride=2), :].astype(dst_ref.dtype)
        dst_ref[h:rows, c] = scr_ref[sl, pl.ds(1, h, stride=2), :].astype(dst_ref.dtype)


def _dft_kernel(ce_ref, co_ref, sef_ref, sof_ref, seg_ref, cog_ref, sog_ref,
                rc_ref, rs_ref, pc_ref, ps_ref, *, tm, h, chunk):
    i = pl.program_id(0)
    mask = 4 * h - 1
    scale = 2.0 * math.pi / (4 * h)

    @pl.when(i == 0)
    def _():
        def base(j, carry):
            r0 = pl.multiple_of(j * chunk, chunk)
            r = r0 + lax.broadcasted_iota(jnp.int32, (chunk, h), 0)
            c = lax.broadcasted_iota(jnp.int32, (chunk, h), 1)
            th_e = ((2 * r * c) & mask).astype(F32) * scale
            th_o = ((r * (2 * c + 1)) & mask).astype(F32) * scale
            rc_ref[pl.ds(r0, chunk), :] = jnp.cos(th_e)
            rs_ref[pl.ds(r0, chunk), :] = jnp.sin(th_e)
            pc_ref[pl.ds(r0, chunk), :] = jnp.cos(th_o)
            ps_ref[pl.ds(r0, chunk), :] = jnp.sin(th_o)
            return carry
        lax.fori_loop(0, tm // chunk, base, 0)

    k0 = i * tm
    c1 = lax.broadcasted_iota(jnp.int32, (1, h), 1)
    th = ((2 * k0 * c1) & mask).astype(F32) * scale
    ce0, se0 = jnp.cos(th), jnp.sin(th)
    th = ((k0 * (2 * c1 + 1)) & mask).astype(F32) * scale
    co0, so0 = jnp.cos(th), jnp.sin(th)
    th = (((2 * k0 + 1) * c1) & mask).astype(F32) * scale
    cg0, sg0 = jnp.cos(th), jnp.sin(th)

    def tile(j, carry):
        r0 = pl.multiple_of(j * chunk, chunk)
        rows = pl.ds(r0, chunk)
        rc, rs, pc, ps = rc_ref[rows, :], rs_ref[rows, :], pc_ref[rows, :], ps_ref[rows, :]
        row = k0 + r0 + lax.broadcasted_iota(jnp.int32, (chunk, h), 0)
        col = lax.broadcasted_iota(jnp.int32, (chunk, h), 1)
        alt_col = (1 - 2 * (col & 1)).astype(F32)
        alt_row = (1 - 2 * (row & 1)).astype(F32)
        se = rs * ce0 + rc * se0
        ce_ref[rows, :] = (rc * ce0 - rs * se0).astype(BF16)
        co_ref[rows, :] = (pc * co0 - ps * so0).astype(BF16)
        sef_ref[rows, :] = jnp.where(row == 0, alt_col, se).astype(BF16)
        sof_ref[rows, :] = jnp.where(row == 0, alt_col, ps * co0 + pc * so0).astype(BF16)
        seg_ref[rows, :] = jnp.where(col == 0, alt_row, se).astype(BF16)
        cog_ref[rows, :] = (rc * cg0 - rs * sg0).astype(BF16)
        sog_ref[rows, :] = jnp.where(col == 0, alt_row, rs * cg0 + rc * sg0).astype(BF16)
        return carry
    lax.fori_loop(0, tm // chunk, tile, 0)


def _dft_tables(seq, tm=256, chunk=32):
    h = seq // 2
    out = jax.ShapeDtypeStruct((h, h), BF16)
    spec = pl.BlockSpec((tm, h), lambda i: (i, 0))
    return pl.pallas_call(
        functools.partial(_dft_kernel, tm=tm, h=h, chunk=chunk),
        grid=(h // tm,),
        out_specs=[spec] * 7,
        out_shape=[out] * 7,
        scratch_shapes=[pltpu.VMEM((tm, h), F32)] * 4,
        compiler_params=_params("arbitrary"),
        name="dft_tables",
    )()


def _in_proj_kernel(x_ref, g_ref, wa_ref, wb_ref, o_ref, h_ref):
    @pl.when(pl.program_id(1) == 0)
    def _():
        x = x_ref[...]
        ms = jnp.mean(x * x, axis=-1, keepdims=True)
        h_ref[...] = (x * lax.rsqrt(ms + EPS) * g_ref[...]).astype(BF16)

    ka = wa_ref.shape[0]
    acc = _dot(h_ref[:, 0:ka], wa_ref[...]) + _dot(h_ref[:, ka:], wb_ref[...])
    o_ref[...] = acc.astype(o_ref.dtype)


def _in_proj(x2d, g, wa, wb, tm=1024, tn=2304):
    m, d = x2d.shape
    n = wa.shape[1]
    assert wa.shape[0] + wb.shape[0] == d
    return pl.pallas_call(
        _in_proj_kernel,
        grid=(m // tm, n // tn),
        in_specs=[pl.BlockSpec((tm, d), lambda i, j: (i, 0)),
                  pl.BlockSpec((1, d), lambda i, j: (0, 0)),
                  pl.BlockSpec((wa.shape[0], tn), lambda i, j: (0, j)),
                  pl.BlockSpec((wb.shape[0], tn), lambda i, j: (0, j))],
        out_specs=pl.BlockSpec((tm, tn), lambda i, j: (i, j)),
        out_shape=jax.ShapeDtypeStruct((m, n), BF16),
        scratch_shapes=[pltpu.VMEM((tm, d), BF16)],
        compiler_params=_params("parallel", "arbitrary", vmem_limit_bytes=V7X_VMEM_LIMIT_LARGE_BYTES),
        name="in_proj",
    )(x2d, g, wa, wb)


def _conv3_kernel(p_ref, w_ref, b_ref, o_ref, scr_ref):
    rows, cols = p_ref.shape[1], p_ref.shape[2]
    h = rows // 2
    pad = 8
    zeros = jnp.zeros((pad, LANES), F32)
    for sl in range(cols // LANES):
        c = slice(sl * LANES, (sl + 1) * LANES)
        scr_ref[sl, 0:pad, :] = zeros
        scr_ref[sl, pad:pad + rows, :] = p_ref[0, :, c].astype(F32)
        scr_ref[sl, pad + rows:, :] = zeros
    for sl in range(cols // LANES):
        c = slice(sl * LANES, (sl + 1) * LANES)
        w0, w1, w2, b = w_ref[0:1, c], w_ref[1:2, c], w_ref[2:3, c], b_ref[:, c]
        xom = scr_ref[sl, pl.ds(pad - 1, h, stride=2), :]
        xe = scr_ref[sl, pl.ds(pad, h, stride=2), :]
        xo = scr_ref[sl, pl.ds(pad + 1, h, stride=2), :]
        xep = scr_ref[sl, pl.ds(pad + 2, h, stride=2), :]
        o_ref[0, 0:h, c] = (w0 * xom + w1 * xe + w2 * xo + b).astype(o_ref.dtype)
        o_ref[0, h:rows, c] = (w0 * xe + w1 * xo + w2 * xep + b).astype(o_ref.dtype)


def _hy_conv3(proj3, w, b, width, ct=1536):
    bsz, seq, _ = proj3.shape
    return pl.pallas_call(
        _conv3_kernel,
        grid=(bsz, width // ct),
        in_specs=[pl.BlockSpec((1, seq, ct), lambda i, j: (i, 0, j)),
                  pl.BlockSpec((HY_SHORT, ct), lambda i, j: (0, j)),
                  pl.BlockSpec((1, ct), lambda i, j: (0, j))],
        out_specs=pl.BlockSpec((1, seq, ct), lambda i, j: (i, 0, j)),
        out_shape=jax.ShapeDtypeStruct((bsz, seq, width), BF16),
        scratch_shapes=[pltpu.VMEM((ct // LANES, seq + 16, LANES), F32)],
        compiler_params=_params("parallel", "parallel"),
        name="hy_conv3",
    )(proj3, w, b)


def _filt_mlp_kernel(mlp_ref, w3f_ref, w3b_ref, w_ref, s_ref, d_ref, tmid_ref, wbf_ref, h2_ref, scr_ref,
                     *, seq, width, ct):
    jc = pl.program_id(1)
    wbf_ref[...] = w_ref[...].astype(BF16)
    n = lax.broadcasted_iota(jnp.int32, (seq, 1), 0).astype(F32)
    t = n / (seq - 1)

    @pl.when((pl.program_id(0) == 0) & (jc == 0))
    def _():
        nl = lax.broadcasted_iota(jnp.int32, (1, seq), 1).astype(F32)
        tl = nl / (seq - 1)
        wang = 2.0 * math.pi * nl / seq
        band = lax.broadcasted_iota(jnp.int32, (HY_EMB_BANDS, 1), 0).astype(F32)
        f = 1e-4 + band * ((HY_EMB_BANDS - 1 - 1e-4) / (HY_EMB_BANDS - 1))
        fw = f * wang
        p = mlp_ref[...]
        hid = p.shape[0]
        nb = HY_EMB_BANDS
        w1t, w1c, w1s = p[:, 0:1], p[:, 1:1 + nb], p[:, 1 + nb:1 + 2 * nb]
        c0 = 1 + 2 * nb
        b1, fr1, b2, fr2 = p[:, c0:c0 + 1], p[:, c0 + 1:c0 + 2], p[:, c0 + 2:c0 + 3], p[:, c0 + 3:c0 + 4]
        w2t = p[:, c0 + 4:c0 + 4 + hid]
        pre1 = w1t * tl + _dot(w1c, jnp.cos(fw)) - _dot(w1s, jnp.sin(fw)) + b1
        h1 = jnp.sin(fr1 * pre1)
        h2t = jnp.sin(fr2 * (_dot(w2t, h1) + b2))
        h2_ref[...] = h2t.T

    h2 = h2_ref[...]
    ch = (jc * ct + lax.broadcasted_iota(jnp.int32, (1, ct), 1)).astype(F32)
    delta = HY_MIN_DECAY + ch * ((HY_MAX_DECAY - HY_MIN_DECAY) / (width - 1))
    decay = jnp.exp(-t * jnp.abs(delta))
    kf = _dot(h2, w3f_ref[...]) * decay
    kb = _dot(h2, w3b_ref[...]) * decay
    row = lax.broadcasted_iota(jnp.int32, (seq, ct), 0)
    kb = jnp.where(row == 0, 0.0, kb)
    l1 = jnp.sum(jnp.abs(kf), axis=0, keepdims=True) + jnp.sum(jnp.abs(kb), axis=0, keepdims=True)
    inv = 1.0 / l1
    s = (kf + kb) * inv
    dm = (kf - kb) * inv
    phase = row & 3
    cmid = jnp.where(phase == 0, 1.0, jnp.where(phase == 2, -1.0, 0.0))
    smid = jnp.where(phase == 1, 1.0, jnp.where(phase == 3, -1.0, 0.0))
    wmid = 2.0 / (2 * seq)
    tmid_ref[0:1, :] = jnp.sum(s * cmid, axis=0, keepdims=True) * wmid
    tmid_ref[1:2, :] = jnp.sum(dm * smid, axis=0, keepdims=True) * (-wmid)
    _parity_split_store(s, scr_ref, s_ref)
    _parity_split_store(dm, scr_ref, d_ref)


def _pack_filter_mlp(w1, b1, fr1, w2, b2, fr2):
    cols = [w1.T, b1.reshape(-1, 1), fr1.reshape(-1, 1), b2.reshape(-1, 1), fr2.reshape(-1, 1), w2.T]
    packed = jnp.concatenate(cols, axis=1)
    pad = -packed.shape[1] % LANES
    return jnp.pad(packed, ((0, 0), (0, pad)))


def _filt_mlp(seq, width, w1, b1, fr1, w2, b2, fr2, w3, w, w_row0, ct=256):
    hid = w2.shape[0]
    mlp = _pack_filter_mlp(w1, b1, fr1, w2, b2, fr2)
    nct = width // ct
    steps = HY_ORDER * nct
    w_rows = w.shape[0] - w_row0
    wr = w_rows // steps
    assert w_row0 % wr == 0
    small = lambda shape: pl.BlockSpec(shape, lambda o, j: (0, 0))
    out_spec = pl.BlockSpec((seq, ct), lambda o, j: (0, o * nct + j))
    return pl.pallas_call(
        functools.partial(_filt_mlp_kernel, seq=seq, width=width, ct=ct),
        grid=(HY_ORDER, nct),
        in_specs=[small(mlp.shape),
                  pl.BlockSpec((hid, ct), lambda o, j: (0, o * 2 * nct + j)),
                  pl.BlockSpec((hid, ct), lambda o, j: (0, o * 2 * nct + nct + j)),
                  pl.BlockSpec((wr, w.shape[1]), lambda o, j: (w_row0 // wr + o * nct + j, 0))],
        out_specs=[out_spec, out_spec, pl.BlockSpec((2, ct), lambda o, j: (0, o * nct + j)),
                   pl.BlockSpec((wr, w.shape[1]), lambda o, j: (o * nct + j, 0))],
        out_shape=[jax.ShapeDtypeStruct((seq, HY_ORDER * width), BF16),
                   jax.ShapeDtypeStruct((seq, HY_ORDER * width), BF16),
                   jax.ShapeDtypeStruct((2, HY_ORDER * width), F32),
                   jax.ShapeDtypeStruct((w_rows, w.shape[1]), BF16)],
        scratch_shapes=[pltpu.VMEM((seq, hid), F32), pltpu.VMEM((ct // LANES, seq, LANES), F32)],
        compiler_params=_params("arbitrary", "arbitrary"),
        name="filt_mlp",
    )(mlp, w3, w3, w)


def _filt_dft_kernel(ce_ref, co_ref, sef_ref, sof_ref, s_ref, d_ref, w_ref, t_ref, wbf_ref, *, tm, seq):
    h = seq // 2
    wbf_ref[...] = w_ref[...].astype(BF16)
    rows = pl.ds(pl.multiple_of(pl.program_id(1) * tm, tm), tm)
    pc = _dot(ce_ref[rows, :], s_ref[0:h, :])
    qc = _dot(co_ref[rows, :], s_ref[h:seq, :])
    ps = _dot(sef_ref[rows, :], d_ref[0:h, :])
    qs = _dot(sof_ref[rows, :], d_ref[h:seq, :])
    row = pl.program_id(1) * tm + lax.broadcasted_iota(jnp.int32, pc.shape, 0)
    wk = jnp.where(row == 0, 1.0, 2.0) * (1.0 / (2 * seq))
    t_ref[0] = ((pc + qc) * wk).astype(BF16)
    t_ref[1] = jnp.where(row == 0, 0.0, -(ps + qs) * wk).astype(BF16)
    t_ref[2] = ((pc - qc) * wk).astype(BF16)
    t_ref[3] = jnp.where(row == 0, 0.0, (ps - qs) * wk).astype(BF16)


def _filt_dft(ce, co, sef, sof, s, d, w, w_rows, tm=512, tn=512):
    h = ce.shape[0]
    seq, cols = s.shape
    ni = h // tm
    wr = w_rows // (cols // tn * ni)
    tab = pl.BlockSpec((seq, tn), lambda j, i: (0, j))
    wspec = pl.BlockSpec((wr, w.shape[1]), lambda j, i: (j * ni + i, 0))
    return pl.pallas_call(
        functools.partial(_filt_dft_kernel, tm=tm, seq=seq),
        grid=(cols // tn, ni),
        in_specs=[_resident((h, h))] * 4 + [tab, tab, wspec],
        out_specs=[pl.BlockSpec((4, tm, tn), lambda j, i: (0, i, j)), wspec],
        out_shape=[jax.ShapeDtypeStruct((4, h, cols), BF16),
                   jax.ShapeDtypeStruct((w_rows, w.shape[1]), BF16)],
        compiler_params=_params("arbitrary", "arbitrary"),
        name="filt_dft",
    )(ce, co, sef, sof, s, d, w)


def _hy_fwd_kernel(ce_ref, co_ref, sef_ref, sof_ref, u_ref, t_ref, tmid_ref, y_ref, *, tm):
    h = ce_ref.shape[0]
    i = pl.program_id(1)
    rows = pl.ds(pl.multiple_of(i * tm, tm), tm)
    ue = u_ref[0, 0:h, :]
    uo = u_ref[0, h:2 * h, :]
    pc = _dot(ce_ref[rows, :], ue)
    qc = _dot(co_ref[rows, :], uo)
    ps = _dot(sef_ref[rows, :], ue)
    qs = _dot(sof_ref[rows, :], uo)
    r0 = (i * tm + lax.broadcasted_iota(jnp.int32, pc.shape, 0)) == 0
    tr1, ti1 = t_ref[0, rows, :].astype(F32), t_ref[1, rows, :].astype(F32)
    tr2, ti2 = t_ref[2, rows, :].astype(F32), t_ref[3, rows, :].astype(F32)
    a1, a2 = pc + qc, pc - qc
    b1, b2 = ps + qs, qs - ps
    yr1, yi1 = a1 * tr1 + b1 * ti1, b1 * tr1 - a1 * ti1
    yr2, yi2 = a2 * tr2 + b2 * ti2, b2 * tr2 - a2 * ti2
    trh, tih = tmid_ref[0:1, :], tmid_ref[1:2, :]
    y_ref[0, 0] = (yr1 + yr2).astype(BF16)
    y_ref[0, 1] = (yr1 - yr2).astype(BF16)
    y_ref[0, 2] = jnp.where(r0, ps * trh + qs * tih, yi1 - yi2).astype(BF16)
    y_ref[0, 3] = jnp.where(r0, qs * trh - ps * tih, yi1 + yi2).astype(BF16)


def _hy_fwd(mats, u3, u_col, tt, tmid, t_col, width, tm=512):
    bsz, seq, _ = u3.shape
    h = seq // 2
    return pl.pallas_call(
        functools.partial(_hy_fwd_kernel, tm=tm),
        grid=(bsz, h // tm),
        in_specs=[_resident((h, h))] * 4
                 + [pl.BlockSpec((1, seq, width), lambda b, i: (b, 0, u_col)),
                    pl.BlockSpec((4, h, width), lambda b, i: (0, 0, t_col), pipeline_mode=pl.Buffered(1)),
                    pl.BlockSpec((2, width), lambda b, i: (0, t_col))],
        out_specs=pl.BlockSpec((1, 4, tm, width), lambda b, i: (b, 0, i, 0)),
        out_shape=jax.ShapeDtypeStruct((bsz, 4, h, width), BF16),
        compiler_params=_params("arbitrary", "arbitrary"),
        name="hy_fwd",
    )(*mats, u3, tt, tmid)


def _hy_inv_kernel(ce_ref, seg_ref, cog_ref, sog_ref, y_ref, g_ref, u_ref, bias_ref, o_ref, *scr,
                   tm, natural, order):
    i = pl.program_id(1)
    rows = pl.ds(pl.multiple_of(i * tm, tm), tm)
    ye = _dot(ce_ref[rows, :], y_ref[0, 0]) + _dot(seg_ref[rows, :], y_ref[0, 2])
    yo = _dot(cog_ref[rows, :], y_ref[0, 1]) + _dot(sog_ref[rows, :], y_ref[0, 3])
    bias = bias_ref[order:order + 1, :]
    ze = g_ref[0, 0].astype(F32) * (ye + u_ref[0, 0].astype(F32) * bias)
    zo = g_ref[0, 1].astype(F32) * (yo + u_ref[0, 1].astype(F32) * bias)
    if natural:
        scr_ref, = scr
        for sl in range(ze.shape[1] // LANES):
            c = slice(sl * LANES, (sl + 1) * LANES)
            scr_ref[sl, pl.ds(0, tm, stride=2), :] = ze[:, c]
            scr_ref[sl, pl.ds(1, tm, stride=2), :] = zo[:, c]
        for sl in range(ze.shape[1] // LANES):
            o_ref[0, :, sl * LANES:(sl + 1) * LANES] = scr_ref[sl].astype(o_ref.dtype)
    else:
        o_ref[0, 0] = ze.astype(o_ref.dtype)
        o_ref[0, 1] = zo.astype(o_ref.dtype)


def _hy_inv(mats, y4, g4, g_col, u4, u_col, bias, order, width, natural, tm=512):
    bsz, _, h, _ = y4.shape
    par = lambda col: pl.BlockSpec((1, 2, tm, width), lambda b, i: (b, 0, i, col))
    if natural:
        out_spec = pl.BlockSpec((1, 2 * tm, width), lambda b, i: (b, i, 0))
        out_shape = jax.ShapeDtypeStruct((bsz, 2 * h, width), BF16)
        scratch = [pltpu.VMEM((width // LANES, 2 * tm, LANES), F32)]
    else:
        out_spec = par(0)
        out_shape = jax.ShapeDtypeStruct((bsz, 2, h, width), BF16)
        scratch = []
    return pl.pallas_call(
        functools.partial(_hy_inv_kernel, tm=tm, natural=natural, order=order),
        grid=(bsz, h // tm),
        in_specs=[_resident((h, h))] * 4
                 + [pl.BlockSpec((1, 4, h, width), lambda b, i: (b, 0, 0, 0)),
                    par(g_col), par(u_col),
                    pl.BlockSpec(bias.shape, lambda b, i: (0, 0))],
        out_specs=out_spec,
        out_shape=out_shape,
        scratch_shapes=scratch,
        compiler_params=_params("arbitrary", "arbitrary"),
        name="hy_inv",
    )(*mats, y4, g4, u4, bias)


def _cf_kernel(*refs, step_rows, n_cast):
    (am_ref, ap_ref, an_ref, bm_ref, bp_ref, bn_ref, w_ref, cb_ref, lg_ref, lb_ref) = refs[:10]
    cast_in = refs[10:10 + n_cast]
    o_ref = refs[10 + n_cast]
    cast_out = refs[11 + n_cast:11 + 2 * n_cast]
    us_ref, cs_ref, wb_ref = refs[11 + 2 * n_cast:]
    width = wb_ref.shape[-1]
    nsl = width // LANES
    j = pl.program_id(1)

    for src, dst in zip(cast_in, cast_out):
        dst[...] = src[...].astype(dst.dtype)

    @pl.when((pl.program_id(0) == 0) & (j == 0))
    def _():
        for t in range(CF_KERNEL):
            wb_ref[t] = jnp.broadcast_to(w_ref[t:t + 1, :], (8, width))

    glu = lambda a, b: a[0].astype(F32) * _sigmoid(b[0].astype(F32))
    u_prev = jnp.where(j == 0, 0.0, glu(ap_ref, bp_ref))
    u_main = glu(am_ref, bm_ref)
    u_next = jnp.where(j == pl.num_programs(1) - 1, 0.0, glu(an_ref, bn_ref))
    for sl in range(nsl):
        c = slice(sl * LANES, (sl + 1) * LANES)
        us_ref[sl, 0:CF_PAD, :] = u_prev[:, c]
        us_ref[sl, CF_PAD:CF_PAD + step_rows, :] = u_main[:, c]
        us_ref[sl, CF_PAD + step_rows:, :] = u_next[:, c]

    half = CF_KERNEL // 2
    nph = 4
    prow = step_rows // nph
    for sl in range(nsl):
        c = slice(sl * LANES, (sl + 1) * LANES)
        accs = [jnp.zeros((prow // 8, 8, LANES), F32) + cb_ref[:, c][None]] * nph
        for r in range(CF_KERNEL + nph - 1):
            tap = us_ref[sl, pl.ds(CF_PAD - half + r, prow, stride=nph), :].reshape(prow // 8, 8, LANES)
            for p in range(nph):
                if 0 <= r - p < CF_KERNEL:
                    accs[p] = accs[p] + wb_ref[r - p, :, c][None] * tap
        for p in range(nph):
            cs_ref[sl, pl.ds(p, prow, stride=nph), :] = accs[p].reshape(prow, LANES)

    tot = cs_ref[0]
    for sl in range(1, nsl):
        tot = tot + cs_ref[sl]
    mu = jnp.sum(tot, axis=-1, keepdims=True) * (1.0 / width)
    sq = None
    for sl in range(nsl):
        cen = cs_ref[sl] - mu
        sq = cen * cen if sq is None else sq + cen * cen
    rstd = lax.rsqrt(jnp.sum(sq, axis=-1, keepdims=True) * (1.0 / width) + EPS)
    for sl in range(nsl):
        c = slice(sl * LANES, (sl + 1) * LANES)
        y = (cs_ref[sl] - mu) * rstd * lg_ref[:, c] + lb_ref[:, c]
        o_ref[0, :, c] = (y * _sigmoid(y)).astype(o_ref.dtype)


def _cf_conv(proj3, a_col, b_col, w, cb, lg, lb, width, casts, step_rows=512):
    bsz, seq, _ = proj3.shape
    nsteps = seq // step_rows
    total = bsz * nsteps
    nblk = step_rows // CF_PAD
    last_blk = seq // CF_PAD - 1
    vec = pl.BlockSpec((1, width), lambda b, j: (0, 0))
    main = lambda col: pl.BlockSpec((1, step_rows, width), lambda b, j: (b, j, col))
    prev = lambda col: pl.BlockSpec((1, CF_PAD, width), lambda b, j: (b, jnp.maximum(j * nblk - 1, 0), col))
    nxt = lambda col: pl.BlockSpec((1, CF_PAD, width),
                                   lambda b, j: (b, jnp.minimum((j + 1) * nblk, last_blk), col))

    def cast_spec(c):
        n = next(n for n in (total, total // 2, total // 4) if c.shape[0] % (16 * n) == 0)
        every = total // n
        return pl.BlockSpec((c.shape[0] // n, c.shape[1]), lambda b, j: ((b * nsteps + j) // every, 0))
    cast_specs = [cast_spec(c) for c in casts]
    outs = pl.pallas_call(
        functools.partial(_cf_kernel, step_rows=step_rows, n_cast=len(casts)),
        grid=(bsz, nsteps),
        in_specs=[main(a_col), prev(a_col), nxt(a_col), main(b_col), prev(b_col), nxt(b_col),
                  pl.BlockSpec((CF_KERNEL, width), lambda b, j: (0, 0)),
                  vec, vec, vec] + cast_specs,
        out_specs=[pl.BlockSpec((1, step_rows, width), lambda b, j: (b, j, 0))] + cast_specs,
        out_shape=[jax.ShapeDtypeStruct((bsz, seq, width), BF16)]
                  + [jax.ShapeDtypeStruct(c.shape, BF16) for c in casts],
        scratch_shapes=[pltpu.VMEM((width // LANES, step_rows + 2 * CF_PAD, LANES), F32),
                        pltpu.VMEM((width // LANES, step_rows, LANES), F32),
                        pltpu.VMEM((CF_KERNEL, 8, width), F32)],
        compiler_params=_params("arbitrary", "arbitrary"),
        name="cf_conv",
    )(proj3, proj3, proj3, proj3, proj3, proj3, w, cb, lg, lb, *casts)
    return outs[0], outs[1:]


def _merge_kernel(ya_ref, yb_ref, ga0_ref, ga1_ref, gb0_ref, gb1_ref, x_ref, pa_ref, pb_ref, wo_ref,
                  g1_ref, x1_ref):
    a = _dot(ya_ref[...], pa_ref[...])
    b = _dot(yb_ref[...], pb_ref[...])
    w = ga0_ref.shape[1]
    m0 = _sigmoid(ga0_ref[...].astype(F32)) * a[:, :w] + _sigmoid(gb0_ref[...].astype(F32)) * b[:, :w]
    m1 = _sigmoid(ga1_ref[...].astype(F32)) * a[:, w:] + _sigmoid(gb1_ref[...].astype(F32)) * b[:, w:]
    o = _dot(m0.astype(BF16), wo_ref[0:w, :]) + _dot(m1.astype(BF16), wo_ref[w:2 * w, :])
    x1_ref[...] = x_ref[...] + o * lax.rsqrt(jnp.mean(o * o, axis=-1, keepdims=True) + EPS) * g1_ref[...]


def _merge(ya, yb, proj, ga_col, gb_col, x2d, pa, pb, wo, g1, tm=256):
    m, d = x2d.shape
    wa = ya.shape[1]
    assert d == 2 * wa
    const = lambda shape: pl.BlockSpec(shape, lambda i: (0, 0))
    gate = lambda col: pl.BlockSpec((tm, wa), lambda i: (i, col))
    return pl.pallas_call(
        _merge_kernel,
        grid=(m // tm,),
        in_specs=[pl.BlockSpec((tm, wa), lambda i: (i, 0)),
                  pl.BlockSpec((tm, wa), lambda i: (i, 0)),
                  gate(ga_col), gate(ga_col + 1), gate(gb_col), gate(gb_col + 1),
                  pl.BlockSpec((tm, d), lambda i: (i, 0)),
                  const((wa, d)), const((wa, d)), const((d, d)), const((1, d))],
        out_specs=pl.BlockSpec((tm, d), lambda i: (i, 0)),
        out_shape=jax.ShapeDtypeStruct((m, d), F32),
        compiler_params=_params("parallel"),
        name="merge",
    )(ya, yb, proj, proj, proj, proj, x2d, pa, pb, wo, g1)


def _ffn_kernel(x1_ref, gpre_ref, wg_ref, wu_ref, wd_ref, gpost_ref, o_ref, h_ref):
    j = pl.program_id(1)

    @pl.when(j == 0)
    def _():
        x1 = x1_ref[...]
        ms = jnp.mean(x1 * x1, axis=-1, keepdims=True)
        h_ref[...] = (x1 * lax.rsqrt(ms + EPS) * gpre_ref[...]).astype(BF16)
        o_ref[...] = jnp.zeros_like(o_ref)

    h = h_ref[...]
    gate = _dot(h, wg_ref[...])
    up = _dot(h, wu_ref[...])
    act = (gate * _sigmoid(gate) * up).astype(BF16)
    o_ref[...] += _dot(act, wd_ref[...])

    @pl.when(j == pl.num_programs(1) - 1)
    def _():
        a = o_ref[...]
        o_ref[...] = x1_ref[...] + a * lax.rsqrt(jnp.mean(a * a, axis=-1, keepdims=True) + EPS) * gpost_ref[...]


def _ffn(x1, gpre, wgu, wd, gpost, tm=1024, th=512):
    m, d = x1.shape
    hidden = wd.shape[0]
    nh = hidden // th
    return pl.pallas_call(
        _ffn_kernel,
        grid=(m // tm, nh),
        in_specs=[pl.BlockSpec((tm, d), lambda i, j: (i, 0)),
                  pl.BlockSpec((1, d), lambda i, j: (0, 0)),
                  pl.BlockSpec((d, th), lambda i, j: (0, j)),
                  pl.BlockSpec((d, th), lambda i, j: (0, nh + j)),
                  pl.BlockSpec((th, d), lambda i, j: (j, 0)),
                  pl.BlockSpec((1, d), lambda i, j: (0, 0))],
        out_specs=pl.BlockSpec((tm, d), lambda i, j: (i, 0)),
        out_shape=jax.ShapeDtypeStruct((m, d), F32),
        scratch_shapes=[pltpu.VMEM((tm, d), BF16)],
        compiler_params=_params("parallel", "arbitrary", vmem_limit_bytes=V7X_VMEM_LIMIT_LARGE_BYTES),
        name="ffn",
    )(x1, gpre, wgu, wgu, wd, gpost)


def kernel(x, mix_pre_g, w_in, hy_conv_w, hy_conv_b, hy_filt_w1, hy_filt_b1, hy_filt_fr1, hy_filt_w2,
           hy_filt_b2, hy_filt_fr2, hy_filt_w3, hy_bias, hy_proj, cf_dw_w, cf_dw_b, cf_ln_g, cf_ln_b,
           cf_proj, w_out, mix_post_g, ffn_pre_g, ffn_w_gu, ffn_w_down, ffn_post_g):
    bsz, seq, d = x.shape
    depth = w_in.shape[0]
    hw = hy_proj.shape[1]
    cw = cf_proj.shape[1]
    assert hw == cw and d % hw == 0
    row = lambda v: v.reshape(1, -1)

    assert depth == 1
    x2d = x.reshape(bsz * seq, d)
    for l in range(depth):
        ce, co, sef, sof, seg, cog, sog = _dft_tables(seq)
        s, dm, tmid, w_in_b = _filt_mlp(seq, hw, hy_filt_w1[l], row(hy_filt_b1[l]), row(hy_filt_fr1[l]),
                                        hy_filt_w2[l], row(hy_filt_b2[l]), row(hy_filt_fr2[l]),
                                        hy_filt_w3[l], w_in[l], d // 2)
        tt, w_in_a = _filt_dft(ce, co, sef, sof, s, dm, w_in[l], d // 2)
        proj = _in_proj(x2d, row(mix_pre_g[l]), w_in_a, w_in_b)
        proj3 = proj.reshape(bsz, seq, -1)
        cf_a_col, cf_b_col = 3, 4
        ga_col, gb_col = 5, 5 + d // hw

        hy = _hy_conv3(proj3, hy_conv_w[l], row(hy_conv_b[l]), 3 * hw)
        hy4 = hy.reshape(bsz, 2, seq // 2, 3 * hw)
        fwd, inv = (ce, co, sef, sof), (ce, seg, cog, sog)
        y1 = _hy_fwd(fwd, hy, 0, tt, tmid, 0, hw)
        z4 = _hy_inv(inv, y1, hy4, 1, hy4, 0, hy_bias[l], 0, hw, natural=False)
        y2 = _hy_fwd(fwd, z4.reshape(bsz, seq, hw), 0, tt, tmid, 1, hw)
        y_a = _hy_inv(inv, y2, hy4, 2, z4, 0, hy_bias[l], 1, hw, natural=True)

        y_b, (hy_proj_bf, cf_proj_bf, w_out_bf, w_gu_bf, w_down_bf) = _cf_conv(
            proj3, cf_a_col, cf_b_col, cf_dw_w[l], row(cf_dw_b[l]), row(cf_ln_g[l]), row(cf_ln_b[l]), cw,
            [hy_proj[l], cf_proj[l], w_out[l], ffn_w_gu[l], ffn_w_down[l]])

        x1 = _merge(y_a.reshape(bsz * seq, hw), y_b.reshape(bsz * seq, cw), proj, ga_col, gb_col, x2d,
                    hy_proj_bf, cf_proj_bf, w_out_bf, row(mix_post_g[l]))
        x2d = _ffn(x1, row(ffn_pre_g[l]), w_gu_bf, w_down_bf, row(ffn_post_g[l]))
    return x2d.reshape(bsz, seq, d)
```

```python
import functools
import math

import jax
import jax.numpy as jnp
from jax import lax
from jax.experimental import pallas as pl
from jax.experimental.pallas import tpu as pltpu

F32 = jnp.float32
BF16 = jnp.bfloat16
EPS = 1e-6

HY_ORDER = 2
HY_EMB_BANDS = 16
HY_DECAY_TARGET = 1e-2
HY_MIN_DECAY = math.log(HY_DECAY_TARGET) / 1.5
HY_MAX_DECAY = math.log(HY_DECAY_TARGET) / 0.3
HY_SHORT = 3
CF_KERNEL = 31
CF_PAD = 16
LANES = 128

V7X_VMEM_LIMIT_BYTES = 56 * 1024 * 1024
V7X_VMEM_LIMIT_LARGE_BYTES = 63 * 1024 * 1024


def _params(*sem, vmem_limit_bytes=V7X_VMEM_LIMIT_BYTES):
    return pltpu.CompilerParams(dimension_semantics=sem, vmem_limit_bytes=vmem_limit_bytes)


def _dot(a, b):
    return jnp.dot(a, b, preferred_element_type=F32)


def _sigmoid(x):
    return 0.5 * jnp.tanh(0.5 * x) + 0.5


def _resident(shape):
    zeros = (0,) * len(shape)
    return pl.BlockSpec(shape, lambda *_: zeros, pipeline_mode=pl.Buffered(1))


def _parity_split_store(val, scr_ref, dst_ref):
    rows, cols = val.shape
    h = rows // 2
    for sl in range(cols // LANES):
        scr_ref[sl] = val[:, sl * LANES:(sl + 1) * LANES]
    for sl in range(cols // LANES):
        c = slice(sl * LANES, (sl + 1) * LANES)
        dst_ref[0:h, c] = scr_ref[sl, pl.ds(0, h, stride=2), :].astype(dst_ref.dtype)
        dst_ref[h:rows, c] = scr_ref[sl, pl.ds(1, h, stride=2), :].astype(dst_ref.dtype)


def _dft_kernel(ce_ref, co_ref, sef_ref, sof_ref, seg_ref, cog_ref, sog_ref,
                rc_ref, rs_ref, pc_ref, ps_ref, *, tm, h, chunk):
    i = pl.program_id(0)
    mask = 4 * h - 1
    scale = 2.0 * math.pi / (4 * h)

    @pl.when(i == 0)
    def _():
        def base(j, carry):
            r0 = pl.multiple_of(j * chunk, chunk)
            r = r0 + lax.broadcasted_iota(jnp.int32, (chunk, h), 0)
            c = lax.broadcasted_iota(jnp.int32, (chunk, h), 1)
            th_e = ((2 * r * c) & mask).astype(F32) * scale
            th_o = ((r * (2 * c + 1)) & mask).astype(F32) * scale
            rc_ref[pl.ds(r0, chunk), :] = jnp.cos(th_e)
            rs_ref[pl.ds(r0, chunk), :] = jnp.sin(th_e)
            pc_ref[pl.ds(r0, chunk), :] = jnp.cos(th_o)
            ps_ref[pl.ds(r0, chunk), :] = jnp.sin(th_o)
            return carry
        lax.fori_loop(0, tm // chunk, base, 0)

    k0 = i * tm
    c1 = lax.broadcasted_iota(jnp.int32, (1, h), 1)
    th = ((2 * k0 * c1) & mask).astype(F32) * scale
    ce0, se0 = jnp.cos(th), jnp.sin(th)
    th = ((k0 * (2 * c1 + 1)) & mask).astype(F32) * scale
    co0, so0 = jnp.cos(th), jnp.sin(th)
    th = (((2 * k0 + 1) * c1) & mask).astype(F32) * scale
    cg0, sg0 = jnp.cos(th), jnp.sin(th)

    def tile(j, carry):
        r0 = pl.multiple_of(j * chunk, chunk)
        rows = pl.ds(r0, chunk)
        rc, rs, pc, ps = rc_ref[rows, :], rs_ref[rows, :], pc_ref[rows, :], ps_ref[rows, :]
        row = k0 + r0 + lax.broadcasted_iota(jnp.int32, (chunk, h), 0)
        col = lax.broadcasted_iota(jnp.int32, (chunk, h), 1)
        alt_col = (1 - 2 * (col & 1)).astype(F32)
        alt_row = (1 - 2 * (row & 1)).astype(F32)
        se = rs * ce0 + rc * se0
        ce_ref[rows, :] = (rc * ce0 - rs * se0).astype(BF16)
        co_ref[rows, :] = (pc * co0 - ps * so0).astype(BF16)
        sef_ref[rows, :] = jnp.where(row == 0, alt_col, se).astype(BF16)
        sof_ref[rows, :] = jnp.where(row == 0, alt_col, ps * co0 + pc * so0).astype(BF16)
        seg_ref[rows, :] = jnp.where(col == 0, alt_row, se).astype(BF16)
        cog_ref[rows, :] = (rc * cg0 - rs * sg0).astype(BF16)
        sog_ref[rows, :] = jnp.where(col == 0, alt_row, rs * cg0 + rc * sg0).astype(BF16)
        return carry
    lax.fori_loop(0, tm // chunk, tile, 0)


def _dft_tables(seq, tm=256, chunk=32):
    h = seq // 2
    out = jax.ShapeDtypeStruct((h, h), BF16)
    spec = pl.BlockSpec((tm, h), lambda i: (i, 0))
    return pl.pallas_call(
        functools.partial(_dft_kernel, tm=tm, h=h, chunk=chunk),
        grid=(h // tm,),
        out_specs=[spec] * 7,
        out_shape=[out] * 7,
        scratch_shapes=[pltpu.VMEM((tm, h), F32)] * 4,
        compiler_params=_params("arbitrary"),
        name="dft_tables",
    )()


def _in_proj_kernel(x_ref, g_ref, wa_ref, wb_ref, o_ref, h_ref):
    @pl.when(pl.program_id(1) == 0)
    def _():
        x = x_ref[...]
        ms = jnp.mean(x * x, axis=-1, keepdims=True)
        h_ref[...] = (x * lax.rsqrt(ms + EPS) * g_ref[...]).astype(BF16)

    ka = wa_ref.shape[0]
    acc = _dot(h_ref[:, 0:ka], wa_ref[...]) + _dot(h_ref[:, ka:], wb_ref[...])
    o_ref[...] = acc.astype(o_ref.dtype)


def _in_proj(x2d, g, wa, wb, tm=1024, tn=2304):
    m, d = x2d.shape
    n = wa.shape[1]
    assert wa.shape[0] + wb.shape[0] == d
    return pl.pallas_call(
        _in_proj_kernel,
        grid=(m // tm, n // tn),
        in_specs=[pl.BlockSpec((tm, d), lambda i, j: (i, 0)),
                  pl.BlockSpec((1, d), lambda i, j: (0, 0)),
                  pl.BlockSpec((wa.shape[0], tn), lambda i, j: (0, j)),
                  pl.BlockSpec((wb.shape[0], tn), lambda i, j: (0, j))],
        out_specs=pl.BlockSpec((tm, tn), lambda i, j: (i, j)),
        out_shape=jax.ShapeDtypeStruct((m, n), BF16),
        scratch_shapes=[pltpu.VMEM((tm, d), BF16)],
        compiler_params=_params("parallel", "arbitrary", vmem_limit_bytes=V7X_VMEM_LIMIT_LARGE_BYTES),
        name="in_proj",
    )(x2d, g, wa, wb)


def _conv3_kernel(p_ref, w_ref, b_ref, o_ref, scr_ref):
    rows, cols = p_ref.shape[1], p_ref.shape[2]
    h = rows // 2
    pad = 8
    zeros = jnp.zeros((pad, LANES), F32)
    for sl in range(cols // LANES):
        c = slice(sl * LANES, (sl + 1) * LANES)
        scr_ref[sl, 0:pad, :] = zeros
        scr_ref[sl, pad:pad + rows, :] = p_ref[0, :, c].astype(F32)
        scr_ref[sl, pad + rows:, :] = zeros
    for sl in range(cols // LANES):
        c = slice(sl * LANES, (sl + 1) * LANES)
        w0, w1, w2, b = w_ref[0:1, c], w_ref[1:2, c], w_ref[2:3, c], b_ref[:, c]
        xom = scr_ref[sl, pl.ds(pad - 1, h, stride=2), :]
        xe = scr_ref[sl, pl.ds(pad, h, stride=2), :]
        xo = scr_ref[sl, pl.ds(pad + 1, h, stride=2), :]
        xep = scr_ref[sl, pl.ds(pad + 2, h, stride=2), :]
        o_ref[0, 0:h, c] = (w0 * xom + w1 * xe + w2 * xo + b).astype(o_ref.dtype)
        o_ref[0, h:rows, c] = (w0 * xe + w1 * xo + w2 * xep + b).astype(o_ref.dtype)


def _hy_conv3(proj3, w, b, width, ct=1536):
    bsz, seq, _ = proj3.shape
    return pl.pallas_call(
        _conv3_kernel,
        grid=(bsz, width // ct),
        in_specs=[pl.BlockSpec((1, seq, ct), lambda i, j: (i, 0, j)),
                  pl.BlockSpec((HY_SHORT, ct), lambda i, j: (0, j)),
                  pl.BlockSpec((1, ct), lambda i, j: (0, j))],
        out_specs=pl.BlockSpec((1, seq, ct), lambda i, j: (i, 0, j)),
        out_shape=jax.ShapeDtypeStruct((bsz, seq, width), BF16),
        scratch_shapes=[pltpu.VMEM((ct // LANES, seq + 16, LANES), F32)],
        compiler_params=_params("parallel", "parallel"),
        name="hy_conv3",
    )(proj3, w, b)


def _filt_mlp_kernel(mlp_ref, w3f_ref, w3b_ref, w_ref, s_ref, d_ref, tmid_ref, wbf_ref, h2_ref, scr_ref,
                     *, seq, width, ct):
    jc = pl.program_id(1)
    wbf_ref[...] = w_ref[...].astype(BF16)
    n = lax.broadcasted_iota(jnp.int32, (seq, 1), 0).astype(F32)
    t = n / (seq - 1)

    @pl.when((pl.program_id(0) == 0) & (jc == 0))
    def _():
        nl = lax.broadcasted_iota(jnp.int32, (1, seq), 1).astype(F32)
        tl = nl / (seq - 1)
        wang = 2.0 * math.pi * nl / seq
        band = lax.broadcasted_iota(jnp.int32, (HY_EMB_BANDS, 1), 0).astype(F32)
        f = 1e-4 + band * ((HY_EMB_BANDS - 1 - 1e-4) / (HY_EMB_BANDS - 1))
        fw = f * wang
        p = mlp_ref[...]
        hid = p.shape[0]
        nb = HY_EMB_BANDS
        w1t, w1c, w1s = p[:, 0:1], p[:, 1:1 + nb], p[:, 1 + nb:1 + 2 * nb]
        c0 = 1 + 2 * nb
        b1, fr1, b2, fr2 = p[:, c0:c0 + 1], p[:, c0 + 1:c0 + 2], p[:, c0 + 2:c0 + 3], p[:, c0 + 3:c0 + 4]
        w2t = p[:, c0 + 4:c0 + 4 + hid]
        pre1 = w1t * tl + _dot(w1c, jnp.cos(fw)) - _dot(w1s, jnp.sin(fw)) + b1
        h1 = jnp.sin(fr1 * pre1)
        h2t = jnp.sin(fr2 * (_dot(w2t, h1) + b2))
        h2_ref[...] = h2t.T

    h2 = h2_ref[...]
    ch = (jc * ct + lax.broadcasted_iota(jnp.int32, (1, ct), 1)).astype(F32)
    delta = HY_MIN_DECAY + ch * ((HY_MAX_DECAY - HY_MIN_DECAY) / (width - 1))
    decay = jnp.exp(-t * jnp.abs(delta))
    kf = _dot(h2, w3f_ref[...]) * decay
    kb = _dot(h2, w3b_ref[...]) * decay
    row = lax.broadcasted_iota(jnp.int32, (seq, ct), 0)
    kb = jnp.where(row == 0, 0.0, kb)
    l1 = jnp.sum(jnp.abs(kf), axis=0, keepdims=True) + jnp.sum(jnp.abs(kb), axis=0, keepdims=True)
    inv = 1.0 / l1
    s = (kf + kb) * inv
    dm = (kf - kb) * inv
    phase = row & 3
    cmid = jnp.where(phase == 0, 1.0, jnp.where(phase == 2, -1.0, 0.0))
    smid = jnp.where(phase == 1, 1.0, jnp.where(phase == 3, -1.0, 0.0))
    wmid = 2.0 / (2 * seq)
    tmid_ref[0:1, :] = jnp.sum(s * cmid, axis=0, keepdims=True) * wmid
    tmid_ref[1:2, :] = jnp.sum(dm * smid, axis=0, keepdims=True) * (-wmid)
    _parity_split_store(s, scr_ref, s_ref)
    _parity_split_store(dm, scr_ref, d_ref)


def _pack_filter_mlp(w1, b1, fr1, w2, b2, fr2):
    cols = [w1.T, b1.reshape(-1, 1), fr1.reshape(-1, 1), b2.reshape(-1, 1), fr2.reshape(-1, 1), w2.T]
    packed = jnp.concatenate(cols, axis=1)
    pad = -packed.shape[1] % LANES
    return jnp.pad(packed, ((0, 0), (0, pad)))


def _filt_mlp(seq, width, w1, b1, fr1, w2, b2, fr2, w3, w, w_row0, ct=256):
    hid = w2.shape[0]
    mlp = _pack_filter_mlp(w1, b1, fr1, w2, b2, fr2)
    nct = width // ct
    steps = HY_ORDER * nct
    w_rows = w.shape[0] - w_row0
    wr = w_rows // steps
    assert w_row0 % wr == 0
    small = lambda shape: pl.BlockSpec(shape, lambda o, j: (0, 0))
    out_spec = pl.BlockSpec((seq, ct), lambda o, j: (0, o * nct + j))
    return pl.pallas_call(
        functools.partial(_filt_mlp_kernel, seq=seq, width=width, ct=ct),
        grid=(HY_ORDER, nct),
        in_specs=[small(mlp.shape),
                  pl.BlockSpec((hid, ct), lambda o, j: (0, o * 2 * nct + j)),
                  pl.BlockSpec((hid, ct), lambda o, j: (0, o * 2 * nct + nct + j)),
                  pl.BlockSpec((wr, w.shape[1]), lambda o, j: (w_row0 // wr + o * nct + j, 0))],
        out_specs=[out_spec, out_spec, pl.BlockSpec((2, ct), lambda o, j: (0, o * nct + j)),
                   pl.BlockSpec((wr, w.shape[1]), lambda o, j: (o * nct + j, 0))],
        out_shape=[jax.ShapeDtypeStruct((seq, HY_ORDER * width), BF16),
                   jax.ShapeDtypeStruct((seq, HY_ORDER * width), BF16),
                   jax.ShapeDtypeStruct((2, HY_ORDER * width), F32),
                   jax.ShapeDtypeStruct((w_rows, w.shape[1]), BF16)],
        scratch_shapes=[pltpu.VMEM((seq, hid), F32), pltpu.VMEM((ct // LANES, seq, LANES), F32)],
        compiler_params=_params("arbitrary", "arbitrary"),
        name="filt_mlp",
    )(mlp, w3, w3, w)


def _filt_dft_kernel(ce_ref, co_ref, sef_ref, sof_ref, s_ref, d_ref, w_ref, t_ref, wbf_ref, *, tm, seq):
    h = seq // 2
    wbf_ref[...] = w_ref[...].astype(BF16)
    rows = pl.ds(pl.multiple_of(pl.program_id(1) * tm, tm), tm)
    pc = _dot(ce_ref[rows, :], s_ref[0:h, :])
    qc = _dot(co_ref[rows, :], s_ref[h:seq, :])
    ps = _dot(sef_ref[rows, :], d_ref[0:h, :])
    qs = _dot(sof_ref[rows, :], d_ref[h:seq, :])
    row = pl.program_id(1) * tm + lax.broadcasted_iota(jnp.int32, pc.shape, 0)
    wk = jnp.where(row == 0, 1.0, 2.0) * (1.0 / (2 * seq))
    t_ref[0] = ((pc + qc) * wk).astype(BF16)
    t_ref[1] = jnp.where(row == 0, 0.0, -(ps + qs) * wk).astype(BF16)
    t_ref[2] = ((pc - qc) * wk).astype(BF16)
    t_ref[3] = jnp.where(row == 0, 0.0, (ps - qs) * wk).astype(BF16)


def _filt_dft(ce, co, sef, sof, s, d, w, w_rows, tm=512, tn=512):
    h = ce.shape[0]
    seq, cols = s.shape
    ni = h // tm
    wr = w_rows // (cols // tn * ni)
    tab = pl.BlockSpec((seq, tn), lambda j, i: (0, j))
    wspec = pl.BlockSpec((wr, w.shape[1]), lambda j, i: (j * ni + i, 0))
    return pl.pallas_call(
        functools.partial(_filt_dft_kernel, tm=tm, seq=seq),
        grid=(cols // tn, ni),
        in_specs=[_resident((h, h))] * 4 + [tab, tab, wspec],
        out_specs=[pl.BlockSpec((4, tm, tn), lambda j, i: (0, i, j)), wspec],
        out_shape=[jax.ShapeDtypeStruct((4, h, cols), BF16),
                   jax.ShapeDtypeStruct((w_rows, w.shape[1]), BF16)],
        compiler_params=_params("arbitrary", "arbitrary"),
        name="filt_dft",
    )(ce, co, sef, sof, s, d, w)


def _hy_fwd_kernel(ce_ref, co_ref, sef_ref, sof_ref, u_ref, t_ref, tmid_ref, y_ref, *, tm):
    h = ce_ref.shape[0]
    i = pl.program_id(1)
    rows = pl.ds(pl.multiple_of(i * tm, tm), tm)
    ue = u_ref[0, 0:h, :]
    uo = u_ref[0, h:2 * h, :]
    pc = _dot(ce_ref[rows, :], ue)
    qc = _dot(co_ref[rows, :], uo)
    ps = _dot(sef_ref[rows, :], ue)
    qs = _dot(sof_ref[rows, :], uo)
    r0 = (i * tm + lax.broadcasted_iota(jnp.int32, pc.shape, 0)) == 0
    tr1, ti1 = t_ref[0, rows, :].astype(F32), t_ref[1, rows, :].astype(F32)
    tr2, ti2 = t_ref[2, rows, :].astype(F32), t_ref[3, rows, :].astype(F32)
    a1, a2 = pc + qc, pc - qc
    b1, b2 = ps + qs, qs - ps
    yr1, yi1 = a1 * tr1 + b1 * ti1, b1 * tr1 - a1 * ti1
    yr2, yi2 = a2 * tr2 + b2 * ti2, b2 * tr2 - a2 * ti2
    trh, tih = tmid_ref[0:1, :], tmid_ref[1:2, :]
    y_ref[0, 0] = (yr1 + yr2).astype(BF16)
    y_ref[0, 1] = (yr1 - yr2).astype(BF16)
    y_ref[0, 2] = jnp.where(r0, ps * trh + qs * tih, yi1 - yi2).astype(BF16)
    y_ref[0, 3] = jnp.where(r0, qs * trh - ps * tih, yi1 + yi2).astype(BF16)


def _hy_fwd(mats, u3, u_col, tt, tmid, t_col, width, tm=512):
    bsz, seq, _ = u3.shape
    h = seq // 2
    return pl.pallas_call(
        functools.partial(_hy_fwd_kernel, tm=tm),
        grid=(bsz, h // tm),
        in_specs=[_resident((h, h))] * 4
                 + [pl.BlockSpec((1, seq, width), lambda b, i: (b, 0, u_col)),
                    pl.BlockSpec((4, h, width), lambda b, i: (0, 0, t_col), pipeline_mode=pl.Buffered(1)),
                    pl.BlockSpec((2, width), lambda b, i: (0, t_col))],
        out_specs=pl.BlockSpec((1, 4, tm, width), lambda b, i: (b, 0, i, 0)),
        out_shape=jax.ShapeDtypeStruct((bsz, 4, h, width), BF16),
        compiler_params=_params("arbitrary", "arbitrary"),
        name="hy_fwd",
    )(*mats, u3, tt, tmid)


def _hy_inv_kernel(ce_ref, seg_ref, cog_ref, sog_ref, y_ref, g_ref, u_ref, bias_ref, o_ref, *scr,
                   tm, natural, order):
    i = pl.program_id(1)
    rows = pl.ds(pl.multiple_of(i * tm, tm), tm)
    ye = _dot(ce_ref[rows, :], y_ref[0, 0]) + _dot(seg_ref[rows, :], y_ref[0, 2])
    yo = _dot(cog_ref[rows, :], y_ref[0, 1]) + _dot(sog_ref[rows, :], y_ref[0, 3])
    bias = bias_ref[order:order + 1, :]
    ze = g_ref[0, 0].astype(F32) * (ye + u_ref[0, 0].astype(F32) * bias)
    zo = g_ref[0, 1].astype(F32) * (yo + u_ref[0, 1].astype(F32) * bias)
    if natural:
        scr_ref, = scr
        for sl in range(ze.shape[1] // LANES):
            c = slice(sl * LANES, (sl + 1) * LANES)
            scr_ref[sl, pl.ds(0, tm, stride=2), :] = ze[:, c]
            scr_ref[sl, pl.ds(1, tm, stride=2), :] = zo[:, c]
        for sl in range(ze.shape[1] // LANES):
            o_ref[0, :, sl * LANES:(sl + 1) * LANES] = scr_ref[sl].astype(o_ref.dtype)
    else:
        o_ref[0, 0] = ze.astype(o_ref.dtype)
        o_ref[0, 1] = zo.astype(o_ref.dtype)


def _hy_inv(mats, y4, g4, g_col, u4, u_col, bias, order, width, natural, tm=512):
    bsz, _, h, _ = y4.shape
    par = lambda col: pl.BlockSpec((1, 2, tm, width), lambda b, i: (b, 0, i, col))
    if natural:
        out_spec = pl.BlockSpec((1, 2 * tm, width), lambda b, i: (b, i, 0))
        out_shape = jax.ShapeDtypeStruct((bsz, 2 * h, width), BF16)
        scratch = [pltpu.VMEM((width // LANES, 2 * tm, LANES), F32)]
    else:
        out_spec = par(0)
        out_shape = jax.ShapeDtypeStruct((bsz, 2, h, width), BF16)
        scratch = []
    return pl.pallas_call(
        functools.partial(_hy_inv_kernel, tm=tm, natural=natural, order=order),
        grid=(bsz, h // tm),
        in_specs=[_resident((h, h))] * 4
                 + [pl.BlockSpec((1, 4, h, width), lambda b, i: (b, 0, 0, 0)),
                    par(g_col), par(u_col),
                    pl.BlockSpec(bias.shape, lambda b, i: (0, 0))],
        out_specs=out_spec,
        out_shape=out_shape,
        scratch_shapes=scratch,
        compiler_params=_params("arbitrary", "arbitrary"),
        name="hy_inv",
    )(*mats, y4, g4, u4, bias)


def _cf_kernel(*refs, step_rows, n_cast):
    (am_ref, ap_ref, an_ref, bm_ref, bp_ref, bn_ref, w_ref, cb_ref, lg_ref, lb_ref) = refs[:10]
    cast_in = refs[10:10 + n_cast]
    o_ref = refs[10 + n_cast]
    cast_out = refs[11 + n_cast:11 + 2 * n_cast]
    us_ref, cs_ref, wb_ref = refs[11 + 2 * n_cast:]
    width = wb_ref.shape[-1]
    nsl = width // LANES
    j = pl.program_id(1)

    for src, dst in zip(cast_in, cast_out):
        dst[...] = src[...].astype(dst.dtype)

    @pl.when((pl.program_id(0) == 0) & (j == 0))
    def _():
        for t in range(CF_KERNEL):
            wb_ref[t] = jnp.broadcast_to(w_ref[t:t + 1, :], (8, width))

    glu = lambda a, b: a[0].astype(F32) * _sigmoid(b[0].astype(F32))
    u_prev = jnp.where(j == 0, 0.0, glu(ap_ref, bp_ref))
    u_main = glu(am_ref, bm_ref)
    u_next = jnp.where(j == pl.num_programs(1) - 1, 0.0, glu(an_ref, bn_ref))
    for sl in range(nsl):
        c = slice(sl * LANES, (sl + 1) * LANES)
        us_ref[sl, 0:CF_PAD, :] = u_prev[:, c]
        us_ref[sl, CF_PAD:CF_PAD + step_rows, :] = u_main[:, c]
        us_ref[sl, CF_PAD + step_rows:, :] = u_next[:, c]

    half = CF_KERNEL // 2
    nph = 4
    prow = step_rows // nph
    for sl in range(nsl):
        c = slice(sl * LANES, (sl + 1) * LANES)
        accs = [jnp.zeros((prow // 8, 8, LANES), F32) + cb_ref[:, c][None]] * nph
        for r in range(CF_KERNEL + nph - 1):
            tap = us_ref[sl, pl.ds(CF_PAD - half + r, prow, stride=nph), :].reshape(prow // 8, 8, LANES)
            for p in range(nph):
                if 0 <= r - p < CF_KERNEL:
                    accs[p] = accs[p] + wb_ref[r - p, :, c][None] * tap
        for p in range(nph):
            cs_ref[sl, pl.ds(p, prow, stride=nph), :] = accs[p].reshape(prow, LANES)

    tot = cs_ref[0]
    for sl in range(1, nsl):
        tot = tot + cs_ref[sl]
    mu = jnp.sum(tot, axis=-1, keepdims=True) * (1.0 / width)
    sq = None
    for sl in range(nsl):
        cen = cs_ref[sl] - mu
        sq = cen * cen if sq is None else sq + cen * cen
    rstd = lax.rsqrt(jnp.sum(sq, axis=-1, keepdims=True) * (1.0 / width) + EPS)
    for sl in range(nsl):
        c = slice(sl * LANES, (sl + 1) * LANES)
        y = (cs_ref[sl] - mu) * rstd * lg_ref[:, c] + lb_ref[:, c]
        o_ref[0, :, c] = (y * _sigmoid(y)).astype(o_ref.dtype)


def _cf_conv(proj3, a_col, b_col, w, cb, lg, lb, width, casts, step_rows=512):
    bsz, seq, _ = proj3.shape
    nsteps = seq // step_rows
    total = bsz * nsteps
    nblk = step_rows // CF_PAD
    last_blk = seq // CF_PAD - 1
    vec = pl.BlockSpec((1, width), lambda b, j: (0, 0))
    main = lambda col: pl.BlockSpec((1, step_rows, width), lambda b, j: (b, j, col))
    prev = lambda col: pl.BlockSpec((1, CF_PAD, width), lambda b, j: (b, jnp.maximum(j * nblk - 1, 0), col))
    nxt = lambda col: pl.BlockSpec((1, CF_PAD, width),
                                   lambda b, j: (b, jnp.minimum((j + 1) * nblk, last_blk), col))

    def cast_spec(c):
        n = next(n for n in (total, total // 2, total // 4) if c.shape[0] % (16 * n) == 0)
        every = total // n
        return pl.BlockSpec((c.shape[0] // n, c.shape[1]), lambda b, j: ((b * nsteps + j) // every, 0))
    cast_specs = [cast_spec(c) for c in casts]
    outs = pl.pallas_call(
        functools.partial(_cf_kernel, step_rows=step_rows, n_cast=len(casts)),
        grid=(bsz, nsteps),
        in_specs=[main(a_col), prev(a_col), nxt(a_col), main(b_col), prev(b_col), nxt(b_col),
                  pl.BlockSpec((CF_KERNEL, width), lambda b, j: (0, 0)),
                  vec, vec, vec] + cast_specs,
        out_specs=[pl.BlockSpec((1, step_rows, width), lambda b, j: (b, j, 0))] + cast_specs,
        out_shape=[jax.ShapeDtypeStruct((bsz, seq, width), BF16)]
                  + [jax.ShapeDtypeStruct(c.shape, BF16) for c in casts],
        scratch_shapes=[pltpu.VMEM((width // LANES, step_rows + 2 * CF_PAD, LANES), F32),
                        pltpu.VMEM((width // LANES, step_rows, LANES), F32),
                        pltpu.VMEM((CF_KERNEL, 8, width), F32)],
        compiler_params=_params("arbitrary", "arbitrary"),
        name="cf_conv",
    )(proj3, proj3, proj3, proj3, proj3, proj3, w, cb, lg, lb, *casts)
    return outs[0], outs[1:]


def _merge_kernel(ya_ref, yb_ref, ga0_ref, ga1_ref, gb0_ref, gb1_ref, x_ref, pa_ref, pb_ref, wo_ref,
                  g1_ref, x1_ref):
    a = _dot(ya_ref[...], pa_ref[...])
    b = _dot(yb_ref[...], pb_ref[...])
    w = ga0_ref.shape[1]
    m0 = _sigmoid(ga0_ref[...].astype(F32)) * a[:, :w] + _sigmoid(gb0_ref[...].astype(F32)) * b[:, :w]
    m1 = _sigmoid(ga1_ref[...].astype(F32)) * a[:, w:] + _sigmoid(gb1_ref[...].astype(F32)) * b[:, w:]
    o = _dot(m0.astype(BF16), wo_ref[0:w, :]) + _dot(m1.astype(BF16), wo_ref[w:2 * w, :])
    x1_ref[...] = x_ref[...] + o * lax.rsqrt(jnp.mean(o * o, axis=-1, keepdims=True) + EPS) * g1_ref[...]


def _merge(ya, yb, proj, ga_col, gb_col, x2d, pa, pb, wo, g1, tm=256):
    m, d = x2d.shape
    wa = ya.shape[1]
    assert d == 2 * wa
    const = lambda shape: pl.BlockSpec(shape, lambda i: (0, 0))
    gate = lambda col: pl.BlockSpec((tm, wa), lambda i: (i, col))
    return pl.pallas_call(
        _merge_kernel,
        grid=(m // tm,),
        in_specs=[pl.BlockSpec((tm, wa), lambda i: (i, 0)),
                  pl.BlockSpec((tm, wa), lambda i: (i, 0)),
                  gate(ga_col), gate(ga_col + 1), gate(gb_col), gate(gb_col + 1),
                  pl.BlockSpec((tm, d), lambda i: (i, 0)),
                  const((wa, d)), const((wa, d)), const((d, d)), const((1, d))],
        out_specs=pl.BlockSpec((tm, d), lambda i: (i, 0)),
        out_shape=jax.ShapeDtypeStruct((m, d), F32),
        compiler_params=_params("parallel"),
        name="merge",
    )(ya, yb, proj, proj, proj, proj, x2d, pa, pb, wo, g1)


def _ffn_kernel(x1_ref, gpre_ref, wg_ref, wu_ref, wd_ref, gpost_ref, o_ref, h_ref):
    j = pl.program_id(1)

    @pl.when(j == 0)
    def _():
        x1 = x1_ref[...]
        ms = jnp.mean(x1 * x1, axis=-1, keepdims=True)
        h_ref[...] = (x1 * lax.rsqrt(ms + EPS) * gpre_ref[...]).astype(BF16)
        o_ref[...] = jnp.zeros_like(o_ref)

    h = h_ref[...]
    gate = _dot(h, wg_ref[...])
    up = _dot(h, wu_ref[...])
    act = (gate * _sigmoid(gate) * up).astype(BF16)
    o_ref[...] += _dot(act, wd_ref[...])

    @pl.when(j == pl.num_programs(1) - 1)
    def _():
        a = o_ref[...]
        o_ref[...] = x1_ref[...] + a * lax.rsqrt(jnp.mean(a * a, axis=-1, keepdims=True) + EPS) * gpost_ref[...]


def _ffn(x1, gpre, wgu, wd, gpost, tm=1024, th=512):
    m, d = x1.shape
    hidden = wd.shape[0]
    nh = hidden // th
    return pl.pallas_call(
        _ffn_kernel,
        grid=(m // tm, nh),
        in_specs=[pl.BlockSpec((tm, d), lambda i, j: (i, 0)),
                  pl.BlockSpec((1, d), lambda i, j: (0, 0)),
                  pl.BlockSpec((d, th), lambda i, j: (0, j)),
                  pl.BlockSpec((d, th), lambda i, j: (0, nh + j)),
                  pl.BlockSpec((th, d), lambda i, j: (j, 0)),
                  pl.BlockSpec((1, d), lambda i, j: (0, 0))],
        out_specs=pl.BlockSpec((tm, d), lambda i, j: (i, 0)),
        out_shape=jax.ShapeDtypeStruct((m, d), F32),
        scratch_shapes=[pltpu.VMEM((tm, d), BF16)],
        compiler_params=_params("parallel", "arbitrary", vmem_limit_bytes=V7X_VMEM_LIMIT_LARGE_BYTES),
        name="ffn",
    )(x1, gpre, wgu, wgu, wd, gpost)


def kernel(x, mix_pre_g, w_in, hy_conv_w, hy_conv_b, hy_filt_w1, hy_filt_b1, hy_filt_fr1, hy_filt_w2,
           hy_filt_b2, hy_filt_fr2, hy_filt_w3, hy_bias, hy_proj, cf_dw_w, cf_dw_b, cf_ln_g, cf_ln_b,
           cf_proj, w_out, mix_post_g, ffn_pre_g, ffn_w_gu, ffn_w_down, ffn_post_g):
    bsz, seq, d = x.shape
    depth = w_in.shape[0]
    hw = hy_proj.shape[1]
    cw = cf_proj.shape[1]
    assert hw == cw and d % hw == 0
    row = lambda v: v.reshape(1, -1)

    assert depth == 1
    x2d = x.reshape(bsz * seq, d)
    for l in range(depth):
        ce, co, sef, sof, seg, cog, sog = _dft_tables(seq)
        s, dm, tmid, w_in_b = _filt_mlp(seq, hw, hy_filt_w1[l], row(hy_filt_b1[l]), row(hy_filt_fr1[l]),
                                        hy_filt_w2[l], row(hy_filt_b2[l]), row(hy_filt_fr2[l]),
                                        hy_filt_w3[l], w_in[l], d // 2)
        tt, w_in_a = _filt_dft(ce, co, sef, sof, s, dm, w_in[l], d // 2)
        proj = _in_proj(x2d, row(mix_pre_g[l]), w_in_a, w_in_b)
        proj3 = proj.reshape(bsz, seq, -1)
        cf_a_col, cf_b_col = 3, 4
        ga_col, gb_col = 5, 5 + d // hw

        hy = _hy_conv3(proj3, hy_conv_w[l], row(hy_conv_b[l]), 3 * hw)
        hy4 = hy.reshape(bsz, 2, seq // 2, 3 * hw)
        fwd, inv = (ce, co, sef, sof), (ce, seg, cog, sog)
        y1 = _hy_fwd(fwd, hy, 0, tt, tmid, 0, hw)
        z4 = _hy_inv(inv, y1, hy4, 1, hy4, 0, hy_bias[l], 0, hw, natural=False)
        y2 = _hy_fwd(fwd, z4.reshape(bsz, seq, hw), 0, tt, tmid, 1, hw)
        y_a = _hy_inv(inv, y2, hy4, 2, z4, 0, hy_bias[l], 1, hw, natural=True)

        y_b, (hy_proj_bf, cf_proj_bf, w_out_bf, w_gu_bf, w_down_bf) = _cf_conv(
            proj3, cf_a_col, cf_b_col, cf_dw_w[l], row(cf_dw_b[l]), row(cf_ln_g[l]), row(cf_ln_b[l]), cw,
            [hy_proj[l], cf_proj[l], w_out[l], ffn_w_gu[l], ffn_w_down[l]])

        x1 = _merge(y_a.reshape(bsz * seq, hw), y_b.reshape(bsz * seq, cw), proj, ga_col, gb_col, x2d,
                    hy_proj_bf, cf_proj_bf, w_out_bf, row(mix_post_g[l]))
        x2d = _ffn(x1, row(ffn_pre_g[l]), w_gu_bf, w_down_bf, row(ffn_post_g[l]))
    return x2d.reshape(bsz, seq, d)
```

```python
import functools
import math

import jax
import jax.numpy as jnp
from jax import lax
from jax.experimental import pallas as pl
from jax.experimental.pallas import tpu as pltpu

F32 = jnp.float32
BF16 = jnp.bfloat16
EPS = 1e-6

HY_ORDER = 2
HY_EMB_BANDS = 16
HY_DECAY_TARGET = 1e-2
HY_MIN_DECAY = math.log(HY_DECAY_TARGET) / 1.5
HY_MAX_DECAY = math.log(HY_DECAY_TARGET) / 0.3
HY_SHORT = 3
CF_KERNEL = 31
CF_PAD = 16
LANES = 128

V7X_VMEM_LIMIT_BYTES = 56 * 1024 * 1024
V7X_VMEM_LIMIT_LARGE_BYTES = 63 * 1024 * 1024


def _params(*sem, vmem_limit_bytes=V7X_VMEM_LIMIT_BYTES):
    return pltpu.CompilerParams(dimension_semantics=sem, vmem_limit_bytes=vmem_limit_bytes)


def _dot(a, b):
    return jnp.dot(a, b, preferred_element_type=F32)


def _sigmoid(x):
    return 0.5 * jnp.tanh(0.5 * x) + 0.5


def _resident(shape):
    zeros = (0,) * len(shape)
    return pl.BlockSpec(shape, lambda *_: zeros, pipeline_mode=pl.Buffered(1))


def _parity_split_store(val, scr_ref, dst_ref):
    rows, cols = val.shape
    h = rows // 2
    for sl in range(cols // LANES):
        scr_ref[sl] = val[:, sl * LANES:(sl + 1) * LANES]
    for sl in range(cols // LANES):
        c = slice(sl * LANES, (sl + 1) * LANES)
        dst_ref[0:h, c] = scr_ref[sl, pl.ds(0, h, stride=2), :].astype(dst_ref.dtype)
        dst_ref[h:rows, c] = scr_ref[sl, pl.ds(1, h, stride=2), :].astype(dst_ref.dtype)


def _dft_kernel(ce_ref, co_ref, sef_ref, sof_ref, seg_ref, cog_ref, sog_ref,
                rc_ref, rs_ref, pc_ref, ps_ref, *, tm, h, chunk):
    i = pl.program_id(0)
    mask = 4 * h - 1
    scale = 2.0 * math.pi / (4 * h)

    @pl.when(i == 0)
    def _():
        def base(j, carry):
            r0 = pl.multiple_of(j * chunk, chunk)
            r = r0 + lax.broadcasted_iota(jnp.int32, (chunk, h), 0)
            c = lax.broadcasted_iota(jnp.int32, (chunk, h), 1)
            th_e = ((2 * r * c) & mask).astype(F32) * scale
            th_o = ((r * (2 * c + 1)) & mask).astype(F32) * scale
            rc_ref[pl.ds(r0, chunk), :] = jnp.cos(th_e)
            rs_ref[pl.ds(r0, chunk), :] = jnp.sin(th_e)
            pc_ref[pl.ds(r0, chunk), :] = jnp.cos(th_o)
            ps_ref[pl.ds(r0, chunk), :] = jnp.sin(th_o)
            return carry
        lax.fori_loop(0, tm // chunk, base, 0)

    k0 = i * tm
    c1 = lax.broadcasted_iota(jnp.int32, (1, h), 1)
    th = ((2 * k0 * c1) & mask).astype(F32) * scale
    ce0, se0 = jnp.cos(th), jnp.sin(th)
    th = ((k0 * (2 * c1 + 1)) & mask).astype(F32) * scale
    co0, so0 = jnp.cos(th), jnp.sin(th)
    th = (((2 * k0 + 1) * c1) & mask).astype(F32) * scale
    cg0, sg0 = jnp.cos(th), jnp.sin(th)

    def tile(j, carry):
        r0 = pl.multiple_of(j * chunk, chunk)
        rows = pl.ds(r0, chunk)
        rc, rs, pc, ps = rc_ref[rows, :], rs_ref[rows, :], pc_ref[rows, :], ps_ref[rows, :]
        row = k0 + r0 + lax.broadcasted_iota(jnp.int32, (chunk, h), 0)
        col = lax.broadcasted_iota(jnp.int32, (chunk, h), 1)
        alt_col = (1 - 2 * (col & 1)).astype(F32)
        alt_row = (1 - 2 * (row & 1)).astype(F32)
        se = rs * ce0 + rc * se0
        ce_ref[rows, :] = (rc * ce0 - rs * se0).astype(BF16)
        co_ref[rows, :] = (pc * co0 - ps * so0).astype(BF16)
        sef_ref[rows, :] = jnp.where(row == 0, alt_col, se).astype(BF16)
        sof_ref[rows, :] = jnp.where(row == 0, alt_col, ps * co0 + pc * so0).astype(BF16)
        seg_ref[rows, :] = jnp.where(col == 0, alt_row, se).astype(BF16)
        cog_ref[rows, :] = (rc * cg0 - rs * sg0).astype(BF16)
        sog_ref[rows, :] = jnp.where(col == 0, alt_row, rs * cg0 + rc * sg0).astype(BF16)
        return carry
    lax.fori_loop(0, tm // chunk, tile, 0)


def _dft_tables(seq, tm=256, chunk=32):
    h = seq // 2
    out = jax.ShapeDtypeStruct((h, h), BF16)
    spec = pl.BlockSpec((tm, h), lambda i: (i, 0))
    return pl.pallas_call(
        functools.partial(_dft_kernel, tm=tm, h=h, chunk=chunk),
        grid=(h // tm,),
        out_specs=[spec] * 7,
        out_shape=[out] * 7,
        scratch_shapes=[pltpu.VMEM((tm, h), F32)] * 4,
        compiler_params=_params("arbitrary"),
        name="dft_tables",
    )()


def _in_proj_kernel(x_ref, g_ref, wa_ref, wb_ref, o_ref, h_ref):
    @pl.when(pl.program_id(1) == 0)
    def _():
        x = x_ref[...]
        ms = jnp.mean(x * x, axis=-1, keepdims=True)
        h_ref[...] = (x * lax.rsqrt(ms + EPS) * g_ref[...]).astype(BF16)

    ka = wa_ref.shape[0]
    acc = _dot(h_ref[:, 0:ka], wa_ref[...]) + _dot(h_ref[:, ka:], wb_ref[...])
    o_ref[...] = acc.astype(o_ref.dtype)


def _in_proj(x2d, g, wa, wb, tm=1024, tn=2304):
    m, d = x2d.shape
    n = wa.shape[1]
    assert wa.shape[0] + wb.shape[0] == d
    return pl.pallas_call(
        _in_proj_kernel,
        grid=(m // tm, n // tn),
        in_specs=[pl.BlockSpec((tm, d), lambda i, j: (i, 0)),
                  pl.BlockSpec((1, d), lambda i, j: (0, 0)),
                  pl.BlockSpec((wa.shape[0], tn), lambda i, j: (0, j)),
                  pl.BlockSpec((wb.shape[0], tn), lambda i, j: (0, j))],
        out_specs=pl.BlockSpec((tm, tn), lambda i, j: (i, j)),
        out_shape=jax.ShapeDtypeStruct((m, n), BF16),
        scratch_shapes=[pltpu.VMEM((tm, d), BF16)],
        compiler_params=_params("parallel", "arbitrary", vmem_limit_bytes=V7X_VMEM_LIMIT_LARGE_BYTES),
        name="in_proj",
    )(x2d, g, wa, wb)


def _conv3_kernel(p_ref, w_ref, b_ref, o_ref, scr_ref):
    rows, cols = p_ref.shape[1], p_ref.shape[2]
    h = rows // 2
    pad = 8
    zeros = jnp.zeros((pad, LANES), F32)
    for sl in range(cols // LANES):
        c = slice(sl * LANES, (sl + 1) * LANES)
        scr_ref[sl, 0:pad, :] = zeros
        scr_ref[sl, pad:pad + rows, :] = p_ref[0, :, c].astype(F32)
        scr_ref[sl, pad + rows:, :] = zeros
    for sl in range(cols // LANES):
        c = slice(sl * LANES, (sl + 1) * LANES)
        w0, w1, w2, b = w_ref[0:1, c], w_ref[1:2, c], w_ref[2:3, c], b_ref[:, c]
        xom = scr_ref[sl, pl.ds(pad - 1, h, stride=2), :]
        xe = scr_ref[sl, pl.ds(pad, h, stride=2), :]
        xo = scr_ref[sl, pl.ds(pad + 1, h, stride=2), :]
        xep = scr_ref[sl, pl.ds(pad + 2, h, stride=2), :]
        o_ref[0, 0:h, c] = (w0 * xom + w1 * xe + w2 * xo + b).astype(o_ref.dtype)
        o_ref[0, h:rows, c] = (w0 * xe + w1 * xo + w2 * xep + b).astype(o_ref.dtype)


def _hy_conv3(proj3, w, b, width, ct=1536):
    bsz, seq, _ = proj3.shape
    return pl.pallas_call(
        _conv3_kernel,
        grid=(bsz, width // ct),
        in_specs=[pl.BlockSpec((1, seq, ct), lambda i, j: (i, 0, j)),
                  pl.BlockSpec((HY_SHORT, ct), lambda i, j: (0, j)),
                  pl.BlockSpec((1, ct), lambda i, j: (0, j))],
        out_specs=pl.BlockSpec((1, seq, ct), lambda i, j: (i, 0, j)),
        out_shape=jax.ShapeDtypeStruct((bsz, seq, width), BF16),
        scratch_shapes=[pltpu.VMEM((ct // LANES, seq + 16, LANES), F32)],
        compiler_params=_params("parallel", "parallel"),
        name="hy_conv3",
    )(proj3, w, b)


def _filt_mlp_kernel(mlp_ref, w3f_ref, w3b_ref, w_ref, s_ref, d_ref, tmid_ref, wbf_ref, h2_ref, scr_ref,
                     *, seq, width, ct):
    jc = pl.program_id(1)
    wbf_ref[...] = w_ref[...].astype(BF16)
    n = lax.broadcasted_iota(jnp.int32, (seq, 1), 0).astype(F32)
    t = n / (seq - 1)

    @pl.when((pl.program_id(0) == 0) & (jc == 0))
    def _():
        nl = lax.broadcasted_iota(jnp.int32, (1, seq), 1).astype(F32)
        tl = nl / (seq - 1)
        wang = 2.0 * math.pi * nl / seq
        band = lax.broadcasted_iota(jnp.int32, (HY_EMB_BANDS, 1), 0).astype(F32)
        f = 1e-4 + band * ((HY_EMB_BANDS - 1 - 1e-4) / (HY_EMB_BANDS - 1))
        fw = f * wang
        p = mlp_ref[...]
        hid = p.shape[0]
        nb = HY_EMB_BANDS
        w1t, w1c, w1s = p[:, 0:1], p[:, 1:1 + nb], p[:, 1 + nb:1 + 2 * nb]
        c0 = 1 + 2 * nb
        b1, fr1, b2, fr2 = p[:, c0:c0 + 1], p[:, c0 + 1:c0 + 2], p[:, c0 + 2:c0 + 3], p[:, c0 + 3:c0 + 4]
        w2t = p[:, c0 + 4:c0 + 4 + hid]
        pre1 = w1t * tl + _dot(w1c, jnp.cos(fw)) - _dot(w1s, jnp.sin(fw)) + b1
        h1 = jnp.sin(fr1 * pre1)
        h2t = jnp.sin(fr2 * (_dot(w2t, h1) + b2))
        h2_ref[...] = h2t.T

    h2 = h2_ref[...]
    ch = (jc * ct + lax.broadcasted_iota(jnp.int32, (1, ct), 1)).astype(F32)
    delta = HY_MIN_DECAY + ch * ((HY_MAX_DECAY - HY_MIN_DECAY) / (width - 1))
    decay = jnp.exp(-t * jnp.abs(delta))
    kf = _dot(h2, w3f_ref[...]) * decay
    kb = _dot(h2, w3b_ref[...]) * decay
    row = lax.broadcasted_iota(jnp.int32, (seq, ct), 0)
    kb = jnp.where(row == 0, 0.0, kb)
    l1 = jnp.sum(jnp.abs(kf), axis=0, keepdims=True) + jnp.sum(jnp.abs(kb), axis=0, keepdims=True)
    inv = 1.0 / l1
    s = (kf + kb) * inv
    dm = (kf - kb) * inv
    phase = row & 3
    cmid = jnp.where(phase == 0, 1.0, jnp.where(phase == 2, -1.0, 0.0))
    smid = jnp.where(phase == 1, 1.0, jnp.where(phase == 3, -1.0, 0.0))
    wmid = 2.0 / (2 * seq)
    tmid_ref[0:1, :] = jnp.sum(s * cmid, axis=0, keepdims=True) * wmid
    tmid_ref[1:2, :] = jnp.sum(dm * smid, axis=0, keepdims=True) * (-wmid)
    _parity_split_store(s, scr_ref, s_ref)
    _parity_split_store(dm, scr_ref, d_ref)


def _pack_filter_mlp(w1, b1, fr1, w2, b2, fr2):
    cols = [w1.T, b1.reshape(-1, 1), fr1.reshape(-1, 1), b2.reshape(-1, 1), fr2.reshape(-1, 1), w2.T]
    packed = jnp.concatenate(cols, axis=1)
    pad = -packed.shape[1] % LANES
    return jnp.pad(packed, ((0, 0), (0, pad)))


def _filt_mlp(seq, width, w1, b1, fr1, w2, b2, fr2, w3, w, w_row0, ct=256):
    hid = w2.shape[0]
    mlp = _pack_filter_mlp(w1, b1, fr1, w2, b2, fr2)
    nct = width // ct
    steps = HY_ORDER * nct
    w_rows = w.shape[0] - w_row0
    wr = w_rows // steps
    assert w_row0 % wr == 0
    small = lambda shape: pl.BlockSpec(shape, lambda o, j: (0, 0))
    out_spec = pl.BlockSpec((seq, ct), lambda o, j: (0, o * nct + j))
    return pl.pallas_call(
        functools.partial(_filt_mlp_kernel, seq=seq, width=width, ct=ct),
        grid=(HY_ORDER, nct),
        in_specs=[small(mlp.shape),
                  pl.BlockSpec((hid, ct), lambda o, j: (0, o * 2 * nct + j)),
                  pl.BlockSpec((hid, ct), lambda o, j: (0, o * 2 * nct + nct + j)),
                  pl.BlockSpec((wr, w.shape[1]), lambda o, j: (w_row0 // wr + o * nct + j, 0))],
        out_specs=[out_spec, out_spec, pl.BlockSpec((2, ct), lambda o, j: (0, o * nct + j)),
                   pl.BlockSpec((wr, w.shape[1]), lambda o, j: (o * nct + j, 0))],
        out_shape=[jax.ShapeDtypeStruct((seq, HY_ORDER * width), BF16),
                   jax.ShapeDtypeStruct((seq, HY_ORDER * width), BF16),
                   jax.ShapeDtypeStruct((2, HY_ORDER * width), F32),
                   jax.ShapeDtypeStruct((w_rows, w.shape[1]), BF16)],
        scratch_shapes=[pltpu.VMEM((seq, hid), F32), pltpu.VMEM((ct // LANES, seq, LANES), F32)],
        compiler_params=_params("arbitrary", "arbitrary"),
        name="filt_mlp",
    )(mlp, w3, w3, w)


def _filt_dft_kernel(ce_ref, co_ref, sef_ref, sof_ref, s_ref, d_ref, w_ref, t_ref, wbf_ref, *, tm, seq):
    h = seq // 2
    wbf_ref[...] = w_ref[...].astype(BF16)
    rows = pl.ds(pl.multiple_of(pl.program_id(1) * tm, tm), tm)
    pc = _dot(ce_ref[rows, :], s_ref[0:h, :])
    qc = _dot(co_ref[rows, :], s_ref[h:seq, :])
    ps = _dot(sef_ref[rows, :], d_ref[0:h, :])
    qs = _dot(sof_ref[rows, :], d_ref[h:seq, :])
    row = pl.program_id(1) * tm + lax.broadcasted_iota(jnp.int32, pc.shape, 0)
    wk = jnp.where(row == 0, 1.0, 2.0) * (1.0 / (2 * seq))
    t_ref[0] = ((pc + qc) * wk).astype(BF16)
    t_ref[1] = jnp.where(row == 0, 0.0, -(ps + qs) * wk).astype(BF16)
    t_ref[2] = ((pc - qc) * wk).astype(BF16)
    t_ref[3] = jnp.where(row == 0, 0.0, (ps - qs) * wk).astype(BF16)


def _filt_dft(ce, co, sef, sof, s, d, w, w_rows, tm=512, tn=512):
    h = ce.shape[0]
    seq, cols = s.shape
    ni = h // tm
    wr = w_rows // (cols // tn * ni)
    tab = pl.BlockSpec((seq, tn), lambda j, i: (0, j))
    wspec = pl.BlockSpec((wr, w.shape[1]), lambda j, i: (j * ni + i, 0))
    return pl.pallas_call(
        functools.partial(_filt_dft_kernel, tm=tm, seq=seq),
        grid=(cols // tn, ni),
        in_specs=[_resident((h, h))] * 4 + [tab, tab, wspec],
        out_specs=[pl.BlockSpec((4, tm, tn), lambda j, i: (0, i, j)), wspec],
        out_shape=[jax.ShapeDtypeStruct((4, h, cols), BF16),
                   jax.ShapeDtypeStruct((w_rows, w.shape[1]), BF16)],
        compiler_params=_params("arbitrary", "arbitrary"),
        name="filt_dft",
    )(ce, co, sef, sof, s, d, w)


def _hy_fwd_kernel(ce_ref, co_ref, sef_ref, sof_ref, u_ref, t_ref, tmid_ref, y_ref, *, tm):
    h = ce_ref.shape[0]
    i = pl.program_id(1)
    rows = pl.ds(pl.multiple_of(i * tm, tm), tm)
    ue = u_ref[0, 0:h, :]
    uo = u_ref[0, h:2 * h, :]
    pc = _dot(ce_ref[rows, :], ue)
    qc = _dot(co_ref[rows, :], uo)
    ps = _dot(sef_ref[rows, :], ue)
    qs = _dot(sof_ref[rows, :], uo)
    r0 = (i * tm + lax.broadcasted_iota(jnp.int32, pc.shape, 0)) == 0
    tr1, ti1 = t_ref[0].astype(F32), t_ref[1].astype(F32)
    tr2, ti2 = t_ref[2].astype(F32), t_ref[3].astype(F32)
    a1, a2 = pc + qc, pc - qc
    b1, b2 = ps + qs, qs - ps
    yr1, yi1 = a1 * tr1 + b1 * ti1, b1 * tr1 - a1 * ti1
    yr2, yi2 = a2 * tr2 + b2 * ti2, b2 * tr2 - a2 * ti2
    trh, tih = tmid_ref[0:1, :], tmid_ref[1:2, :]
    y_ref[0, 0] = (yr1 + yr2).astype(BF16)
    y_ref[0, 1] = (yr1 - yr2).astype(BF16)
    y_ref[0, 2] = jnp.where(r0, ps * trh + qs * tih, yi1 - yi2).astype(BF16)
    y_ref[0, 3] = jnp.where(r0, qs * trh - ps * tih, yi1 + yi2).astype(BF16)


def _hy_fwd(mats, u3, u_col, tt, tmid, t_col, width, tm=512):
    bsz, seq, _ = u3.shape
    h = seq // 2
    return pl.pallas_call(
        functools.partial(_hy_fwd_kernel, tm=tm),
        grid=(bsz, h // tm),
        in_specs=[_resident((h, h))] * 4
                 + [pl.BlockSpec((1, seq, width), lambda b, i: (b, 0, u_col)),
                    pl.BlockSpec((4, tm, width), lambda b, i: (0, i, t_col)),
                    pl.BlockSpec((2, width), lambda b, i: (0, t_col))],
        out_specs=pl.BlockSpec((1, 4, tm, width), lambda b, i: (b, 0, i, 0)),
        out_shape=jax.ShapeDtypeStruct((bsz, 4, h, width), BF16),
        compiler_params=_params("arbitrary", "arbitrary"),
        name="hy_fwd",
    )(*mats, u3, tt, tmid)


def _hy_inv_kernel(ce_ref, seg_ref, cog_ref, sog_ref, y_ref, g_ref, u_ref, bias_ref, o_ref, *scr,
                   tm, natural, order):
    i = pl.program_id(1)
    rows = pl.ds(pl.multiple_of(i * tm, tm), tm)
    ye = _dot(ce_ref[rows, :], y_ref[0, 0]) + _dot(seg_ref[rows, :], y_ref[0, 2])
    yo = _dot(cog_ref[rows, :], y_ref[0, 1]) + _dot(sog_ref[rows, :], y_ref[0, 3])
    bias = bias_ref[order:order + 1, :]
    ze = g_ref[0, 0].astype(F32) * (ye + u_ref[0, 0].astype(F32) * bias)
    zo = g_ref[0, 1].astype(F32) * (yo + u_ref[0, 1].astype(F32) * bias)
    if natural:
        scr_ref, = scr
        for sl in range(ze.shape[1] // LANES):
            c = slice(sl * LANES, (sl + 1) * LANES)
            scr_ref[sl, pl.ds(0, tm, stride=2), :] = ze[:, c]
            scr_ref[sl, pl.ds(1, tm, stride=2), :] = zo[:, c]
        for sl in range(ze.shape[1] // LANES):
            o_ref[0, :, sl * LANES:(sl + 1) * LANES] = scr_ref[sl].astype(o_ref.dtype)
    else:
        o_ref[0, 0] = ze.astype(o_ref.dtype)
        o_ref[0, 1] = zo.astype(o_ref.dtype)


def _hy_inv(mats, y4, g4, g_col, u4, u_col, bias, order, width, natural, tm=512):
    bsz, _, h, _ = y4.shape
    par = lambda col: pl.BlockSpec((1, 2, tm, width), lambda b, i: (b, 0, i, col))
    if natural:
        out_spec = pl.BlockSpec((1, 2 * tm, width), lambda b, i: (b, i, 0))
        out_shape = jax.ShapeDtypeStruct((bsz, 2 * h, width), BF16)
        scratch = [pltpu.VMEM((width // LANES, 2 * tm, LANES), F32)]
    else:
        out_spec = par(0)
        out_shape = jax.ShapeDtypeStruct((bsz, 2, h, width), BF16)
        scratch = []
    return pl.pallas_call(
        functools.partial(_hy_inv_kernel, tm=tm, natural=natural, order=order),
        grid=(bsz, h // tm),
        in_specs=[_resident((h, h))] * 4
                 + [pl.BlockSpec((1, 4, h, width), lambda b, i: (b, 0, 0, 0)),
                    par(g_col), par(u_col),
                    pl.BlockSpec(bias.shape, lambda b, i: (0, 0))],
        out_specs=out_spec,
        out_shape=out_shape,
        scratch_shapes=scratch,
        compiler_params=_params("arbitrary", "arbitrary"),
        name="hy_inv",
    )(*mats, y4, g4, u4, bias)


def _cf_kernel(*refs, step_rows, n_cast):
    (am_ref, ap_ref, an_ref, bm_ref, bp_ref, bn_ref, w_ref, cb_ref, lg_ref, lb_ref) = refs[:10]
    cast_in = refs[10:10 + n_cast]
    o_ref = refs[10 + n_cast]
    cast_out = refs[11 + n_cast:11 + 2 * n_cast]
    us_ref, cs_ref, wb_ref = refs[11 + 2 * n_cast:]
    width = wb_ref.shape[-1]
    nsl = width // LANES
    j = pl.program_id(1)

    for src, dst in zip(cast_in, cast_out):
        dst[...] = src[...].astype(dst.dtype)

    @pl.when((pl.program_id(0) == 0) & (j == 0))
    def _():
        for t in range(CF_KERNEL):
            wb_ref[t] = jnp.broadcast_to(w_ref[t:t + 1, :], (8, width))

    glu = lambda a, b: a[0].astype(F32) * _sigmoid(b[0].astype(F32))
    u_prev = jnp.where(j == 0, 0.0, glu(ap_ref, bp_ref))
    u_main = glu(am_ref, bm_ref)
    u_next = jnp.where(j == pl.num_programs(1) - 1, 0.0, glu(an_ref, bn_ref))
    for sl in range(nsl):
        c = slice(sl * LANES, (sl + 1) * LANES)
        us_ref[sl, 0:CF_PAD, :] = u_prev[:, c]
        us_ref[sl, CF_PAD:CF_PAD + step_rows, :] = u_main[:, c]
        us_ref[sl, CF_PAD + step_rows:, :] = u_next[:, c]

    half = CF_KERNEL // 2
    nph = 4
    prow = step_rows // nph
    for sl in range(nsl):
        c = slice(sl * LANES, (sl + 1) * LANES)
        accs = [jnp.zeros((prow // 8, 8, LANES), F32) + cb_ref[:, c][None]] * nph
        for r in range(CF_KERNEL + nph - 1):
            tap = us_ref[sl, pl.ds(CF_PAD - half + r, prow, stride=nph), :].reshape(prow // 8, 8, LANES)
            for p in range(nph):
                if 0 <= r - p < CF_KERNEL:
                    accs[p] = accs[p] + wb_ref[r - p, :, c][None] * tap
        for p in range(nph):
            cs_ref[sl, pl.ds(p, prow, stride=nph), :] = accs[p].reshape(prow, LANES)

    tot = cs_ref[0]
    for sl in range(1, nsl):
        tot = tot + cs_ref[sl]
    mu = jnp.sum(tot, axis=-1, keepdims=True) * (1.0 / width)
    sq = None
    for sl in range(nsl):
        cen = cs_ref[sl] - mu
        sq = cen * cen if sq is None else sq + cen * cen
    rstd = lax.rsqrt(jnp.sum(sq, axis=-1, keepdims=True) * (1.0 / width) + EPS)
    for sl in range(nsl):
        c = slice(sl * LANES, (sl + 1) * LANES)
        y = (cs_ref[sl] - mu) * rstd * lg_ref[:, c] + lb_ref[:, c]
        o_ref[0, :, c] = (y * _sigmoid(y)).astype(o_ref.dtype)


def _cf_conv(proj3, a_col, b_col, w, cb, lg, lb, width, casts, step_rows=512):
    bsz, seq, _ = proj3.shape
    nsteps = seq // step_rows
    total = bsz * nsteps
    nblk = step_rows // CF_PAD
    last_blk = seq // CF_PAD - 1
    vec = pl.BlockSpec((1, width), lambda b, j: (0, 0))
    main = lambda col: pl.BlockSpec((1, step_rows, width), lambda b, j: (b, j, col))
    prev = lambda col: pl.BlockSpec((1, CF_PAD, width), lambda b, j: (b, jnp.maximum(j * nblk - 1, 0), col))
    nxt = lambda col: pl.BlockSpec((1, CF_PAD, width),
                                   lambda b, j: (b, jnp.minimum((j + 1) * nblk, last_blk), col))

    def cast_spec(c):
        n = next(n for n in (total, total // 2, total // 4) if c.shape[0] % (16 * n) == 0)
        every = total // n
        return pl.BlockSpec((c.shape[0] // n, c.shape[1]), lambda b, j: ((b * nsteps + j) // every, 0))
    cast_specs = [cast_spec(c) for c in casts]
    outs = pl.pallas_call(
        functools.partial(_cf_kernel, step_rows=step_rows, n_cast=len(casts)),
        grid=(bsz, nsteps),
        in_specs=[main(a_col), prev(a_col), nxt(a_col), main(b_col), prev(b_col), nxt(b_col),
                  pl.BlockSpec((CF_KERNEL, width), lambda b, j: (0, 0)),
                  vec, vec, vec] + cast_specs,
        out_specs=[pl.BlockSpec((1, step_rows, width), lambda b, j: (b, j, 0))] + cast_specs,
        out_shape=[jax.ShapeDtypeStruct((bsz, seq, width), BF16)]
                  + [jax.ShapeDtypeStruct(c.shape, BF16) for c in casts],
        scratch_shapes=[pltpu.VMEM((width // LANES, step_rows + 2 * CF_PAD, LANES), F32),
                        pltpu.VMEM((width // LANES, step_rows, LANES), F32),
                        pltpu.VMEM((CF_KERNEL, 8, width), F32)],
        compiler_params=_params("arbitrary", "arbitrary"),
        name="cf_conv",
    )(proj3, proj3, proj3, proj3, proj3, proj3, w, cb, lg, lb, *casts)
    return outs[0], outs[1:]


def _merge_kernel(ya_ref, yb_ref, ga0_ref, ga1_ref, gb0_ref, gb1_ref, x_ref, pa_ref, pb_ref, wo_ref,
                  g1_ref, x1_ref):
    a = _dot(ya_ref[...], pa_ref[...])
    b = _dot(yb_ref[...], pb_ref[...])
    w = ga0_ref.shape[1]
    m0 = _sigmoid(ga0_ref[...].astype(F32)) * a[:, :w] + _sigmoid(gb0_ref[...].astype(F32)) * b[:, :w]
    m1 = _sigmoid(ga1_ref[...].astype(F32)) * a[:, w:] + _sigmoid(gb1_ref[...].astype(F32)) * b[:, w:]
    o = _dot(m0.astype(BF16), wo_ref[0:w, :]) + _dot(m1.astype(BF16), wo_ref[w:2 * w, :])
    x1_ref[...] = x_ref[...] + o * lax.rsqrt(jnp.mean(o * o, axis=-1, keepdims=True) + EPS) * g1_ref[...]


def _merge(ya, yb, proj, ga_col, gb_col, x2d, pa, pb, wo, g1, tm=256):
    m, d = x2d.shape
    wa = ya.shape[1]
    assert d == 2 * wa
    const = lambda shape: pl.BlockSpec(shape, lambda i: (0, 0))
    gate = lambda col: pl.BlockSpec((tm, wa), lambda i: (i, col))
    return pl.pallas_call(
        _merge_kernel,
        grid=(m // tm,),
        in_specs=[pl.BlockSpec((tm, wa), lambda i: (i, 0)),
                  pl.BlockSpec((tm, wa), lambda i: (i, 0)),
                  gate(ga_col), gate(ga_col + 1), gate(gb_col), gate(gb_col + 1),
                  pl.BlockSpec((tm, d), lambda i: (i, 0)),
                  const((wa, d)), const((wa, d)), const((d, d)), const((1, d))],
        out_specs=pl.BlockSpec((tm, d), lambda i: (i, 0)),
        out_shape=jax.ShapeDtypeStruct((m, d), F32),
        compiler_params=_params("parallel"),
        name="merge",
    )(ya, yb, proj, proj, proj, proj, x2d, pa, pb, wo, g1)


def _ffn_kernel(x1_ref, gpre_ref, wg_ref, wu_ref, wd_ref, gpost_ref, o_ref, h_ref):
    j = pl.program_id(1)

    @pl.when(j == 0)
    def _():
        x1 = x1_ref[...]
        ms = jnp.mean(x1 * x1, axis=-1, keepdims=True)
        h_ref[...] = (x1 * lax.rsqrt(ms + EPS) * gpre_ref[...]).astype(BF16)
        o_ref[...] = jnp.zeros_like(o_ref)

    h = h_ref[...]
    gate = _dot(h, wg_ref[...])
    up = _dot(h, wu_ref[...])
    act = (gate * _sigmoid(gate) * up).astype(BF16)
    o_ref[...] += _dot(act, wd_ref[...])

    @pl.when(j == pl.num_programs(1) - 1)
    def _():
        a = o_ref[...]
        o_ref[...] = x1_ref[...] + a * lax.rsqrt(jnp.mean(a * a, axis=-1, keepdims=True) + EPS) * gpost_ref[...]


def _ffn(x1, gpre, wgu, wd, gpost, tm=1024, th=512):
    m, d = x1.shape
    hidden = wd.shape[0]
    nh = hidden // th
    return pl.pallas_call(
        _ffn_kernel,
        grid=(m // tm, nh),
        in_specs=[pl.BlockSpec((tm, d), lambda i, j: (i, 0)),
                  pl.BlockSpec((1, d), lambda i, j: (0, 0)),
                  pl.BlockSpec((d, th), lambda i, j: (0, j)),
                  pl.BlockSpec((d, th), lambda i, j: (0, nh + j)),
                  pl.BlockSpec((th, d), lambda i, j: (j, 0)),
                  pl.BlockSpec((1, d), lambda i, j: (0, 0))],
        out_specs=pl.BlockSpec((tm, d), lambda i, j: (i, 0)),
        out_shape=jax.ShapeDtypeStruct((m, d), F32),
        scratch_shapes=[pltpu.VMEM((tm, d), BF16)],
        compiler_params=_params("parallel", "arbitrary", vmem_limit_bytes=V7X_VMEM_LIMIT_LARGE_BYTES),
        name="ffn",
    )(x1, gpre, wgu, wgu, wd, gpost)


def kernel(x, mix_pre_g, w_in, hy_conv_w, hy_conv_b, hy_filt_w1, hy_filt_b1, hy_filt_fr1, hy_filt_w2,
           hy_filt_b2, hy_filt_fr2, hy_filt_w3, hy_bias, hy_proj, cf_dw_w, cf_dw_b, cf_ln_g, cf_ln_b,
           cf_proj, w_out, mix_post_g, ffn_pre_g, ffn_w_gu, ffn_w_down, ffn_post_g):
    bsz, seq, d = x.shape
    depth = w_in.shape[0]
    hw = hy_proj.shape[1]
    cw = cf_proj.shape[1]
    assert hw == cw and d % hw == 0
    row = lambda v: v.reshape(1, -1)

    assert depth == 1
    x2d = x.reshape(bsz * seq, d)
    for l in range(depth):
        ce, co, sef, sof, seg, cog, sog = _dft_tables(seq)
        s, dm, tmid, w_in_b = _filt_mlp(seq, hw, hy_filt_w1[l], row(hy_filt_b1[l]), row(hy_filt_fr1[l]),
                                        hy_filt_w2[l], row(hy_filt_b2[l]), row(hy_filt_fr2[l]),
                                        hy_filt_w3[l], w_in[l], d // 2)
        tt, w_in_a = _filt_dft(ce, co, sef, sof, s, dm, w_in[l], d // 2)
        proj = _in_proj(x2d, row(mix_pre_g[l]), w_in_a, w_in_b)
        proj3 = proj.reshape(bsz, seq, -1)
        cf_a_col, cf_b_col = 3, 4
        ga_col, gb_col = 5, 5 + d // hw

        hy = _hy_conv3(proj3, hy_conv_w[l], row(hy_conv_b[l]), 3 * hw)
        hy4 = hy.reshape(bsz, 2, seq // 2, 3 * hw)
        fwd, inv = (ce, co, sef, sof), (ce, seg, cog, sog)
        y1 = _hy_fwd(fwd, hy, 0, tt, tmid, 0, hw)
        z4 = _hy_inv(inv, y1, hy4, 1, hy4, 0, hy_bias[l], 0, hw, natural=False)
        y2 = _hy_fwd(fwd, z4.reshape(bsz, seq, hw), 0, tt, tmid, 1, hw)
        y_a = _hy_inv(inv, y2, hy4, 2, z4, 0, hy_bias[l], 1, hw, natural=True)

        y_b, (hy_proj_bf, cf_proj_bf, w_out_bf, w_gu_bf, w_down_bf) = _cf_conv(
            proj3, cf_a_col, cf_b_col, cf_dw_w[l], row(cf_dw_b[l]), row(cf_ln_g[l]), row(cf_ln_b[l]), cw,
            [hy_proj[l], cf_proj[l], w_out[l], ffn_w_gu[l], ffn_w_down[l]])

        x1 = _merge(y_a.reshape(bsz * seq, hw), y_b.reshape(bsz * seq, cw), proj, ga_col, gb_col, x2d,
                    hy_proj_bf, cf_proj_bf, w_out_bf, row(mix_post_g[l]))
        x2d = _ffn(x1, row(ffn_pre_g[l]), w_gu_bf, w_down_bf, row(ffn_post_g[l]))
    return x2d.reshape(bsz, seq, d)
```
